```python
import math
import jax, jax.numpy as jnp
from jax import lax
import numpy as np

D_MODEL = 1024
BATCH = 2
SEQ = 8192
DEPTH = 1

MIX_WIDTH = D_MODEL
ATTN_WIDTH = MIX_WIDTH // 2
POOL_WIDTH = MIX_WIDTH - ATTN_WIDTH

HEAD_DIM = 64
N_HEADS = ATTN_WIDTH // HEAD_DIM
N_KV_HEADS = 2
GROUP = N_HEADS // N_KV_HEADS
KV_WIDTH = N_KV_HEADS * HEAD_DIM
WINDOW = 128
BLOCK = 128
ROPE_THETA = 10000.0

POOL_WINDOWS = (2, 4, 8, 16)
N_POOL_GROUPS = len(POOL_WINDOWS)
POOL_GROUP_WIDTH = POOL_WIDTH // N_POOL_GROUPS

IN_PROJ_WIDTH = ATTN_WIDTH + 2 * KV_WIDTH + POOL_WIDTH

FFN_MULT_OF = 256
D_FF = ((8 * D_MODEL // 3 + FFN_MULT_OF - 1) // FFN_MULT_OF) * FFN_MULT_OF

N_MOD = 6
RMS_EPS = 1e-6

kernel_name = "hymba_pool_swa_sink_adaln_block"


def rmsnorm(x, g):
    xf = x.astype(jnp.float32)
    y = xf * lax.rsqrt(jnp.mean(xf * xf, axis=-1, keepdims=True) + RMS_EPS)
    return (y * g.astype(jnp.float32)).astype(x.dtype)


def modulate(h, shift, scale):
    return h * (1.0 + scale[:, None, :]) + shift[:, None, :]


def apply_rope(t, positions):
    half = HEAD_DIM // 2
    inv_freq = ROPE_THETA ** (-jnp.arange(half, dtype=jnp.float32) * (2.0 / HEAD_DIM))
    ang = positions.astype(jnp.float32)[:, :, None] * inv_freq[None, None, :]
    cos = jnp.cos(ang)[:, :, None, :]
    sin = jnp.sin(ang)[:, :, None, :]
    tf = t.astype(jnp.float32)
    t1, t2 = tf[..., :half], tf[..., half:]
    out = jnp.concatenate([t1 * cos - t2 * sin, t2 * cos + t1 * sin], axis=-1)
    return out.astype(t.dtype)


def sliding_window_attention_with_sinks(q, k, v, sinks):
    b, s = q.shape[0], q.shape[1]
    nb = s // BLOCK
    qb = q.reshape(b, nb, BLOCK, N_KV_HEADS, GROUP, HEAD_DIM)

    def with_prev_block(t):
        tb = t.reshape(b, nb, BLOCK, N_KV_HEADS, HEAD_DIM)
        prev = jnp.pad(tb, ((0, 0), (1, 0), (0, 0), (0, 0), (0, 0)))[:, :nb]
        return jnp.concatenate([prev, tb], axis=2)

    kk = with_prev_block(k)
    vv = with_prev_block(v)

    scale = 1.0 / math.sqrt(HEAD_DIM)
    logits = jnp.einsum('bnqhgd,bnkhd->bnhgqk', qb.astype(jnp.float32), kk.astype(jnp.float32)) * scale

    qi = jnp.arange(BLOCK)[:, None]
    kj = jnp.arange(2 * BLOCK)[None, :]
    rel = kj - BLOCK - qi
    band = (rel <= 0) & (rel > -WINDOW)
    not_pad = (jnp.arange(nb)[:, None] > 0) | (jnp.arange(2 * BLOCK)[None, :] >= BLOCK)
    mask = band[None, :, :] & not_pad[:, None, :]
    logits = jnp.where(mask[None, :, None, None, :, :], logits, -jnp.inf)

    sink = sinks.astype(jnp.float32).reshape(N_KV_HEADS, GROUP)[None, None, :, :, None, None]
    m = jnp.maximum(jnp.max(logits, axis=-1, keepdims=True), sink)
    p = jnp.exp(logits - m)
    denom = jnp.sum(p, axis=-1, keepdims=True) + jnp.exp(sink - m)
    out = jnp.einsum('bnhgqk,bnkhd->bnqhgd', p / denom, vv.astype(jnp.float32))
    return out.reshape(b, s, N_HEADS * HEAD_DIM).astype(q.dtype)


def multiscale_pool_mixer(u, w_pool, pool_scale):
    s = u.shape[1]
    outs = []
    for gi, w in enumerate(POOL_WINDOWS):
        ug = u[..., gi * POOL_GROUP_WIDTH:(gi + 1) * POOL_GROUP_WIDTH]
        uf = ug.astype(jnp.float32)
        cs = jnp.cumsum(uf, axis=1)
        shifted = jnp.pad(cs, ((0, 0), (w, 0), (0, 0)))[:, :s]
        count = jnp.minimum(jnp.arange(s) + 1, w).astype(jnp.float32)
        mean = (cs - shifted) / count[None, :, None]
        pooled = (mean - uf).astype(u.dtype)
        outs.append(jnp.einsum('bsc,cd->bsd', pooled, w_pool[gi]))
    return jnp.concatenate(outs, axis=-1) * pool_scale


def setup_inputs(seed: int = 0) -> dict:
    key = jax.random.key(seed)
    ks = jax.random.split(key, 20)
    f32 = jnp.float32
    x = jax.random.normal(ks[0], (BATCH, SEQ, D_MODEL), f32)
    c = jax.random.normal(ks[1], (BATCH, D_MODEL), f32)
    offsets = jax.random.randint(ks[2], (BATCH, 1), 0, 1024, dtype=jnp.int32)
    positions = offsets + jnp.arange(SEQ, dtype=jnp.int32)[None, :]
    w_ada = jax.random.normal(ks[3], (D_MODEL, N_MOD * D_MODEL), f32) * (D_MODEL ** -0.5) * 0.5
    b_ada = jax.random.normal(ks[4], (N_MOD * D_MODEL,), f32) * 0.02
    norm1 = 1.0 + 0.05 * jax.random.normal(ks[5], (D_MODEL,), f32)
    w_in = jax.random.normal(ks[6], (D_MODEL, IN_PROJ_WIDTH), f32) * (D_MODEL ** -0.5)
    sinks = jax.random.normal(ks[7], (N_HEADS,), f32) * 0.5
    w_pool = jax.random.normal(ks[8], (N_POOL_GROUPS, POOL_GROUP_WIDTH, POOL_GROUP_WIDTH), f32) * (POOL_GROUP_WIDTH ** -0.5)
    pool_scale = 1.0 + 0.1 * jax.random.normal(ks[9], (POOL_WIDTH,), f32)
    w_out = jax.random.normal(ks[10], (MIX_WIDTH, D_MODEL), f32) * (MIX_WIDTH ** -0.5)
    norm2 = 1.0 + 0.05 * jax.random.normal(ks[11], (D_MODEL,), f32)
    w_gate = jax.random.normal(ks[12], (D_MODEL, D_FF), f32) * (D_MODEL ** -0.5)
    w_up = jax.random.normal(ks[13], (D_MODEL, D_FF), f32) * (D_MODEL ** -0.5)
    w_down = jax.random.normal(ks[14], (D_FF, D_MODEL), f32) * (D_FF ** -0.5)
    norm_f = 1.0 + 0.05 * jax.random.normal(ks[15], (D_MODEL,), f32)
    return {"x": x, "c": c, "positions": positions, "w_ada": w_ada, "b_ada": b_ada,
            "norm1": norm1, "w_in": w_in, "sinks": sinks, "w_pool": w_pool,
            "pool_scale": pool_scale, "w_out": w_out, "norm2": norm2,
            "w_gate": w_gate, "w_up": w_up, "w_down": w_down, "norm_f": norm_f}


def reference(x, c, positions, w_ada, b_ada, norm1, w_in, sinks, w_pool, pool_scale,
              w_out, norm2, w_gate, w_up, w_down, norm_f):
    b, s, _ = x.shape
    mod = jax.nn.silu(c) @ w_ada + b_ada
    shift1, scale1, gate1, shift2, scale2, gate2 = jnp.split(mod, N_MOD, axis=-1)

    for _ in range(DEPTH):
        h = modulate(rmsnorm(x, norm1), shift1, scale1)
        u = h @ w_in
        q = u[..., :ATTN_WIDTH].reshape(b, s, N_HEADS, HEAD_DIM)
        k = u[..., ATTN_WIDTH:ATTN_WIDTH + KV_WIDTH].reshape(b, s, N_KV_HEADS, HEAD_DIM)
        v = u[..., ATTN_WIDTH + KV_WIDTH:ATTN_WIDTH + 2 * KV_WIDTH].reshape(b, s, N_KV_HEADS, HEAD_DIM)
        u_pool = u[..., ATTN_WIDTH + 2 * KV_WIDTH:]

        q = apply_rope(q, positions)
        k = apply_rope(k, positions)
        attn_out = sliding_window_attention_with_sinks(q, k, v, sinks)
        pool_out = multiscale_pool_mixer(u_pool, w_pool, pool_scale)

        mixed = jnp.concatenate([attn_out, pool_out], axis=-1) @ w_out
        x = x + gate1[:, None, :] * mixed

        h2 = modulate(rmsnorm(x, norm2), shift2, scale2)
        ff = (jax.nn.silu(h2 @ w_gate) * (h2 @ w_up)) @ w_down
        x = x + gate2[:, None, :] * ff

    return rmsnorm(x, norm_f)
```

```python
import functools
import math

import jax
import jax.numpy as jnp
from jax import lax
from jax.experimental import pallas as pl
from jax.experimental.pallas import tpu as pltpu

F32 = jnp.float32
BF16 = jnp.bfloat16

D_MODEL = 1024
HEAD_DIM = 64
N_HEADS = 8
N_KV_HEADS = 2
GROUP = N_HEADS // N_KV_HEADS
ATTN_WIDTH = N_HEADS * HEAD_DIM
KV_WIDTH = N_KV_HEADS * HEAD_DIM
POOL_WINDOWS = (2, 4, 8, 16)
POOL_GROUP_WIDTH = 128
POOL_WIDTH = POOL_GROUP_WIDTH * len(POOL_WINDOWS)
IN_PROJ_WIDTH = ATTN_WIDTH + 2 * KV_WIDTH + POOL_WIDTH
WINDOW = 128
ROPE_THETA = 10000.0
N_MOD = 6
RMS_EPS = 1e-6
HALF = HEAD_DIM // 2

LANES = 128
POOL_HALO = 16
VMEM_LIMIT_BYTES = 56 * 1024 * 1024

TOKEN_TILE = 512
FF_CHUNK = 256


def _const_spec(shape):
    zeros = (0,) * len(shape)
    return pl.BlockSpec(shape, lambda *_: zeros, pipeline_mode=pl.Buffered(1))


def _rmsnorm(x, g):
    ms = jnp.mean(x * x, axis=-1, keepdims=True)
    return x * lax.rsqrt(ms + RMS_EPS) * g


def _mod_kernel(c_ref, w_ref, b_ref, o_ref):
    c = c_ref[...]
    sc = c * jax.nn.sigmoid(c)
    o_ref[...] = jnp.dot(sc, w_ref[...], preferred_element_type=F32,
                         precision=lax.Precision.HIGHEST) + b_ref[...]


def _adaln_mod(c, w_ada, b_ada):
    b = c.shape[0]
    rows = 8
    c8 = jnp.pad(c, ((0, rows - b), (0, 0)))
    out = pl.pallas_call(
        _mod_kernel,
        grid=(N_MOD,),
        in_specs=[
            pl.BlockSpec((rows, D_MODEL), lambda j: (0, 0)),
            pl.BlockSpec((D_MODEL, D_MODEL), lambda j: (0, j)),
            pl.BlockSpec((1, D_MODEL), lambda j: (0, j)),
        ],
        out_specs=pl.BlockSpec((rows, D_MODEL), lambda j: (0, j)),
        out_shape=jax.ShapeDtypeStruct((rows, N_MOD * D_MODEL), F32),
        compiler_params=pltpu.CompilerParams(dimension_semantics=("arbitrary",)),
        name="adaln_mod",
    )(c8, w_ada, b_ada.reshape(1, -1))
    return out[:b].reshape(b, N_MOD, D_MODEL)


def _trig_kernel(pos_ref, f_ref, cos_ref, sin_ref):
    ang = pos_ref[...].astype(F32) * f_ref[...]
    cos_ref[...] = jnp.cos(ang)
    sin_ref[...] = jnp.sin(ang)


def _rope_tables(positions):
    b, s = positions.shape
    inv_freq = ROPE_THETA ** (-jnp.arange(HALF, dtype=F32) * (2.0 / HEAD_DIM))
    per_row = LANES // HALF
    rows = b * s // per_row
    pos_rep = jnp.repeat(positions.reshape(rows, per_row), HALF, axis=1)
    freq = jnp.tile(inv_freq, per_row).reshape(1, LANES)
    block_rows = 1024
    cos, sin = pl.pallas_call(
        _trig_kernel,
        grid=(rows // block_rows,),
        in_specs=[
            pl.BlockSpec((block_rows, LANES), lambda i: (i, 0)),
            pl.BlockSpec((1, LANES), lambda i: (0, 0)),
        ],
        out_specs=[pl.BlockSpec((block_rows, LANES), lambda i: (i, 0))] * 2,
        out_shape=[jax.ShapeDtypeStruct((rows, LANES), F32)] * 2,
        compiler_params=pltpu.CompilerParams(dimension_semantics=("arbitrary",)),
        name="rope_table",
    )(pos_rep, freq)
    return cos.reshape(b, s, HALF), sin.reshape(b, s, HALF)


def _inproj_kernel(x_ref, mod_ref, n1_ref, w_ref, cos_ref, sin_ref,
                   q_ref, k_ref, v_ref, up_ref):
    x = x_ref[0]
    shift = mod_ref[0, 0:1, :]
    scale = mod_ref[0, 1:2, :]
    h = _rmsnorm(x, n1_ref[...]) * (1.0 + scale) + shift
    u = jnp.dot(h.astype(BF16), w_ref[...], preferred_element_type=F32)

    c32 = cos_ref[0]
    s32 = sin_ref[0]
    cos128 = jnp.concatenate([c32, c32, c32, c32], axis=-1)
    sin128 = jnp.concatenate([-s32, s32, -s32, s32], axis=-1)
    lane = lax.broadcasted_iota(jnp.int32, (1, LANES), 1)
    first_half = (lane % HEAD_DIM) < HALF

    def rope(t):
        partner = jnp.where(first_half,
                            pltpu.roll(t, LANES - HALF, 1),
                            pltpu.roll(t, HALF, 1))
        return t * cos128 + partner * sin128

    q_scale = 1.0 / math.sqrt(HEAD_DIM)
    for j in range(ATTN_WIDTH // LANES):
        q_ref[0, :, j * LANES:(j + 1) * LANES] = (
            rope(u[:, j * LANES:(j + 1) * LANES]) * q_scale).astype(BF16)

    kr = rope(u[:, ATTN_WIDTH:ATTN_WIDTH + KV_WIDTH])
    vv = u[:, ATTN_WIDTH + KV_WIDTH:ATTN_WIDTH + 2 * KV_WIDTH]
    for g in range(N_KV_HEADS):
        kg = kr[:, g * HEAD_DIM:(g + 1) * HEAD_DIM]
        vg = vv[:, g * HEAD_DIM:(g + 1) * HEAD_DIM]
        k_ref[0, :, g * LANES:(g + 1) * LANES] = jnp.concatenate([kg, kg], axis=-1).astype(BF16)
        v_ref[0, :, g * LANES:(g + 1) * LANES] = jnp.concatenate([vg, vg], axis=-1).astype(BF16)
    up_ref[0] = u[:, ATTN_WIDTH + 2 * KV_WIDTH:]


def _in_proj(x, mod, norm1, w_in, cos, sin):
    b, s, d = x.shape
    t = TOKEN_TILE
    tok = lambda width: pl.BlockSpec((1, t, width), lambda bi, i: (bi, i, 0))
    return pl.pallas_call(
        _inproj_kernel,
        grid=(b, s // t),
        in_specs=[
            tok(d),
            pl.BlockSpec((1, N_MOD, d), lambda bi, i: (bi, 0, 0)),
            _const_spec((1, d)),
            _const_spec((d, IN_PROJ_WIDTH)),
            tok(HALF),
            tok(HALF),
        ],
        out_specs=[tok(ATTN_WIDTH), tok(2 * KV_WIDTH), tok(2 * KV_WIDTH), tok(POOL_WIDTH)],
        out_shape=[
            jax.ShapeDtypeStruct((b, s, ATTN_WIDTH), BF16),
            jax.ShapeDtypeStruct((b, s, 2 * KV_WIDTH), BF16),
            jax.ShapeDtypeStruct((b, s, 2 * KV_WIDTH), BF16),
            jax.ShapeDtypeStruct((b, s, POOL_WIDTH), F32),
        ],
        compiler_params=pltpu.CompilerParams(
            dimension_semantics=("arbitrary", "arbitrary"), vmem_limit_bytes=VMEM_LIMIT_BYTES),
        name="in_proj",
    )(x, mod, norm1.reshape(1, d), w_in.astype(BF16), cos, sin)


def _mixer_kernel(sinks_ref, q_ref, kc_ref, kp_ref, vc_ref, vp_ref, upc_ref, upp_ref,
                  x_ref, mod_ref, wpool_ref, pscale_ref, wout_ref,
                  o_ref, mix_ref, ext_ref):
    t = q_ref.shape[1]
    i = pl.program_id(1)
    first = i == 0

    qi = lax.broadcasted_iota(jnp.int32, (WINDOW, 2 * WINDOW), 0)
    kj = lax.broadcasted_iota(jnp.int32, (WINDOW, 2 * WINDOW), 1)
    rel = kj - WINDOW - qi
    band = (rel <= 0) & (rel > -WINDOW)
    band_first = band & (kj >= jnp.where(first, WINDOW, 0))
    lane = lax.broadcasted_iota(jnp.int32, (1, LANES), 1)
    low_lanes = lane < HEAD_DIM

    for j in range(t // WINDOW):
        rows = slice(j * WINDOW, (j + 1) * WINDOW)
        if j == 0:
            k_prev, v_prev, mask = kp_ref[0], vp_ref[0], band_first
        else:
            prev_rows = slice((j - 1) * WINDOW, j * WINDOW)
            k_prev, v_prev, mask = kc_ref[0, prev_rows, :], vc_ref[0, prev_rows, :], band
        kk = jnp.concatenate([k_prev, kc_ref[0, rows, :]], axis=0)
        vv = jnp.concatenate([v_prev, vc_ref[0, rows, :]], axis=0)
        for g in range(N_KV_HEADS):
            kd = kk[:, g * LANES:(g + 1) * LANES]
            vd = vv[:, g * LANES:(g + 1) * LANES]
            zero = jnp.zeros_like(kd)
            k_bd = jnp.concatenate([jnp.where(low_lanes, kd, zero),
                                    jnp.where(low_lanes, zero, kd)], axis=0)
            v_bd = jnp.concatenate([jnp.where(low_lanes, vd, zero),
                                    jnp.where(low_lanes, zero, vd)], axis=0)
            for c in range(GROUP // 2):
                chunk = g * (GROUP // 2) + c
                cols = slice(chunk * LANES, (chunk + 1) * LANES)
                qc = q_ref[0, rows, cols]
                logits = lax.dot_general(qc, k_bd, (((1,), (1,)), ((), ())),
                                         preferred_element_type=F32)
                probs, inv_den = [], []
                for hh in range(2):
                    sink = sinks_ref[2 * chunk + hh]
                    l = jnp.where(mask, logits[:, hh * 2 * WINDOW:(hh + 1) * 2 * WINDOW], -jnp.inf)
                    m = jnp.maximum(jnp.max(l, axis=-1, keepdims=True), sink)
                    p = jnp.exp(l - m)
                    den = jnp.sum(p, axis=-1, keepdims=True) + jnp.exp(sink - m)
                    probs.append(p.astype(BF16))
                    inv_den.append(1.0 / den)
                pv = jnp.dot(jnp.concatenate(probs, axis=-1), v_bd,
                             preferred_element_type=F32)
                pv = pv * jnp.where(low_lanes, inv_den[0], inv_den[1])
                mix_ref[rows, cols] = pv.astype(BF16)

    halo = upp_ref[0]
    ext_ref[0:POOL_HALO, :] = jnp.where(first, jnp.zeros_like(halo), halo)
    ext_ref[POOL_HALO:, :] = upc_ref[0]
    pos_in_seq = i * t + lax.broadcasted_iota(jnp.int32, (t, 1), 0)
    for gi, w in enumerate(POOL_WINDOWS):
        cols = slice(gi * POOL_GROUP_WIDTH, (gi + 1) * POOL_GROUP_WIDTH)
        tok = ext_ref[POOL_HALO:, cols]
        total = tok
        for back in range(1, w):
            total = total + ext_ref[POOL_HALO - back:POOL_HALO - back + t, cols]
        count = jnp.minimum(pos_in_seq + 1, w).astype(F32)
        pooled = (total / count - tok).astype(BF16)
        po = jnp.dot(pooled, wpool_ref[gi], preferred_element_type=F32) * pscale_ref[:, cols]
        mix_ref[:, ATTN_WIDTH + gi * POOL_GROUP_WIDTH:ATTN_WIDTH + (gi + 1) * POOL_GROUP_WIDTH] = (
            po.astype(BF16))

    mixed = jnp.dot(mix_ref[...], wout_ref[...], preferred_element_type=F32)
    gate = mod_ref[0, 2:3, :]
    o_ref[0] = x_ref[0] + gate * mixed


def _mixer(q, k, v, up, x, mod, sinks, w_pool, pool_scale, w_out):
    b, s, d = x.shape
    t = TOKEN_TILE
    blocks_per_tile = t // WINDOW
    halos_per_tile = t // POOL_HALO
    tok = lambda width: pl.BlockSpec((1, t, width), lambda bi, i, *_: (bi, i, 0))
    prev_kv = pl.BlockSpec(
        (1, WINDOW, 2 * KV_WIDTH),
        lambda bi, i, *_: (bi, jnp.maximum(i * blocks_per_tile - 1, 0), 0))
    prev_pool = pl.BlockSpec(
        (1, POOL_HALO, POOL_WIDTH),
        lambda bi, i, *_: (bi, jnp.maximum(i * halos_per_tile - 1, 0), 0))
    const = lambda shape: pl.BlockSpec(shape, lambda *_: (0,) * len(shape),
                                       pipeline_mode=pl.Buffered(1))
    grid_spec = pltpu.PrefetchScalarGridSpec(
        num_scalar_prefetch=1,
        grid=(b, s // t),
        in_specs=[
            tok(ATTN_WIDTH),
            tok(2 * KV_WIDTH), prev_kv,
            tok(2 * KV_WIDTH), prev_kv,
            tok(POOL_WIDTH), prev_pool,
            tok(d),
            pl.BlockSpec((1, N_MOD, d), lambda bi, i, *_: (bi, 0, 0)),
            const((len(POOL_WINDOWS), POOL_GROUP_WIDTH, POOL_GROUP_WIDTH)),
            const((1, POOL_WIDTH)),
            const((ATTN_WIDTH + POOL_WIDTH, d)),
        ],
        out_specs=tok(d),
        scratch_shapes=[
            pltpu.VMEM((t, ATTN_WIDTH + POOL_WIDTH), BF16),
            pltpu.VMEM((t + POOL_HALO, POOL_WIDTH), F32),
        ],
    )
    return pl.pallas_call(
        _mixer_kernel,
        grid_spec=grid_spec,
        out_shape=jax.ShapeDtypeStruct((b, s, d), F32),
        compiler_params=pltpu.CompilerParams(
            dimension_semantics=("arbitrary", "arbitrary"), vmem_limit_bytes=VMEM_LIMIT_BYTES),
        name="mixer",
    )(sinks, q, k, k, v, v, up, up, x, mod, w_pool.astype(BF16),
      pool_scale.reshape(1, -1), w_out.astype(BF16))


def _ffn_kernel(x_ref, mod_ref, n2_ref, nf_ref, wg_ref, wu_ref, wd_ref, o_ref, act_ref):
    x = x_ref[0]
    shift = mod_ref[0, 3:4, :]
    scale = mod_ref[0, 4:5, :]
    gate = mod_ref[0, 5:6, :]
    h = (_rmsnorm(x, n2_ref[...]) * (1.0 + scale) + shift).astype(BF16)
    d_ff = wg_ref.shape[1]
    for n in range(d_ff // FF_CHUNK):
        cols = slice(n * FF_CHUNK, (n + 1) * FF_CHUNK)
        g = jnp.dot(h, wg_ref[:, cols], preferred_element_type=F32)
        u = jnp.dot(h, wu_ref[:, cols], preferred_element_type=F32)
        act_ref[:, cols] = (g * jax.nn.sigmoid(g) * u).astype(BF16)
    ff = jnp.dot(act_ref[...], wd_ref[...], preferred_element_type=F32)
    o_ref[0] = _rmsnorm(x + gate * ff, nf_ref[...])


def _ffn(x, mod, norm2, norm_f, w_gate, w_up, w_down):
    b, s, d = x.shape
    d_ff = w_gate.shape[1]
    assert d_ff % FF_CHUNK == 0
    t = TOKEN_TILE
    tok = pl.BlockSpec((1, t, d), lambda bi, i: (bi, i, 0))
    return pl.pallas_call(
        _ffn_kernel,
        grid=(b, s // t),
        in_specs=[
            tok,
            pl.BlockSpec((1, N_MOD, d), lambda bi, i: (bi, 0, 0)),
            _const_spec((1, d)),
            _const_spec((1, d)),
            _const_spec((d, d_ff)),
            _const_spec((d, d_ff)),
            _const_spec((d_ff, d)),
        ],
        out_specs=tok,
        out_shape=jax.ShapeDtypeStruct((b, s, d), F32),
        scratch_shapes=[pltpu.VMEM((t, d_ff), BF16)],
        compiler_params=pltpu.CompilerParams(
            dimension_semantics=("arbitrary", "arbitrary"), vmem_limit_bytes=VMEM_LIMIT_BYTES),
        name="ffn",
    )(x, mod, norm2.reshape(1, d), norm_f.reshape(1, d),
      w_gate.astype(BF16), w_up.astype(BF16), w_down.astype(BF16))


def kernel(x, c, positions, w_ada, b_ada, norm1, w_in, sinks, w_pool, pool_scale,
           w_out, norm2, w_gate, w_up, w_down, norm_f):
    b, s, d = x.shape
    assert d == D_MODEL and s % TOKEN_TILE == 0 and TOKEN_TILE % WINDOW == 0
    mod = _adaln_mod(c, w_ada, b_ada)
    cos, sin = _rope_tables(positions)
    q, k, v, up = _in_proj(x, mod, norm1, w_in, cos, sin)
    x1 = _mixer(q, k, v, up, x, mod, sinks, w_pool, pool_scale, w_out)
    return _ffn(x1, mod, norm2, norm_f, w_gate, w_up, w_down)
```

```python
import functools
import math

import jax
import jax.numpy as jnp
from jax import lax
from jax.experimental import pallas as pl
from jax.experimental.pallas import tpu as pltpu

F32 = jnp.float32
BF16 = jnp.bfloat16

D_MODEL = 1024
HEAD_DIM = 64
N_HEADS = 8
N_KV_HEADS = 2
GROUP = N_HEADS // N_KV_HEADS
ATTN_WIDTH = N_HEADS * HEAD_DIM
KV_WIDTH = N_KV_HEADS * HEAD_DIM
POOL_WINDOWS = (2, 4, 8, 16)
POOL_GROUP_WIDTH = 128
POOL_WIDTH = POOL_GROUP_WIDTH * len(POOL_WINDOWS)
IN_PROJ_WIDTH = ATTN_WIDTH + 2 * KV_WIDTH + POOL_WIDTH
WINDOW = 128
ROPE_THETA = 10000.0
N_MOD = 6
RMS_EPS = 1e-6
HALF = HEAD_DIM // 2

LANES = 128
POOL_HALO = 16
VMEM_LIMIT_BYTES = 56 * 1024 * 1024

TOKEN_TILE = 512
FF_CHUNK = 256
ATTN_SKEW = 3


def _const_spec(shape):
    zeros = (0,) * len(shape)
    return pl.BlockSpec(shape, lambda *_: zeros, pipeline_mode=pl.Buffered(1))


def _rmsnorm(x, g):
    ms = jnp.mean(x * x, axis=-1, keepdims=True)
    return x * lax.rsqrt(ms + RMS_EPS) * g


def _mod_kernel(c_ref, w_ref, b_ref, o_ref):
    c = c_ref[...]
    sc = c * jax.nn.sigmoid(c)
    o_ref[...] = jnp.dot(sc, w_ref[...], preferred_element_type=F32,
                         precision=lax.Precision.HIGHEST) + b_ref[...]


def _adaln_mod(c, w_ada, b_ada):
    b = c.shape[0]
    rows = 8
    c8 = jnp.pad(c, ((0, rows - b), (0, 0)))
    out = pl.pallas_call(
        _mod_kernel,
        grid=(N_MOD,),
        in_specs=[
            pl.BlockSpec((rows, D_MODEL), lambda j: (0, 0)),
            pl.BlockSpec((D_MODEL, D_MODEL), lambda j: (0, j)),
            pl.BlockSpec((1, D_MODEL), lambda j: (0, j)),
        ],
        out_specs=pl.BlockSpec((rows, D_MODEL), lambda j: (0, j)),
        out_shape=jax.ShapeDtypeStruct((rows, N_MOD * D_MODEL), F32),
        compiler_params=pltpu.CompilerParams(dimension_semantics=("arbitrary",)),
        name="adaln_mod",
    )(c8, w_ada, b_ada.reshape(1, -1))
    return out[:b].reshape(b, N_MOD, D_MODEL)


def _trig_kernel(pos_ref, f_ref, cos_ref, sin_ref):
    ang = pos_ref[...].astype(F32) * f_ref[...]
    cos_ref[...] = jnp.cos(ang)
    sin_ref[...] = jnp.sin(ang)


def _rope_tables(positions):
    b, s = positions.shape
    inv_freq = ROPE_THETA ** (-jnp.arange(HALF, dtype=F32) * (2.0 / HEAD_DIM))
    per_row = LANES // HALF
    rows = b * s // per_row
    pos_rep = jnp.repeat(positions.reshape(rows, per_row), HALF, axis=1)
    freq = jnp.tile(inv_freq, per_row).reshape(1, LANES)
    block_rows = 1024
    cos, sin = pl.pallas_call(
        _trig_kernel,
        grid=(rows // block_rows,),
        in_specs=[
            pl.BlockSpec((block_rows, LANES), lambda i: (i, 0)),
            pl.BlockSpec((1, LANES), lambda i: (0, 0)),
        ],
        out_specs=[pl.BlockSpec((block_rows, LANES), lambda i: (i, 0))] * 2,
        out_shape=[jax.ShapeDtypeStruct((rows, LANES), F32)] * 2,
        compiler_params=pltpu.CompilerParams(dimension_semantics=("arbitrary",)),
        name="rope_table",
    )(pos_rep, freq)
    return cos.reshape(b, s, HALF), sin.reshape(b, s, HALF)


def _inproj_kernel(x_ref, mod_ref, n1_ref, w_ref, cos_ref, sin_ref,
                   q_ref, k_ref, v_ref, up_ref):
    x = x_ref[0]
    shift = mod_ref[0, 0:1, :]
    scale = mod_ref[0, 1:2, :]
    h = _rmsnorm(x, n1_ref[...]) * (1.0 + scale) + shift
    u = jnp.dot(h.astype(BF16), w_ref[...], preferred_element_type=F32)

    c32 = cos_ref[0]
    s32 = sin_ref[0]
    cos128 = jnp.concatenate([c32, c32, c32, c32], axis=-1)
    sin128 = jnp.concatenate([-s32, s32, -s32, s32], axis=-1)
    lane = lax.broadcasted_iota(jnp.int32, (1, LANES), 1)
    first_half = (lane % HEAD_DIM) < HALF

    def rope(t):
        partner = jnp.where(first_half,
                            pltpu.roll(t, LANES - HALF, 1),
                            pltpu.roll(t, HALF, 1))
        return t * cos128 + partner * sin128

    q_scale = 1.0 / math.sqrt(HEAD_DIM)
    for j in range(ATTN_WIDTH // LANES):
        q_ref[0, :, j * LANES:(j + 1) * LANES] = (
            rope(u[:, j * LANES:(j + 1) * LANES]) * q_scale).astype(BF16)

    kr = rope(u[:, ATTN_WIDTH:ATTN_WIDTH + KV_WIDTH])
    vv = u[:, ATTN_WIDTH + KV_WIDTH:ATTN_WIDTH + 2 * KV_WIDTH]
    for g in range(N_KV_HEADS):
        kg = kr[:, g * HEAD_DIM:(g + 1) * HEAD_DIM]
        vg = vv[:, g * HEAD_DIM:(g + 1) * HEAD_DIM]
        k_ref[0, :, g * LANES:(g + 1) * LANES] = jnp.concatenate([kg, kg], axis=-1).astype(BF16)
        v_ref[0, :, g * LANES:(g + 1) * LANES] = jnp.concatenate([vg, vg], axis=-1).astype(BF16)
    up_ref[0] = u[:, ATTN_WIDTH + 2 * KV_WIDTH:]


def _in_proj(x, mod, norm1, w_in, cos, sin):
    b, s, d = x.shape
    t = TOKEN_TILE
    tok = lambda width: pl.BlockSpec((1, t, width), lambda bi, i: (bi, i, 0))
    return pl.pallas_call(
        _inproj_kernel,
        grid=(b, s // t),
        in_specs=[
            tok(d),
            pl.BlockSpec((1, N_MOD, d), lambda bi, i: (bi, 0, 0)),
            _const_spec((1, d)),
            _const_spec((d, IN_PROJ_WIDTH)),
            tok(HALF),
            tok(HALF),
        ],
        out_specs=[tok(ATTN_WIDTH), tok(2 * KV_WIDTH), tok(2 * KV_WIDTH), tok(POOL_WIDTH)],
        out_shape=[
            jax.ShapeDtypeStruct((b, s, ATTN_WIDTH), BF16),
            jax.ShapeDtypeStruct((b, s, 2 * KV_WIDTH), BF16),
            jax.ShapeDtypeStruct((b, s, 2 * KV_WIDTH), BF16),
            jax.ShapeDtypeStruct((b, s, POOL_WIDTH), F32),
        ],
        compiler_params=pltpu.CompilerParams(
            dimension_semantics=("arbitrary", "arbitrary"), vmem_limit_bytes=VMEM_LIMIT_BYTES),
        name="in_proj",
    )(x, mod, norm1.reshape(1, d), w_in.astype(BF16), cos, sin)


def _mixer_kernel(sinks_ref, q_ref, kc_ref, kp_ref, vc_ref, vp_ref, upc_ref, upp_ref,
                  x_ref, mod_ref, wpool_ref, pscale_ref, wout_ref,
                  o_ref, mix_ref, ext_ref):
    t = q_ref.shape[1]
    i = pl.program_id(1)
    first = i == 0

    qi = lax.broadcasted_iota(jnp.int32, (WINDOW, 2 * WINDOW), 0)
    kj = lax.broadcasted_iota(jnp.int32, (WINDOW, 2 * WINDOW), 1)
    rel = kj - WINDOW - qi
    band = (rel <= 0) & (rel > -WINDOW)
    band_first = band & (kj >= jnp.where(first, WINDOW, 0))
    lane = lax.broadcasted_iota(jnp.int32, (1, LANES), 1)
    low_lanes = lane < HEAD_DIM

    gate = mod_ref[0, 2:3, :]

    halo = upp_ref[0]
    ext_ref[0:POOL_HALO, :] = jnp.where(first, jnp.zeros_like(halo), halo)
    ext_ref[POOL_HALO:, :] = upc_ref[0]
    pos_in_seq = i * t + lax.broadcasted_iota(jnp.int32, (t, 1), 0)
    for gi, w in enumerate(POOL_WINDOWS):
        cols = slice(gi * POOL_GROUP_WIDTH, (gi + 1) * POOL_GROUP_WIDTH)
        tok = ext_ref[POOL_HALO:, cols]
        total = tok
        for back in range(1, w):
            total = total + ext_ref[POOL_HALO - back:POOL_HALO - back + t, cols]
        count = jnp.minimum(pos_in_seq + 1, w).astype(F32)
        pooled = (total / count - tok).astype(BF16)
        po = jnp.dot(pooled, wpool_ref[gi], preferred_element_type=F32) * pscale_ref[:, cols]
        mix_ref[:, ATTN_WIDTH + gi * POOL_GROUP_WIDTH:ATTN_WIDTH + (gi + 1) * POOL_GROUP_WIDTH] = (
            po.astype(BF16))

    n_blocks = t // WINDOW
    chunks = N_HEADS // 2
    bodies = [(j, chunk) for j in range(n_blocks) for chunk in range(chunks)]
    kv_cache = {}

    def block_diag_kv(j, g):
        if (j, g) not in kv_cache:
            rows = slice(j * WINDOW, (j + 1) * WINDOW)
            lanes_g = slice(g * LANES, (g + 1) * LANES)
            if j == 0:
                k_prev, v_prev = kp_ref[0, :, lanes_g], vp_ref[0, :, lanes_g]
            else:
                prev_rows = slice((j - 1) * WINDOW, j * WINDOW)
                k_prev, v_prev = kc_ref[0, prev_rows, lanes_g], vc_ref[0, prev_rows, lanes_g]
            kd = jnp.concatenate([k_prev, kc_ref[0, rows, lanes_g]], axis=0)
            vd = jnp.concatenate([v_prev, vc_ref[0, rows, lanes_g]], axis=0)
            zero = jnp.zeros_like(kd)
            k_bd = jnp.concatenate([jnp.where(low_lanes, kd, zero),
                                    jnp.where(low_lanes, zero, kd)], axis=0)
            v_bd = jnp.concatenate([jnp.where(low_lanes, vd, zero),
                                    jnp.where(low_lanes, zero, vd)], axis=0)
            kv_cache[(j, g)] = (k_bd, v_bd)
        return kv_cache[(j, g)]

    def scores(j, chunk):
        k_bd, _ = block_diag_kv(j, chunk // (GROUP // 2))
        qc = q_ref[0, j * WINDOW:(j + 1) * WINDOW, chunk * LANES:(chunk + 1) * LANES]
        return lax.dot_general(qc, k_bd, (((1,), (1,)), ((), ())),
                               preferred_element_type=F32)

    def softmax(j, chunk, logits):
        mask = band_first if j == 0 else band
        probs, inv_den = [], []
        for hh in range(2):
            sink = sinks_ref[2 * chunk + hh]
            l = jnp.where(mask, logits[:, hh * 2 * WINDOW:(hh + 1) * 2 * WINDOW], -jnp.inf)
            m = jnp.maximum(jnp.max(l, axis=-1, keepdims=True), sink)
            p = jnp.exp(l - m)
            den = jnp.sum(p, axis=-1, keepdims=True) + jnp.exp(sink - m)
            probs.append(p.astype(BF16))
            inv_den.append(1.0 / den)
        return jnp.concatenate(probs, axis=-1), jnp.where(low_lanes, inv_den[0], inv_den[1])

    def values(j, chunk, probs, inv_den):
        _, v_bd = block_diag_kv(j, chunk // (GROUP // 2))
        pv = jnp.dot(probs, v_bd, preferred_element_type=F32)
        mix_ref[j * WINDOW:(j + 1) * WINDOW, chunk * LANES:(chunk + 1) * LANES] = (
            (pv * inv_den).astype(BF16))

    def out_proj(j):
        rows = slice(j * WINDOW, (j + 1) * WINDOW)
        mixed = jnp.dot(mix_ref[rows, :], wout_ref[...], preferred_element_type=F32)
        o_ref[0, rows, :] = x_ref[0, rows, :] + gate * mixed

    logits, probs = {}, {}
    n = len(bodies)
    for step in range(n + ATTN_SKEW):
        if step < n:
            logits[step] = scores(*bodies[step])
        if 0 <= step - 1 < n:
            probs[step - 1] = softmax(*bodies[step - 1], logits.pop(step - 1))
        done = step - ATTN_SKEW
        if 0 <= done < n:
            values(*bodies[done], *probs.pop(done))
            j, chunk = bodies[done]
            if chunk == chunks - 1:
                out_proj(j)


def _mixer(q, k, v, up, x, mod, sinks, w_pool, pool_scale, w_out):
    b, s, d = x.shape
    t = TOKEN_TILE
    blocks_per_tile = t // WINDOW
    halos_per_tile = t // POOL_HALO
    tok = lambda width: pl.BlockSpec((1, t, width), lambda bi, i, *_: (bi, i, 0))
    prev_kv = pl.BlockSpec(
        (1, WINDOW, 2 * KV_WIDTH),
        lambda bi, i, *_: (bi, jnp.maximum(i * blocks_per_tile - 1, 0), 0))
    prev_pool = pl.BlockSpec(
        (1, POOL_HALO, POOL_WIDTH),
        lambda bi, i, *_: (bi, jnp.maximum(i * halos_per_tile - 1, 0), 0))
    const = lambda shape: pl.BlockSpec(shape, lambda *_: (0,) * len(shape),
                                       pipeline_mode=pl.Buffered(1))
    grid_spec = pltpu.PrefetchScalarGridSpec(
        num_scalar_prefetch=1,
        grid=(b, s // t),
        in_specs=[
            tok(ATTN_WIDTH),
            tok(2 * KV_WIDTH), prev_kv,
            tok(2 * KV_WIDTH), prev_kv,
            tok(POOL_WIDTH), prev_pool,
            tok(d),
            pl.BlockSpec((1, N_MOD, d), lambda bi, i, *_: (bi, 0, 0)),
            const((len(POOL_WINDOWS), POOL_GROUP_WIDTH, POOL_GROUP_WIDTH)),
            const((1, POOL_WIDTH)),
            const((ATTN_WIDTH + POOL_WIDTH, d)),
        ],
        out_specs=tok(d),
        scratch_shapes=[
            pltpu.VMEM((t, ATTN_WIDTH + POOL_WIDTH), BF16),
            pltpu.VMEM((t + POOL_HALO, POOL_WIDTH), F32),
        ],
    )
    return pl.pallas_call(
        _mixer_kernel,
        grid_spec=grid_spec,
        out_shape=jax.ShapeDtypeStruct((b, s, d), F32),
        compiler_params=pltpu.CompilerParams(
            dimension_semantics=("arbitrary", "arbitrary"), vmem_limit_bytes=VMEM_LIMIT_BYTES),
        name="mixer",
    )(sinks, q, k, k, v, v, up, up, x, mod, w_pool.astype(BF16),
      pool_scale.reshape(1, -1), w_out.astype(BF16))


def _ffn_kernel(x_ref, mod_ref, n2_ref, nf_ref, wg_ref, wu_ref, wd_ref, o_ref, act_ref):
    x = x_ref[0]
    shift = mod_ref[0, 3:4, :]
    scale = mod_ref[0, 4:5, :]
    gate = mod_ref[0, 5:6, :]
    h = (_rmsnorm(x, n2_ref[...]) * (1.0 + scale) + shift).astype(BF16)
    d_ff = wg_ref.shape[1]
    for n in range(d_ff // FF_CHUNK):
        cols = slice(n * FF_CHUNK, (n + 1) * FF_CHUNK)
        g = jnp.dot(h, wg_ref[:, cols], preferred_element_type=F32)
        u = jnp.dot(h, wu_ref[:, cols], preferred_element_type=F32)
        act_ref[:, cols] = (g * jax.nn.sigmoid(g) * u).astype(BF16)
    ff = jnp.dot(act_ref[...], wd_ref[...], preferred_element_type=F32)
    o_ref[0] = _rmsnorm(x + gate * ff, nf_ref[...])


def _ffn(x, mod, norm2, norm_f, w_gate, w_up, w_down):
    b, s, d = x.shape
    d_ff = w_gate.shape[1]
    assert d_ff % FF_CHUNK == 0
    t = TOKEN_TILE
    tok = pl.BlockSpec((1, t, d), lambda bi, i: (bi, i, 0))
    return pl.pallas_call(
        _ffn_kernel,
        grid=(b, s // t),
        in_specs=[
            tok,
            pl.BlockSpec((1, N_MOD, d), lambda bi, i: (bi, 0, 0)),
            _const_spec((1, d)),
            _const_spec((1, d)),
            _const_spec((d, d_ff)),
            _const_spec((d, d_ff)),
            _const_spec((d_ff, d)),
        ],
        out_specs=tok,
        out_shape=jax.ShapeDtypeStruct((b, s, d), F32),
        scratch_shapes=[pltpu.VMEM((t, d_ff), BF16)],
        compiler_params=pltpu.CompilerParams(
            dimension_semantics=("arbitrary", "arbitrary"), vmem_limit_bytes=VMEM_LIMIT_BYTES),
        name="ffn",
    )(x, mod, norm2.reshape(1, d), norm_f.reshape(1, d),
      w_gate.astype(BF16), w_up.astype(BF16), w_down.astype(BF16))


def kernel(x, c, positions, w_ada, b_ada, norm1, w_in, sinks, w_pool, pool_scale,
           w_out, norm2, w_gate, w_up, w_down, norm_f):
    b, s, d = x.shape
    assert d == D_MODEL and s % TOKEN_TILE == 0 and TOKEN_TILE % WINDOW == 0
    mod = _adaln_mod(c, w_ada, b_ada)
    cos, sin = _rope_tables(positions)
    q, k, v, up = _in_proj(x, mod, norm1, w_in, cos, sin)
    x1 = _mixer(q, k, v, up, x, mod, sinks, w_pool, pool_scale, w_out)
    return _ffn(x1, mod, norm2, norm_f, w_gate, w_up, w_down)
```

```python
import functools
import math

import jax
import jax.numpy as jnp
from jax import lax
from jax.experimental import pallas as pl
from jax.experimental.pallas import tpu as pltpu

F32 = jnp.float32
BF16 = jnp.bfloat16

D_MODEL = 1024
HEAD_DIM = 64
N_HEADS = 8
N_KV_HEADS = 2
GROUP = N_HEADS // N_KV_HEADS
ATTN_WIDTH = N_HEADS * HEAD_DIM
KV_WIDTH = N_KV_HEADS * HEAD_DIM
POOL_WINDOWS = (2, 4, 8, 16)
POOL_GROUP_WIDTH = 128
POOL_WIDTH = POOL_GROUP_WIDTH * len(POOL_WINDOWS)
IN_PROJ_WIDTH = ATTN_WIDTH + 2 * KV_WIDTH + POOL_WIDTH
WINDOW = 128
ROPE_THETA = 10000.0
N_MOD = 6
RMS_EPS = 1e-6
HALF = HEAD_DIM // 2

LANES = 128
POOL_HALO = 16
VMEM_LIMIT_BYTES = 56 * 1024 * 1024

TOKEN_TILE = 512
FF_CHUNK = 256
DOWN_CHUNK = 256
ATTN_SKEW = 3


def _const_spec(shape):
    zeros = (0,) * len(shape)
    return pl.BlockSpec(shape, lambda *_: zeros, pipeline_mode=pl.Buffered(1))


def _rms_scale(x):
    return lax.rsqrt(jnp.mean(x * x, axis=-1, keepdims=True) + RMS_EPS)


def _interleave(front, back):
    keyed = [((k + 0.5) / len(front), 0, k, f) for k, f in enumerate(front)]
    keyed += [((k + 0.5) / len(back), 1, k, f) for k, f in enumerate(back)]
    return [f for *_, f in sorted(keyed, key=lambda e: e[:3])]


def _mod_kernel(c_ref, w_ref, b_ref, o_ref):
    c = c_ref[...]
    sc = c * jax.nn.sigmoid(c)
    o_ref[...] = jnp.dot(sc, w_ref[...], preferred_element_type=F32,
                         precision=lax.Precision.HIGHEST) + b_ref[...]


def _adaln_mod(c, w_ada, b_ada):
    b = c.shape[0]
    rows = 8
    c8 = jnp.pad(c, ((0, rows - b), (0, 0)))
    out = pl.pallas_call(
        _mod_kernel,
        grid=(N_MOD,),
        in_specs=[
            pl.BlockSpec((rows, D_MODEL), lambda j: (0, 0)),
            pl.BlockSpec((D_MODEL, D_MODEL), lambda j: (0, j)),
            pl.BlockSpec((1, D_MODEL), lambda j: (0, j)),
        ],
        out_specs=pl.BlockSpec((rows, D_MODEL), lambda j: (0, j)),
        out_shape=jax.ShapeDtypeStruct((rows, N_MOD * D_MODEL), F32),
        compiler_params=pltpu.CompilerParams(dimension_semantics=("arbitrary",)),
        name="adaln_mod",
    )(c8, w_ada, b_ada.reshape(1, -1))
    return out[:b].reshape(b, N_MOD, D_MODEL)


def _trig_kernel(pos_ref, f_ref, cos_ref, sin_ref):
    ang = pos_ref[...].astype(F32) * f_ref[...]
    cos = jnp.cos(ang)
    sin = jnp.sin(ang)
    rows = pos_ref.shape[0]
    per_row = LANES // HALF
    for m in range(per_row):
        c32 = cos[:, m * HALF:(m + 1) * HALF]
        s32 = sin[:, m * HALF:(m + 1) * HALF]
        cos_ref[pl.ds(m, rows, stride=per_row), :] = jnp.concatenate([c32, c32, c32, c32], axis=-1)
        sin_ref[pl.ds(m, rows, stride=per_row), :] = jnp.concatenate([-s32, s32, -s32, s32], axis=-1)


def _rope_tables(positions):
    b, s = positions.shape
    inv_freq = ROPE_THETA ** (-jnp.arange(HALF, dtype=F32) * (2.0 / HEAD_DIM))
    per_row = LANES // HALF
    rows = b * s // per_row
    pos_rep = jnp.repeat(positions.reshape(rows, per_row), HALF, axis=1)
    freq = jnp.tile(inv_freq, per_row).reshape(1, LANES)
    block_rows = 512
    return pl.pallas_call(
        _trig_kernel,
        grid=(rows // block_rows,),
        in_specs=[
            pl.BlockSpec((block_rows, LANES), lambda i: (i, 0)),
            pl.BlockSpec((1, LANES), lambda i: (0, 0)),
        ],
        out_specs=[pl.BlockSpec((block_rows * per_row, LANES), lambda i: (i, 0))] * 2,
        out_shape=[jax.ShapeDtypeStruct((b * s, LANES), F32)] * 2,
        compiler_params=pltpu.CompilerParams(dimension_semantics=("arbitrary",)),
        name="rope_table",
    )(pos_rep, freq)


def _layer_kernel(sinks_ref, x_ref, cos_ref, sin_ref, modf_ref, modb_ref, n1_ref, n2_ref, nf_ref,
                  win_ref, wpool_ref, pscale_ref, wout_ref, wg_ref, wu_ref, wd_ref,
                  o_ref,
                  h_ref, q_ref, k_ref, v_ref, ext_ref, mix_ref, x1_ref, h2_ref, act_ref,
                  *, tiles_per_seq):
    t = x_ref.shape[1]
    s = pl.program_id(0)
    n_tiles = pl.num_programs(0) - 1
    tile_f = jnp.minimum(s, n_tiles - 1)
    first = (tile_f % tiles_per_seq) == 0
    slot_f = s % 2
    slot_b = 1 - slot_f

    @pl.when(s == 0)
    def _():
        x1_ref[1] = jnp.zeros((t, D_MODEL), F32)
        k_ref[t:, :] = jnp.zeros((WINDOW, 2 * KV_WIDTH), BF16)
        v_ref[t:, :] = jnp.zeros((WINDOW, 2 * KV_WIDTH), BF16)
        ext_ref[t:, :] = jnp.zeros((POOL_HALO, POOL_WIDTH), F32)

    lane = lax.broadcasted_iota(jnp.int32, (1, LANES), 1)
    low_lanes = lane < HEAD_DIM
    first_half = (lane % HEAD_DIM) < HALF

    def f_norm():
        x = x_ref[0]
        scale = n1_ref[...] * (1.0 + modf_ref[0, 1:2, :])
        h_ref[...] = (x * _rms_scale(x) * scale + modf_ref[0, 0:1, :]).astype(BF16)

    def rope(tile):
        partner = jnp.where(first_half,
                            pltpu.roll(tile, LANES - HALF, 1),
                            pltpu.roll(tile, HALF, 1))
        return tile * cos_ref[...] + partner * sin_ref[...]

    def f_q():
        u = jnp.dot(h_ref[...], win_ref[:, :ATTN_WIDTH], preferred_element_type=F32)
        q_scale = 1.0 / math.sqrt(HEAD_DIM)
        for j in range(ATTN_WIDTH // LANES):
            cols = slice(j * LANES, (j + 1) * LANES)
            q_ref[:, cols] = (rope(u[:, cols]) * q_scale).astype(BF16)

    def f_kv():
        k_ref[0:WINDOW, :] = k_ref[t:, :]
        v_ref[0:WINDOW, :] = v_ref[t:, :]
        u = jnp.dot(h_ref[...], win_ref[:, ATTN_WIDTH:ATTN_WIDTH + 2 * KV_WIDTH],
                    preferred_element_type=F32)
        kr = rope(u[:, :KV_WIDTH])
        vv = u[:, KV_WIDTH:]
        for g in range(N_KV_HEADS):
            kg = kr[:, g * HEAD_DIM:(g + 1) * HEAD_DIM]
            vg = vv[:, g * HEAD_DIM:(g + 1) * HEAD_DIM]
            cols = slice(g * LANES, (g + 1) * LANES)
            k_ref[WINDOW:, cols] = jnp.concatenate([kg, kg], axis=-1).astype(BF16)
            v_ref[WINDOW:, cols] = jnp.concatenate([vg, vg], axis=-1).astype(BF16)

    def f_up():
        halo = ext_ref[t:, :]
        ext_ref[0:POOL_HALO, :] = jnp.where(first, jnp.zeros_like(halo), halo)
        ext_ref[POOL_HALO:, :] = jnp.dot(h_ref[...], win_ref[:, ATTN_WIDTH + 2 * KV_WIDTH:],
                                         preferred_element_type=F32)

    def f_pool(gi, w):
        def task():
            cols = slice(gi * POOL_GROUP_WIDTH, (gi + 1) * POOL_GROUP_WIDTH)
            pos_in_seq = ((tile_f % tiles_per_seq) * t
                          + lax.broadcasted_iota(jnp.int32, (t, 1), 0))
            tok = ext_ref[POOL_HALO:, cols]
            total = tok
            for back in range(1, w):
                total = total + ext_ref[POOL_HALO - back:POOL_HALO - back + t, cols]
            count = jnp.minimum(pos_in_seq + 1, w).astype(F32)
            pooled = (total / count - tok).astype(BF16)
            po = jnp.dot(pooled, wpool_ref[gi], preferred_element_type=F32) * pscale_ref[:, cols]
            mix_ref[:, ATTN_WIDTH + gi * POOL_GROUP_WIDTH:
                    ATTN_WIDTH + (gi + 1) * POOL_GROUP_WIDTH] = po.astype(BF16)
        return task

    n_blocks = t // WINDOW
    chunks = N_HEADS // 2
    bodies = [(j, chunk) for j in range(n_blocks) for chunk in range(chunks)]
    kv_cache, logits, probs = {}, {}, {}

    def band_mask(j):
        qi = lax.broadcasted_iota(jnp.int32, (WINDOW, 2 * WINDOW), 0)
        kj = lax.broadcasted_iota(jnp.int32, (WINDOW, 2 * WINDOW), 1)
        rel = kj - WINDOW - qi
        band = (rel <= 0) & (rel > -WINDOW)
        if j == 0:
            band = band & (kj >= jnp.where(first, WINDOW, 0))
        return band

    def block_diag_kv(j, g):
        if (j, g) not in kv_cache:
            rows = slice(j * WINDOW, (j + 2) * WINDOW)
            cols = slice(g * LANES, (g + 1) * LANES)
            kd = k_ref[rows, cols]
            vd = v_ref[rows, cols]
            zero = jnp.zeros_like(kd)
            k_bd = jnp.concatenate([jnp.where(low_lanes, kd, zero),
                                    jnp.where(low_lanes, zero, kd)], axis=0)
            v_bd = jnp.concatenate([jnp.where(low_lanes, vd, zero),
                                    jnp.where(low_lanes, zero, vd)], axis=0)
            kv_cache[(j, g)] = (k_bd, v_bd)
        return kv_cache[(j, g)]

    def scores(j, chunk):
        k_bd, _ = block_diag_kv(j, chunk // (GROUP // 2))
        qc = q_ref[j * WINDOW:(j + 1) * WINDOW, chunk * LANES:(chunk + 1) * LANES]
        return lax.dot_general(qc, k_bd, (((1,), (1,)), ((), ())),
                               preferred_element_type=F32)

    def softmax(j, chunk, lg):
        mask = band_mask(j)
        ps, inv_den = [], []
        for hh in range(2):
            sink = sinks_ref[2 * chunk + hh]
            l = jnp.where(mask, lg[:, hh * 2 * WINDOW:(hh + 1) * 2 * WINDOW], -jnp.inf)
            m = jnp.maximum(jnp.max(l, axis=-1, keepdims=True), sink)
            p = jnp.exp(l - m)
            den = jnp.sum(p, axis=-1, keepdims=True) + jnp.exp(sink - m)
            ps.append(p.astype(BF16))
            inv_den.append(1.0 / den)
        return jnp.concatenate(ps, axis=-1), jnp.where(low_lanes, inv_den[0], inv_den[1])

    def values(j, chunk, p, inv_den):
        _, v_bd = block_diag_kv(j, chunk // (GROUP // 2))
        pv = jnp.dot(p, v_bd, preferred_element_type=F32)
        mix_ref[j * WINDOW:(j + 1) * WINDOW, chunk * LANES:(chunk + 1) * LANES] = (
            (pv * inv_den).astype(BF16))

    def out_proj(j):
        rows = slice(j * WINDOW, (j + 1) * WINDOW)
        mixed = jnp.dot(mix_ref[rows, :], wout_ref[...], preferred_element_type=F32)
        x1_ref[slot_f, rows, :] = x_ref[0, rows, :] + modf_ref[0, 2:3, :] * mixed

    def f_attn(step):
        def task():
            n = len(bodies)
            if step < n:
                logits[step] = scores(*bodies[step])
            if 0 <= step - 1 < n:
                probs[step - 1] = softmax(*bodies[step - 1], logits.pop(step - 1))
            done = step - ATTN_SKEW
            if 0 <= done < n:
                values(*bodies[done], *probs.pop(done))
                j, chunk = bodies[done]
                if chunk == chunks - 1:
                    out_proj(j)
        return task

    front = [f_norm, f_q, f_kv, f_up]
    front += [f_pool(gi, w) for gi, w in enumerate(POOL_WINDOWS)]
    front += [f_attn(step) for step in range(len(bodies) + ATTN_SKEW)]

    def b_norm():
        x1 = x1_ref[slot_b]
        scale = n2_ref[...] * (1.0 + modb_ref[0, 4:5, :])
        h2_ref[...] = (x1 * _rms_scale(x1) * scale + modb_ref[0, 3:4, :]).astype(BF16)

    def b_gate_up(n):
        def task():
            cols = slice(n * FF_CHUNK, (n + 1) * FF_CHUNK)
            g = jnp.dot(h2_ref[...], wg_ref[:, cols], preferred_element_type=F32)
            u = jnp.dot(h2_ref[...], wu_ref[:, cols], preferred_element_type=F32)
            act_ref[:, cols] = (g * jax.nn.sigmoid(g) * u).astype(BF16)
        return task

    def b_down(n):
        def task():
            cols = slice(n * DOWN_CHUNK, (n + 1) * DOWN_CHUNK)
            ff = jnp.dot(act_ref[...], wd_ref[:, cols], preferred_element_type=F32)
            o_ref[0, :, cols] = x1_ref[slot_b, :, cols] + modb_ref[0, 5:6, cols] * ff
        return task

    def b_final():
        x2 = o_ref[0]
        o_ref[0] = x2 * _rms_scale(x2) * nf_ref[...]

    d_ff = wg_ref.shape[1]
    back = [b_norm]
    back += [b_gate_up(n) for n in range(d_ff // FF_CHUNK)]
    back += [b_down(n) for n in range(D_MODEL // DOWN_CHUNK)]
    back += [b_final]

    for task in _interleave(front, back):
        task()


def _layer(x, mod, cos, sin, sinks, norm1, norm2, norm_f, w_in, w_pool, pool_scale, w_out,
           w_gate, w_up, w_down):
    b, s, d = x.shape
    t = TOKEN_TILE
    d_ff = w_gate.shape[1]
    assert d == D_MODEL and s % t == 0 and t % WINDOW == 0 and d_ff % FF_CHUNK == 0
    tiles_per_seq = s // t
    n_tiles = b * tiles_per_seq

    def front_tile(step):
        return jnp.minimum(step, n_tiles - 1)

    def back_tile(step):
        return jnp.maximum(step - 1, 0)

    def tok_map(tile_of):
        return lambda step, *_: (tile_of(step) // tiles_per_seq, tile_of(step) % tiles_per_seq, 0)

    def mod_map(tile_of):
        return lambda step, *_: (tile_of(step) // tiles_per_seq, 0, 0)

    table_spec = pl.BlockSpec((t, LANES), lambda step, *_: (front_tile(step), 0))
    grid_spec = pltpu.PrefetchScalarGridSpec(
        num_scalar_prefetch=1,
        grid=(n_tiles + 1,),
        in_specs=[
            pl.BlockSpec((1, t, d), tok_map(front_tile)),
            table_spec, table_spec,
            pl.BlockSpec((1, N_MOD, d), mod_map(front_tile)),
            pl.BlockSpec((1, N_MOD, d), mod_map(back_tile)),
            _const_spec((1, d)), _const_spec((1, d)), _const_spec((1, d)),
            _const_spec((d, IN_PROJ_WIDTH)),
            _const_spec((len(POOL_WINDOWS), POOL_GROUP_WIDTH, POOL_GROUP_WIDTH)),
            _const_spec((1, POOL_WIDTH)),
            _const_spec((ATTN_WIDTH + POOL_WIDTH, d)),
            _const_spec((d, d_ff)), _const_spec((d, d_ff)), _const_spec((d_ff, d)),
        ],
        out_specs=pl.BlockSpec((1, t, d), tok_map(back_tile)),
        scratch_shapes=[
            pltpu.VMEM((t, d), BF16),
            pltpu.VMEM((t, ATTN_WIDTH), BF16),
            pltpu.VMEM((t + WINDOW, 2 * KV_WIDTH), BF16),
            pltpu.VMEM((t + WINDOW, 2 * KV_WIDTH), BF16),
            pltpu.VMEM((t + POOL_HALO, POOL_WIDTH), F32),
            pltpu.VMEM((t, ATTN_WIDTH + POOL_WIDTH), BF16),
            pltpu.VMEM((2, t, d), F32),
            pltpu.VMEM((t, d), BF16),
            pltpu.VMEM((t, d_ff), BF16),
        ],
    )
    return pl.pallas_call(
        functools.partial(_layer_kernel, tiles_per_seq=tiles_per_seq),
        grid_spec=grid_spec,
        out_shape=jax.ShapeDtypeStruct((b, s, d), F32),
        compiler_params=pltpu.CompilerParams(
            dimension_semantics=("arbitrary",), vmem_limit_bytes=VMEM_LIMIT_BYTES),
        name="layer",
    )(sinks, x, cos, sin, mod, mod, norm1.reshape(1, d), norm2.reshape(1, d), norm_f.reshape(1, d),
      w_in.astype(BF16), w_pool.astype(BF16), pool_scale.reshape(1, -1), w_out.astype(BF16),
      w_gate.astype(BF16), w_up.astype(BF16), w_down.astype(BF16))


def kernel(x, c, positions, w_ada, b_ada, norm1, w_in, sinks, w_pool, pool_scale,
           w_out, norm2, w_gate, w_up, w_down, norm_f):
    mod = _adaln_mod(c, w_ada, b_ada)
    cos, sin = _rope_tables(positions)
    return _layer(x, mod, cos, sin, sinks, norm1, norm2, norm_f, w_in, w_pool, pool_scale,
                  w_out, w_gate, w_up, w_down)
```

```python
import functools
import math

import jax
import jax.numpy as jnp
from jax import lax
from jax.experimental import pallas as pl
from jax.experimental.pallas import tpu as pltpu

F32 = jnp.float32
BF16 = jnp.bfloat16

D_MODEL = 1024
HEAD_DIM = 64
N_HEADS = 8
N_KV_HEADS = 2
GROUP = N_HEADS // N_KV_HEADS
ATTN_WIDTH = N_HEADS * HEAD_DIM
KV_WIDTH = N_KV_HEADS * HEAD_DIM
POOL_WINDOWS = (2, 4, 8, 16)
POOL_GROUP_WIDTH = 128
POOL_WIDTH = POOL_GROUP_WIDTH * len(POOL_WINDOWS)
IN_PROJ_WIDTH = ATTN_WIDTH + 2 * KV_WIDTH + POOL_WIDTH
WINDOW = 128
ROPE_THETA = 10000.0
N_MOD = 6
RMS_EPS = 1e-6
HALF = HEAD_DIM // 2

LANES = 128
POOL_HALO = 16
VMEM_LIMIT_BYTES = 56 * 1024 * 1024

TOKEN_TILE = 512
FF_CHUNK = 256
DOWN_CHUNK = 256
ATTN_SKEW = 3
CAST_CHUNK_ROWS = 256


def _const_spec(shape):
    zeros = (0,) * len(shape)
    return pl.BlockSpec(shape, lambda *_: zeros, pipeline_mode=pl.Buffered(1))


def _rms_scale(x):
    return lax.rsqrt(jnp.mean(x * x, axis=-1, keepdims=True) + RMS_EPS)


def _interleave(front, back):
    keyed = [((k + 0.5) / len(front), 0, k, f) for k, f in enumerate(front)]
    keyed += [((k + 0.5) / len(back), 1, k, f) for k, f in enumerate(back)]
    return [f for *_, f in sorted(keyed, key=lambda e: e[:3])]


def _mod_kernel(c_ref, w_ref, b_ref, o_ref):
    c = c_ref[...]
    sc = c * jax.nn.sigmoid(c)
    o_ref[...] = jnp.dot(sc, w_ref[...], preferred_element_type=F32,
                         precision=lax.Precision.HIGHEST) + b_ref[...]


def _adaln_mod(c, w_ada, b_ada):
    b = c.shape[0]
    rows = 8
    c8 = jnp.pad(c, ((0, rows - b), (0, 0)))
    out = pl.pallas_call(
        _mod_kernel,
        grid=(N_MOD,),
        in_specs=[
            pl.BlockSpec((rows, D_MODEL), lambda j: (0, 0)),
            pl.BlockSpec((D_MODEL, D_MODEL), lambda j: (0, j)),
            pl.BlockSpec((1, D_MODEL), lambda j: (0, j)),
        ],
        out_specs=pl.BlockSpec((rows, D_MODEL), lambda j: (0, j)),
        out_shape=jax.ShapeDtypeStruct((rows, N_MOD * D_MODEL), F32),
        compiler_params=pltpu.CompilerParams(dimension_semantics=("arbitrary",)),
        name="adaln_mod",
    )(c8, w_ada, b_ada.reshape(1, -1))
    return out[:b].reshape(b, N_MOD, D_MODEL)


def _trig_kernel(pos_ref, f_ref, cos_ref, sin_ref):
    ang = pos_ref[...].astype(F32) * f_ref[...]
    cos = jnp.cos(ang)
    sin = jnp.sin(ang)
    rows = pos_ref.shape[0]
    per_row = LANES // HALF
    for m in range(per_row):
        c32 = cos[:, m * HALF:(m + 1) * HALF]
        s32 = sin[:, m * HALF:(m + 1) * HALF]
        cos_ref[pl.ds(m, rows, stride=per_row), :] = jnp.concatenate([c32, c32, c32, c32], axis=-1)
        sin_ref[pl.ds(m, rows, stride=per_row), :] = jnp.concatenate([-s32, s32, -s32, s32], axis=-1)


def _rope_tables(positions):
    b, s = positions.shape
    inv_freq = ROPE_THETA ** (-jnp.arange(HALF, dtype=F32) * (2.0 / HEAD_DIM))
    per_row = LANES // HALF
    rows = b * s // per_row
    pos_rep = jnp.repeat(positions.reshape(rows, per_row), HALF, axis=1)
    freq = jnp.tile(inv_freq, per_row).reshape(1, LANES)
    block_rows = 512
    return pl.pallas_call(
        _trig_kernel,
        grid=(rows // block_rows,),
        in_specs=[
            pl.BlockSpec((block_rows, LANES), lambda i: (i, 0)),
            pl.BlockSpec((1, LANES), lambda i: (0, 0)),
        ],
        out_specs=[pl.BlockSpec((block_rows * per_row, LANES), lambda i: (i, 0))] * 2,
        out_shape=[jax.ShapeDtypeStruct((b * s, LANES), F32)] * 2,
        compiler_params=pltpu.CompilerParams(dimension_semantics=("arbitrary",)),
        name="rope_table",
    )(pos_rep, freq)


def _load_weights_as_bf16(pairs, stage_ref, sems):
    chunks = [(src, dst, c) for src, dst in pairs
              for c in range(src.shape[0] // CAST_CHUNK_ROWS)]

    def copy(k):
        src, _, c = chunks[k]
        return pltpu.make_async_copy(
            src.at[pl.ds(c * CAST_CHUNK_ROWS, CAST_CHUNK_ROWS), :],
            stage_ref.at[k % 2, :, pl.ds(0, src.shape[1])],
            sems.at[k % 2])

    copy(0).start()
    for k, (src, dst, c) in enumerate(chunks):
        if k + 1 < len(chunks):
            copy(k + 1).start()
        copy(k).wait()
        dst[c * CAST_CHUNK_ROWS:(c + 1) * CAST_CHUNK_ROWS, :] = (
            stage_ref[k % 2, :, 0:src.shape[1]].astype(BF16))


def _layer_kernel(sinks_ref, x_ref, cos_ref, sin_ref, modf_ref, modb_ref, n1_ref, n2_ref, nf_ref,
                  pscale_ref, win_hbm, wpool_hbm, wout_hbm, wg_hbm, wu_hbm, wd_hbm,
                  o_ref,
                  win_ref, wpool_ref, wout_ref, wg_ref, wu_ref, wd_ref, stage_ref, cast_sems,
                  h_ref, q_ref, k_ref, v_ref, ext_ref, mix_ref, x1_ref, h2_ref, act_ref,
                  *, tiles_per_seq):
    t = x_ref.shape[1]
    s = pl.program_id(0)
    n_tiles = pl.num_programs(0) - 1
    tile_f = jnp.minimum(s, n_tiles - 1)
    first = (tile_f % tiles_per_seq) == 0
    slot_f = s % 2
    slot_b = 1 - slot_f

    @pl.when(s == 0)
    def _():
        _load_weights_as_bf16(
            [(win_hbm, win_ref), (wpool_hbm, wpool_ref), (wout_hbm, wout_ref),
             (wg_hbm, wg_ref), (wu_hbm, wu_ref), (wd_hbm, wd_ref)], stage_ref, cast_sems)
        x1_ref[1] = jnp.zeros((t, D_MODEL), F32)
        k_ref[t:, :] = jnp.zeros((WINDOW, 2 * KV_WIDTH), BF16)
        v_ref[t:, :] = jnp.zeros((WINDOW, 2 * KV_WIDTH), BF16)
        ext_ref[t:, :] = jnp.zeros((POOL_HALO, POOL_WIDTH), F32)

    lane = lax.broadcasted_iota(jnp.int32, (1, LANES), 1)
    low_lanes = lane < HEAD_DIM
    first_half = (lane % HEAD_DIM) < HALF

    def f_norm():
        x = x_ref[0]
        scale = n1_ref[...] * (1.0 + modf_ref[0, 1:2, :])
        h_ref[...] = (x * _rms_scale(x) * scale + modf_ref[0, 0:1, :]).astype(BF16)

    def rope(tile):
        partner = jnp.where(first_half,
                            pltpu.roll(tile, LANES - HALF, 1),
                            pltpu.roll(tile, HALF, 1))
        return tile * cos_ref[...] + partner * sin_ref[...]

    def f_q():
        u = jnp.dot(h_ref[...], win_ref[:, :ATTN_WIDTH], preferred_element_type=F32)
        q_scale = 1.0 / math.sqrt(HEAD_DIM)
        for j in range(ATTN_WIDTH // LANES):
            cols = slice(j * LANES, (j + 1) * LANES)
            q_ref[:, cols] = (rope(u[:, cols]) * q_scale).astype(BF16)

    def f_kv():
        k_ref[0:WINDOW, :] = k_ref[t:, :]
        v_ref[0:WINDOW, :] = v_ref[t:, :]
        u = jnp.dot(h_ref[...], win_ref[:, ATTN_WIDTH:ATTN_WIDTH + 2 * KV_WIDTH],
                    preferred_element_type=F32)
        kr = rope(u[:, :KV_WIDTH])
        vv = u[:, KV_WIDTH:]
        for g in range(N_KV_HEADS):
            kg = kr[:, g * HEAD_DIM:(g + 1) * HEAD_DIM]
            vg = vv[:, g * HEAD_DIM:(g + 1) * HEAD_DIM]
            cols = slice(g * LANES, (g + 1) * LANES)
            k_ref[WINDOW:, cols] = jnp.concatenate([kg, kg], axis=-1).astype(BF16)
            v_ref[WINDOW:, cols] = jnp.concatenate([vg, vg], axis=-1).astype(BF16)

    def f_up():
        halo = ext_ref[t:, :]
        ext_ref[0:POOL_HALO, :] = jnp.where(first, jnp.zeros_like(halo), halo)
        ext_ref[POOL_HALO:, :] = jnp.dot(h_ref[...], win_ref[:, ATTN_WIDTH + 2 * KV_WIDTH:],
                                         preferred_element_type=F32)

    def f_pool(gi, w):
        def task():
            cols = slice(gi * POOL_GROUP_WIDTH, (gi + 1) * POOL_GROUP_WIDTH)
            pos_in_seq = ((tile_f % tiles_per_seq) * t
                          + lax.broadcasted_iota(jnp.int32, (t, 1), 0))
            tok = ext_ref[POOL_HALO:, cols]
            total = tok
            for back in range(1, w):
                total = total + ext_ref[POOL_HALO - back:POOL_HALO - back + t, cols]
            count = jnp.minimum(pos_in_seq + 1, w).astype(F32)
            pooled = (total / count - tok).astype(BF16)
            po = jnp.dot(pooled, wpool_ref[cols, :], preferred_element_type=F32) * pscale_ref[:, cols]
            mix_ref[:, ATTN_WIDTH + gi * POOL_GROUP_WIDTH:
                    ATTN_WIDTH + (gi + 1) * POOL_GROUP_WIDTH] = po.astype(BF16)
        return task

    n_blocks = t // WINDOW
    chunks = N_HEADS // 2
    bodies = [(j, chunk) for j in range(n_blocks) for chunk in range(chunks)]
    kv_cache, logits, probs = {}, {}, {}

    def band_mask(j):
        qi = lax.broadcasted_iota(jnp.int32, (WINDOW, 2 * WINDOW), 0)
        kj = lax.broadcasted_iota(jnp.int32, (WINDOW, 2 * WINDOW), 1)
        rel = kj - WINDOW - qi
        band = (rel <= 0) & (rel > -WINDOW)
        if j == 0:
            band = band & (kj >= jnp.where(first, WINDOW, 0))
        return band

    def block_diag_kv(j, g):
        if (j, g) not in kv_cache:
            rows = slice(j * WINDOW, (j + 2) * WINDOW)
            cols = slice(g * LANES, (g + 1) * LANES)
            kd = k_ref[rows, cols]
            vd = v_ref[rows, cols]
            zero = jnp.zeros_like(kd)
            k_bd = jnp.concatenate([jnp.where(low_lanes, kd, zero),
                                    jnp.where(low_lanes, zero, kd)], axis=0)
            v_bd = jnp.concatenate([jnp.where(low_lanes, vd, zero),
                                    jnp.where(low_lanes, zero, vd)], axis=0)
            kv_cache[(j, g)] = (k_bd, v_bd)
        return kv_cache[(j, g)]

    def scores(j, chunk):
        k_bd, _ = block_diag_kv(j, chunk // (GROUP // 2))
        qc = q_ref[j * WINDOW:(j + 1) * WINDOW, chunk * LANES:(chunk + 1) * LANES]
        return lax.dot_general(qc, k_bd, (((1,), (1,)), ((), ())),
                               preferred_element_type=F32)

    def softmax(j, chunk, lg):
        mask = band_mask(j)
        ps, inv_den = [], []
        for hh in range(2):
            sink = sinks_ref[2 * chunk + hh]
            l = jnp.where(mask, lg[:, hh * 2 * WINDOW:(hh + 1) * 2 * WINDOW], -jnp.inf)
            m = jnp.maximum(jnp.max(l, axis=-1, keepdims=True), sink)
            p = jnp.exp(l - m)
            den = jnp.sum(p, axis=-1, keepdims=True) + jnp.exp(sink - m)
            ps.append(p.astype(BF16))
            inv_den.append(1.0 / den)
        return jnp.concatenate(ps, axis=-1), jnp.where(low_lanes, inv_den[0], inv_den[1])

    def values(j, chunk, p, inv_den):
        _, v_bd = block_diag_kv(j, chunk // (GROUP // 2))
        pv = jnp.dot(p, v_bd, preferred_element_type=F32)
        mix_ref[j * WINDOW:(j + 1) * WINDOW, chunk * LANES:(chunk + 1) * LANES] = (
            (pv * inv_den).astype(BF16))

    def out_proj(j):
        rows = slice(j * WINDOW, (j + 1) * WINDOW)
        mixed = jnp.dot(mix_ref[rows, :], wout_ref[...], preferred_element_type=F32)
        x1_ref[slot_f, rows, :] = x_ref[0, rows, :] + modf_ref[0, 2:3, :] * mixed

    def f_attn(step):
        def task():
            n = len(bodies)
            if step < n:
                logits[step] = scores(*bodies[step])
            if 0 <= step - 1 < n:
                probs[step - 1] = softmax(*bodies[step - 1], logits.pop(step - 1))
            done = step - ATTN_SKEW
            if 0 <= done < n:
                values(*bodies[done], *probs.pop(done))
                j, chunk = bodies[done]
                if chunk == chunks - 1:
                    out_proj(j)
        return task

    front = [f_norm, f_q, f_kv, f_up]
    front += [f_pool(gi, w) for gi, w in enumerate(POOL_WINDOWS)]
    front += [f_attn(step) for step in range(len(bodies) + ATTN_SKEW)]

    def b_norm():
        x1 = x1_ref[slot_b]
        scale = n2_ref[...] * (1.0 + modb_ref[0, 4:5, :])
        h2_ref[...] = (x1 * _rms_scale(x1) * scale + modb_ref[0, 3:4, :]).astype(BF16)

    def b_gate_up(n):
        def task():
            cols = slice(n * FF_CHUNK, (n + 1) * FF_CHUNK)
            g = jnp.dot(h2_ref[...], wg_ref[:, cols], preferred_element_type=F32)
            u = jnp.dot(h2_ref[...], wu_ref[:, cols], preferred_element_type=F32)
            act_ref[:, cols] = (g * jax.nn.sigmoid(g) * u).astype(BF16)
        return task

    def b_down(n):
        def task():
            cols = slice(n * DOWN_CHUNK, (n + 1) * DOWN_CHUNK)
            ff = jnp.dot(act_ref[...], wd_ref[:, cols], preferred_element_type=F32)
            o_ref[0, :, cols] = x1_ref[slot_b, :, cols] + modb_ref[0, 5:6, cols] * ff
        return task

    def b_final():
        x2 = o_ref[0]
        o_ref[0] = x2 * _rms_scale(x2) * nf_ref[...]

    d_ff = wg_ref.shape[1]
    back = [b_norm]
    back += [b_gate_up(n) for n in range(d_ff // FF_CHUNK)]
    back += [b_down(n) for n in range(D_MODEL // DOWN_CHUNK)]
    back += [b_final]

    for task in _interleave(front, back):
        task()


def _layer(x, mod, cos, sin, sinks, norm1, norm2, norm_f, w_in, w_pool, pool_scale, w_out,
           w_gate, w_up, w_down):
    b, s, d = x.shape
    t = TOKEN_TILE
    d_ff = w_gate.shape[1]
    assert d == D_MODEL and s % t == 0 and t % WINDOW == 0 and d_ff % FF_CHUNK == 0
    assert d_ff % CAST_CHUNK_ROWS == 0 and d % CAST_CHUNK_ROWS == 0
    tiles_per_seq = s // t
    n_tiles = b * tiles_per_seq

    def front_tile(step):
        return jnp.minimum(step, n_tiles - 1)

    def back_tile(step):
        return jnp.maximum(step - 1, 0)

    def tok_map(tile_of):
        return lambda step, *_: (tile_of(step) // tiles_per_seq, tile_of(step) % tiles_per_seq, 0)

    def mod_map(tile_of):
        return lambda step, *_: (tile_of(step) // tiles_per_seq, 0, 0)

    table_spec = pl.BlockSpec((t, LANES), lambda step, *_: (front_tile(step), 0))
    grid_spec = pltpu.PrefetchScalarGridSpec(
        num_scalar_prefetch=1,
        grid=(n_tiles + 1,),
        in_specs=[
            pl.BlockSpec((1, t, d), tok_map(front_tile)),
            table_spec, table_spec,
            pl.BlockSpec((1, N_MOD, d), mod_map(front_tile)),
            pl.BlockSpec((1, N_MOD, d), mod_map(back_tile)),
            _const_spec((1, d)), _const_spec((1, d)), _const_spec((1, d)),
            _const_spec((1, POOL_WIDTH)),
        ] + [pl.BlockSpec(memory_space=pl.ANY)] * 6,
        out_specs=pl.BlockSpec((1, t, d), tok_map(back_tile)),
        scratch_shapes=[
            pltpu.VMEM((d, IN_PROJ_WIDTH), BF16),
            pltpu.VMEM((POOL_WIDTH, POOL_GROUP_WIDTH), BF16),
            pltpu.VMEM((ATTN_WIDTH + POOL_WIDTH, d), BF16),
            pltpu.VMEM((d, d_ff), BF16),
            pltpu.VMEM((d, d_ff), BF16),
            pltpu.VMEM((d_ff, d), BF16),
            pltpu.VMEM((2, CAST_CHUNK_ROWS, max(d_ff, IN_PROJ_WIDTH)), F32),
            pltpu.SemaphoreType.DMA((2,)),
            pltpu.VMEM((t, d), BF16),
            pltpu.VMEM((t, ATTN_WIDTH), BF16),
            pltpu.VMEM((t + WINDOW, 2 * KV_WIDTH), BF16),
            pltpu.VMEM((t + WINDOW, 2 * KV_WIDTH), BF16),
            pltpu.VMEM((t + POOL_HALO, POOL_WIDTH), F32),
            pltpu.VMEM((t, ATTN_WIDTH + POOL_WIDTH), BF16),
            pltpu.VMEM((2, t, d), F32),
            pltpu.VMEM((t, d), BF16),
            pltpu.VMEM((t, d_ff), BF16),
        ],
    )
    return pl.pallas_call(
        functools.partial(_layer_kernel, tiles_per_seq=tiles_per_seq),
        grid_spec=grid_spec,
        out_shape=jax.ShapeDtypeStruct((b, s, d), F32),
        compiler_params=pltpu.CompilerParams(
            dimension_semantics=("arbitrary",), vmem_limit_bytes=VMEM_LIMIT_BYTES),
        name="layer",
    )(sinks, x, cos, sin, mod, mod, norm1.reshape(1, d), norm2.reshape(1, d), norm_f.reshape(1, d),
      pool_scale.reshape(1, -1), w_in, w_pool.reshape(POOL_WIDTH, POOL_GROUP_WIDTH), w_out,
      w_gate, w_up, w_down)


def kernel(x, c, positions, w_ada, b_ada, norm1, w_in, sinks, w_pool, pool_scale,
           w_out, norm2, w_gate, w_up, w_down, norm_f):
    mod = _adaln_mod(c, w_ada, b_ada)
    cos, sin = _rope_tables(positions)
    return _layer(x, mod, cos, sin, sinks, norm1, norm2, norm_f, w_in, w_pool, pool_scale,
                  w_out, w_gate, w_up, w_down)
```

```python
import functools
import math

import jax
import jax.numpy as jnp
from jax import lax
from jax.experimental import pallas as pl
from jax.experimental.pallas import tpu as pltpu

F32 = jnp.float32
BF16 = jnp.bfloat16

D_MODEL = 1024
HEAD_DIM = 64
N_HEADS = 8
N_KV_HEADS = 2
GROUP = N_HEADS // N_KV_HEADS
ATTN_WIDTH = N_HEADS * HEAD_DIM
KV_WIDTH = N_KV_HEADS * HEAD_DIM
POOL_WINDOWS = (2, 4, 8, 16)
POOL_GROUP_WIDTH = 128
POOL_WIDTH = POOL_GROUP_WIDTH * len(POOL_WINDOWS)
IN_PROJ_WIDTH = ATTN_WIDTH + 2 * KV_WIDTH + POOL_WIDTH
WINDOW = 128
ROPE_THETA = 10000.0
N_MOD = 6
RMS_EPS = 1e-6
HALF = HEAD_DIM // 2

LANES = 128
POOL_HALO = 16
VMEM_LIMIT_BYTES = 56 * 1024 * 1024

TOKEN_TILE = 512
FF_CHUNK = 256
DOWN_CHUNK = 256
ATTN_SKEW = 3
CAST_CHUNK_ROWS = 256


def _const_spec(shape):
    zeros = (0,) * len(shape)
    return pl.BlockSpec(shape, lambda *_: zeros, pipeline_mode=pl.Buffered(1))


def _rms_scale(x):
    return lax.rsqrt(jnp.mean(x * x, axis=-1, keepdims=True) + RMS_EPS)


def _interleave(front, back):
    keyed = [((k + 0.5) / len(front), 0, k, f) for k, f in enumerate(front)]
    keyed += [((k + 0.5) / len(back), 1, k, f) for k, f in enumerate(back)]
    return [f for *_, f in sorted(keyed, key=lambda e: e[:3])]


def _mod_kernel(ct_ref, w_ref, b_ref, o_ref):
    ct = ct_ref[...]
    sc = ct * jax.nn.sigmoid(ct)
    w = w_ref[...]
    for r in range(ct.shape[1]):
        o_ref[0, r:r + 1, :] = jnp.sum(w * sc[:, r:r + 1], axis=0, keepdims=True) + b_ref[...]


def _adaln_mod(c, w_ada, b_ada):
    b, d = c.shape
    return pl.pallas_call(
        _mod_kernel,
        grid=(N_MOD,),
        in_specs=[
            pl.BlockSpec((d, b), lambda j: (0, 0)),
            pl.BlockSpec((d, d), lambda j: (0, j)),
            pl.BlockSpec((1, d), lambda j: (0, j)),
        ],
        out_specs=pl.BlockSpec((1, b, d), lambda j: (j, 0, 0)),
        out_shape=jax.ShapeDtypeStruct((N_MOD, b, d), F32),
        compiler_params=pltpu.CompilerParams(dimension_semantics=("arbitrary",)),
        name="adaln_mod",
    )(c.T, w_ada, b_ada.reshape(1, -1))


def _load_weights_as_bf16(pairs, stage_ref, sems):
    chunks = [(src, dst, c) for src, dst in pairs
              for c in range(src.shape[0] // CAST_CHUNK_ROWS)]

    def copy(k):
        src, _, c = chunks[k]
        return pltpu.make_async_copy(
            src.at[pl.ds(c * CAST_CHUNK_ROWS, CAST_CHUNK_ROWS), :],
            stage_ref.at[k % 2, :, pl.ds(0, src.shape[1])],
            sems.at[k % 2])

    copy(0).start()
    for k, (src, dst, c) in enumerate(chunks):
        if k + 1 < len(chunks):
            copy(k + 1).start()
        copy(k).wait()
        dst[c * CAST_CHUNK_ROWS:(c + 1) * CAST_CHUNK_ROWS, :] = (
            stage_ref[k % 2, :, 0:src.shape[1]].astype(BF16))


def _layer_kernel(sinks_ref, x_ref, pos_ref, freq_ref, mod_ref, n1_ref, n2_ref, nf_ref,
                  pscale_ref, win_hbm, wpool_hbm, wout_hbm, wg_hbm, wu_hbm, wd_hbm,
                  o_ref,
                  win_ref, wpool_ref, wout_ref, wg_ref, wu_ref, wd_ref, stage_ref, cast_sems,
                  cos_ref, sin_ref, h_ref, q_ref, k_ref, v_ref, ext_ref, mix_ref, x1_ref, h2_ref,
                  act_ref, *, tiles_per_seq):
    t = x_ref.shape[1]
    s = pl.program_id(0)
    n_tiles = pl.num_programs(0) - 1
    tile_f = jnp.minimum(s, n_tiles - 1)
    first = (tile_f % tiles_per_seq) == 0
    slot_f = s % 2
    slot_b = 1 - slot_f
    seq_f = tile_f // tiles_per_seq
    seq_b = jnp.maximum(s - 1, 0) // tiles_per_seq

    def mod_row(k, seq):
        return mod_ref[k, pl.ds(seq, 1), :]

    @pl.when(s == 0)
    def _():
        _load_weights_as_bf16(
            [(win_hbm, win_ref), (wpool_hbm, wpool_ref), (wout_hbm, wout_ref),
             (wg_hbm, wg_ref), (wu_hbm, wu_ref), (wd_hbm, wd_ref)], stage_ref, cast_sems)
        x1_ref[1] = jnp.zeros((t, D_MODEL), F32)
        k_ref[t:, :] = jnp.zeros((WINDOW, 2 * KV_WIDTH), BF16)
        v_ref[t:, :] = jnp.zeros((WINDOW, 2 * KV_WIDTH), BF16)
        ext_ref[t:, :] = jnp.zeros((POOL_HALO, POOL_WIDTH), F32)

    lane = lax.broadcasted_iota(jnp.int32, (1, LANES), 1)
    low_lanes = lane < HEAD_DIM
    first_half = (lane % HEAD_DIM) < HALF

    def f_norm():
        x = x_ref[0]
        scale = n1_ref[...] * (1.0 + mod_row(1, seq_f))
        h_ref[...] = (x * _rms_scale(x) * scale + mod_row(0, seq_f)).astype(BF16)

    def f_trig():
        ang = pos_ref[...].astype(F32) * freq_ref[...]
        cos, sin = jnp.cos(ang), jnp.sin(ang)
        per_row = LANES // HALF
        for m in range(per_row):
            c32 = cos[:, m * HALF:(m + 1) * HALF]
            s32 = sin[:, m * HALF:(m + 1) * HALF]
            rows = pl.ds(m, t // per_row, stride=per_row)
            cos_ref[rows, :] = jnp.concatenate([c32, c32, c32, c32], axis=-1)
            sin_ref[rows, :] = jnp.concatenate([-s32, s32, -s32, s32], axis=-1)

    def rope(tile):
        partner = jnp.where(first_half,
                            pltpu.roll(tile, LANES - HALF, 1),
                            pltpu.roll(tile, HALF, 1))
        return tile * cos_ref[...] + partner * sin_ref[...]

    def f_q():
        u = jnp.dot(h_ref[...], win_ref[:, :ATTN_WIDTH], preferred_element_type=F32)
        q_scale = 1.0 / math.sqrt(HEAD_DIM)
        for j in range(ATTN_WIDTH // LANES):
            cols = slice(j * LANES, (j + 1) * LANES)
            q_ref[:, cols] = (rope(u[:, cols]) * q_scale).astype(BF16)

    def f_kv():
        k_ref[0:WINDOW, :] = k_ref[t:, :]
        v_ref[0:WINDOW, :] = v_ref[t:, :]
        u = jnp.dot(h_ref[...], win_ref[:, ATTN_WIDTH:ATTN_WIDTH + 2 * KV_WIDTH],
                    preferred_element_type=F32)
        kr = rope(u[:, :KV_WIDTH])
        vv = u[:, KV_WIDTH:]
        for g in range(N_KV_HEADS):
            kg = kr[:, g * HEAD_DIM:(g + 1) * HEAD_DIM]
            vg = vv[:, g * HEAD_DIM:(g + 1) * HEAD_DIM]
            cols = slice(g * LANES, (g + 1) * LANES)
            k_ref[WINDOW:, cols] = jnp.concatenate([kg, kg], axis=-1).astype(BF16)
            v_ref[WINDOW:, cols] = jnp.concatenate([vg, vg], axis=-1).astype(BF16)

    def f_up():
        halo = ext_ref[t:, :]
        ext_ref[0:POOL_HALO, :] = jnp.where(first, jnp.zeros_like(halo), halo)
        ext_ref[POOL_HALO:, :] = jnp.dot(h_ref[...], win_ref[:, ATTN_WIDTH + 2 * KV_WIDTH:],
                                         preferred_element_type=F32)

    def f_pool(gi, w):
        def task():
            cols = slice(gi * POOL_GROUP_WIDTH, (gi + 1) * POOL_GROUP_WIDTH)
            pos_in_seq = ((tile_f % tiles_per_seq) * t
                          + lax.broadcasted_iota(jnp.int32, (t, 1), 0))
            tok = ext_ref[POOL_HALO:, cols]
            total = tok
            for back in range(1, w):
                total = total + ext_ref[POOL_HALO - back:POOL_HALO - back + t, cols]
            count = jnp.minimum(pos_in_seq + 1, w).astype(F32)
            pooled = (total / count - tok).astype(BF16)
            po = jnp.dot(pooled, wpool_ref[cols, :], preferred_element_type=F32) * pscale_ref[:, cols]
            mix_ref[:, ATTN_WIDTH + gi * POOL_GROUP_WIDTH:
                    ATTN_WIDTH + (gi + 1) * POOL_GROUP_WIDTH] = po.astype(BF16)
        return task

    n_blocks = t // WINDOW
    chunks = N_HEADS // 2
    bodies = [(j, chunk) for j in range(n_blocks) for chunk in range(chunks)]
    kv_cache, logits, probs = {}, {}, {}

    def band_mask(j):
        qi = lax.broadcasted_iota(jnp.int32, (WINDOW, 2 * WINDOW), 0)
        kj = lax.broadcasted_iota(jnp.int32, (WINDOW, 2 * WINDOW), 1)
        rel = kj - WINDOW - qi
        band = (rel <= 0) & (rel > -WINDOW)
        if j == 0:
            band = band & (kj >= jnp.where(first, WINDOW, 0))
        return band

    def block_diag_kv(j, g):
        if (j, g) not in kv_cache:
            rows = slice(j * WINDOW, (j + 2) * WINDOW)
            cols = slice(g * LANES, (g + 1) * LANES)
            kd = k_ref[rows, cols]
            vd = v_ref[rows, cols]
            zero = jnp.zeros_like(kd)
            k_bd = jnp.concatenate([jnp.where(low_lanes, kd, zero),
                                    jnp.where(low_lanes, zero, kd)], axis=0)
            v_bd = jnp.concatenate([jnp.where(low_lanes, vd, zero),
                                    jnp.where(low_lanes, zero, vd)], axis=0)
            kv_cache[(j, g)] = (k_bd, v_bd)
        return kv_cache[(j, g)]

    def scores(j, chunk):
        k_bd, _ = block_diag_kv(j, chunk // (GROUP // 2))
        qc = q_ref[j * WINDOW:(j + 1) * WINDOW, chunk * LANES:(chunk + 1) * LANES]
        return lax.dot_general(qc, k_bd, (((1,), (1,)), ((), ())),
                               preferred_element_type=F32)

    def softmax(j, chunk, lg):
        mask = band_mask(j)
        ps, inv_den = [], []
        for hh in range(2):
            sink = sinks_ref[2 * chunk + hh]
            l = jnp.where(mask, lg[:, hh * 2 * WINDOW:(hh + 1) * 2 * WINDOW], -jnp.inf)
            m = jnp.maximum(jnp.max(l, axis=-1, keepdims=True), sink)
            p = jnp.exp(l - m)
            den = jnp.sum(p, axis=-1, keepdims=True) + jnp.exp(sink - m)
            ps.append(p.astype(BF16))
            inv_den.append(1.0 / den)
        return jnp.concatenate(ps, axis=-1), jnp.where(low_lanes, inv_den[0], inv_den[1])

    def values(j, chunk, p, inv_den):
        _, v_bd = block_diag_kv(j, chunk // (GROUP // 2))
        pv = jnp.dot(p, v_bd, preferred_element_type=F32)
        mix_ref[j * WINDOW:(j + 1) * WINDOW, chunk * LANES:(chunk + 1) * LANES] = (
            (pv * inv_den).astype(BF16))

    def out_proj(j):
        rows = slice(j * WINDOW, (j + 1) * WINDOW)
        mixed = jnp.dot(mix_ref[rows, :], wout_ref[...], preferred_element_type=F32)
        x1_ref[slot_f, rows, :] = x_ref[0, rows, :] + mod_row(2, seq_f) * mixed

    def f_attn(step):
        def task():
            n = len(bodies)
            if step < n:
                logits[step] = scores(*bodies[step])
            if 0 <= step - 1 < n:
                probs[step - 1] = softmax(*bodies[step - 1], logits.pop(step - 1))
            done = step - ATTN_SKEW
            if 0 <= done < n:
                values(*bodies[done], *probs.pop(done))
                j, chunk = bodies[done]
                if chunk == chunks - 1:
                    out_proj(j)
        return task

    front = [f_norm, f_trig, f_q, f_kv, f_up]
    front += [f_pool(gi, w) for gi, w in enumerate(POOL_WINDOWS)]
    front += [f_attn(step) for step in range(len(bodies) + ATTN_SKEW)]

    def b_norm():
        x1 = x1_ref[slot_b]
        scale = n2_ref[...] * (1.0 + mod_row(4, seq_b))
        h2_ref[...] = (x1 * _rms_scale(x1) * scale + mod_row(3, seq_b)).astype(BF16)

    def b_gate_up(n):
        def task():
            cols = slice(n * FF_CHUNK, (n + 1) * FF_CHUNK)
            g = jnp.dot(h2_ref[...], wg_ref[:, cols], preferred_element_type=F32)
            u = jnp.dot(h2_ref[...], wu_ref[:, cols], preferred_element_type=F32)
            act_ref[:, cols] = (g * jax.nn.sigmoid(g) * u).astype(BF16)
        return task

    def b_down(n):
        def task():
            cols = slice(n * DOWN_CHUNK, (n + 1) * DOWN_CHUNK)
            ff = jnp.dot(act_ref[...], wd_ref[:, cols], preferred_element_type=F32)
            o_ref[0, :, cols] = x1_ref[slot_b, :, cols] + mod_row(5, seq_b)[:, cols] * ff
        return task

    def b_final():
        x2 = o_ref[0]
        o_ref[0] = x2 * _rms_scale(x2) * nf_ref[...]

    d_ff = wg_ref.shape[1]
    back = [b_norm]
    back += [b_gate_up(n) for n in range(d_ff // FF_CHUNK)]
    back += [b_down(n) for n in range(D_MODEL // DOWN_CHUNK)]
    back += [b_final]

    for task in _interleave(front, back):
        task()


def _layer(x, mod, positions, sinks, norm1, norm2, norm_f, w_in, w_pool, pool_scale, w_out,
           w_gate, w_up, w_down):
    b, s, d = x.shape
    t = TOKEN_TILE
    d_ff = w_gate.shape[1]
    assert d == D_MODEL and s % t == 0 and t % WINDOW == 0 and d_ff % FF_CHUNK == 0
    assert d_ff % CAST_CHUNK_ROWS == 0 and d % CAST_CHUNK_ROWS == 0
    tiles_per_seq = s // t
    n_tiles = b * tiles_per_seq

    def front_tile(step):
        return jnp.minimum(step, n_tiles - 1)

    def back_tile(step):
        return jnp.maximum(step - 1, 0)

    def tok_map(tile_of):
        return lambda step, *_: (tile_of(step) // tiles_per_seq, tile_of(step) % tiles_per_seq, 0)

    per_row = LANES // HALF
    inv_freq = ROPE_THETA ** (-jnp.arange(HALF, dtype=F32) * (2.0 / HEAD_DIM))
    freq = jnp.tile(inv_freq, per_row).reshape(1, LANES)
    pos_rep = jnp.repeat(positions.reshape(b * s // per_row, per_row), HALF, axis=1)
    grid_spec = pltpu.PrefetchScalarGridSpec(
        num_scalar_prefetch=1,
        grid=(n_tiles + 1,),
        in_specs=[
            pl.BlockSpec((1, t, d), tok_map(front_tile)),
            pl.BlockSpec((t // per_row, LANES), lambda step, *_: (front_tile(step), 0)),
            _const_spec((1, LANES)),
            _const_spec((N_MOD, b, d)),
            _const_spec((1, d)), _const_spec((1, d)), _const_spec((1, d)),
            _const_spec((1, POOL_WIDTH)),
        ] + [pl.BlockSpec(memory_space=pl.ANY)] * 6,
        out_specs=pl.BlockSpec((1, t, d), tok_map(back_tile)),
        scratch_shapes=[
            pltpu.VMEM((d, IN_PROJ_WIDTH), BF16),
            pltpu.VMEM((POOL_WIDTH, POOL_GROUP_WIDTH), BF16),
            pltpu.VMEM((ATTN_WIDTH + POOL_WIDTH, d), BF16),
            pltpu.VMEM((d, d_ff), BF16),
            pltpu.VMEM((d, d_ff), BF16),
            pltpu.VMEM((d_ff, d), BF16),
            pltpu.VMEM((2, CAST_CHUNK_ROWS, max(d_ff, IN_PROJ_WIDTH)), F32),
            pltpu.SemaphoreType.DMA((2,)),
            pltpu.VMEM((t, LANES), F32),
            pltpu.VMEM((t, LANES), F32),
            pltpu.VMEM((t, d), BF16),
            pltpu.VMEM((t, ATTN_WIDTH), BF16),
            pltpu.VMEM((t + WINDOW, 2 * KV_WIDTH), BF16),
            pltpu.VMEM((t + WINDOW, 2 * KV_WIDTH), BF16),
            pltpu.VMEM((t + POOL_HALO, POOL_WIDTH), F32),
            pltpu.VMEM((t, ATTN_WIDTH + POOL_WIDTH), BF16),
            pltpu.VMEM((2, t, d), F32),
            pltpu.VMEM((t, d), BF16),
            pltpu.VMEM((t, d_ff), BF16),
        ],
    )
    return pl.pallas_call(
        functools.partial(_layer_kernel, tiles_per_seq=tiles_per_seq),
        grid_spec=grid_spec,
        out_shape=jax.ShapeDtypeStruct((b, s, d), F32),
        compiler_params=pltpu.CompilerParams(
            dimension_semantics=("arbitrary",), vmem_limit_bytes=VMEM_LIMIT_BYTES),
        name="layer",
    )(sinks, x, pos_rep, freq, mod, norm1.reshape(1, d), norm2.reshape(1, d), norm_f.reshape(1, d),
      pool_scale.reshape(1, -1), w_in, w_pool.reshape(POOL_WIDTH, POOL_GROUP_WIDTH), w_out,
      w_gate, w_up, w_down)


def kernel(x, c, positions, w_ada, b_ada, norm1, w_in, sinks, w_pool, pool_scale,
           w_out, norm2, w_gate, w_up, w_down, norm_f):
    mod = _adaln_mod(c, w_ada, b_ada)
    return _layer(x, mod, positions, sinks, norm1, norm2, norm_f, w_in, w_pool, pool_scale,
                  w_out, w_gate, w_up, w_down)
```

```python
import functools
import math

import jax
import jax.numpy as jnp
from jax import lax
from jax.experimental import pallas as pl
from jax.experimental.pallas import tpu as pltpu

F32 = jnp.float32
BF16 = jnp.bfloat16

D_MODEL = 1024
HEAD_DIM = 64
N_HEADS = 8
N_KV_HEADS = 2
GROUP = N_HEADS // N_KV_HEADS
ATTN_WIDTH = N_HEADS * HEAD_DIM
KV_WIDTH = N_KV_HEADS * HEAD_DIM
POOL_WINDOWS = (2, 4, 8, 16)
POOL_GROUP_WIDTH = 128
POOL_WIDTH = POOL_GROUP_WIDTH * len(POOL_WINDOWS)
IN_PROJ_WIDTH = ATTN_WIDTH + 2 * KV_WIDTH + POOL_WIDTH
WINDOW = 128
ROPE_THETA = 10000.0
N_MOD = 6
RMS_EPS = 1e-6
HALF = HEAD_DIM // 2

LANES = 128
POOL_HALO = 16
VMEM_LIMIT_BYTES = 56 * 1024 * 1024

TOKEN_TILE = 512
FF_CHUNK = 256
DOWN_CHUNK = 256
ATTN_SKEW = 3
CAST_CHUNK_ROWS = 256
OUT_PROJ_LAG = 2
BACK_SPAN = 0.9


def _const_spec(shape):
    zeros = (0,) * len(shape)
    return pl.BlockSpec(shape, lambda *_: zeros, pipeline_mode=pl.Buffered(1))


def _rms_scale(x):
    return lax.rsqrt(jnp.mean(x * x, axis=-1, keepdims=True) + RMS_EPS)


def _interleave(front, back):
    keyed = [((k + 0.5) / len(front), 0, k, f) for k, f in enumerate(front)]
    keyed += [(BACK_SPAN * (k + 0.5) / len(back), 1, k, f) for k, f in enumerate(back)]
    return [f for *_, f in sorted(keyed, key=lambda e: e[:3])]


def _mod_kernel(ct_ref, w_ref, b_ref, o_ref):
    ct = ct_ref[...]
    sc = ct * jax.nn.sigmoid(ct)
    w = w_ref[...]
    for r in range(ct.shape[1]):
        o_ref[0, r:r + 1, :] = jnp.sum(w * sc[:, r:r + 1], axis=0, keepdims=True) + b_ref[...]


def _adaln_mod(c, w_ada, b_ada):
    b, d = c.shape
    return pl.pallas_call(
        _mod_kernel,
        grid=(N_MOD,),
        in_specs=[
            pl.BlockSpec((d, b), lambda j: (0, 0)),
            pl.BlockSpec((d, d), lambda j: (0, j)),
            pl.BlockSpec((1, d), lambda j: (0, j)),
        ],
        out_specs=pl.BlockSpec((1, b, d), lambda j: (j, 0, 0)),
        out_shape=jax.ShapeDtypeStruct((N_MOD, b, d), F32),
        compiler_params=pltpu.CompilerParams(dimension_semantics=("arbitrary",)),
        name="adaln_mod",
    )(c.T, w_ada, b_ada.reshape(1, -1))


def _load_weights_as_bf16(pairs, stage_ref, sems):
    chunks = [(src, dst, c) for src, dst in pairs
              for c in range(src.shape[0] // CAST_CHUNK_ROWS)]

    def copy(k):
        src, _, c = chunks[k]
        return pltpu.make_async_copy(
            src.at[pl.ds(c * CAST_CHUNK_ROWS, CAST_CHUNK_ROWS), :],
            stage_ref.at[k % 2, :, pl.ds(0, src.shape[1])],
            sems.at[k % 2])

    copy(0).start()
    for k, (src, dst, c) in enumerate(chunks):
        if k + 1 < len(chunks):
            copy(k + 1).start()
        copy(k).wait()
        dst[c * CAST_CHUNK_ROWS:(c + 1) * CAST_CHUNK_ROWS, :] = (
            stage_ref[k % 2, :, 0:src.shape[1]].astype(BF16))


def _layer_kernel(sinks_ref, x_ref, pos_ref, freq_ref, mod_ref, n1_ref, n2_ref, nf_ref,
                  pscale_ref, win_hbm, wpool_hbm, wout_hbm, wg_hbm, wu_hbm, wd_hbm,
                  o_ref,
                  win_ref, wpool_ref, wout_ref, wg_ref, wu_ref, wd_ref, stage_ref, cast_sems,
                  cos_ref, sin_ref, h_ref, q_ref, k_ref, v_ref, ext_ref, mix_ref, x1_ref, h2_ref,
                  h2n_ref, act_ref, *, tiles_per_seq):
    t = x_ref.shape[1]
    s = pl.program_id(0)
    n_tiles = pl.num_programs(0) - 1
    tile_f = jnp.minimum(s, n_tiles - 1)
    first = (tile_f % tiles_per_seq) == 0
    slot_f = s % 2
    slot_b = 1 - slot_f
    seq_f = tile_f // tiles_per_seq
    seq_b = jnp.maximum(s - 1, 0) // tiles_per_seq

    def mod_row(k, seq):
        return mod_ref[k, pl.ds(seq, 1), :]

    @pl.when(s == 0)
    def _():
        _load_weights_as_bf16(
            [(win_hbm, win_ref), (wpool_hbm, wpool_ref), (wout_hbm, wout_ref),
             (wg_hbm, wg_ref), (wu_hbm, wu_ref), (wd_hbm, wd_ref)], stage_ref, cast_sems)
        x1_ref[1] = jnp.zeros((t, D_MODEL), F32)
        h2n_ref[...] = jnp.zeros((t, D_MODEL), BF16)
        k_ref[t:, :] = jnp.zeros((WINDOW, 2 * KV_WIDTH), BF16)
        v_ref[t:, :] = jnp.zeros((WINDOW, 2 * KV_WIDTH), BF16)
        ext_ref[t:, :] = jnp.zeros((POOL_HALO, POOL_WIDTH), F32)

    lane = lax.broadcasted_iota(jnp.int32, (1, LANES), 1)
    low_lanes = lane < HEAD_DIM
    first_half = (lane % HEAD_DIM) < HALF

    def f_norm():
        x = x_ref[0]
        scale = n1_ref[...] * (1.0 + mod_row(1, seq_f))
        h_ref[...] = (x * _rms_scale(x) * scale + mod_row(0, seq_f)).astype(BF16)

    def f_trig():
        ang = pos_ref[...].astype(F32) * freq_ref[...]
        cos, sin = jnp.cos(ang), jnp.sin(ang)
        per_row = LANES // HALF
        for m in range(per_row):
            c32 = cos[:, m * HALF:(m + 1) * HALF]
            s32 = sin[:, m * HALF:(m + 1) * HALF]
            rows = pl.ds(m, t // per_row, stride=per_row)
            cos_ref[rows, :] = jnp.concatenate([c32, c32, c32, c32], axis=-1)
            sin_ref[rows, :] = jnp.concatenate([-s32, s32, -s32, s32], axis=-1)

    def rope(tile):
        partner = jnp.where(first_half,
                            pltpu.roll(tile, LANES - HALF, 1),
                            pltpu.roll(tile, HALF, 1))
        return tile * cos_ref[...] + partner * sin_ref[...]

    def f_q():
        u = jnp.dot(h_ref[...], win_ref[:, :ATTN_WIDTH], preferred_element_type=F32)
        q_scale = 1.0 / math.sqrt(HEAD_DIM)
        for j in range(ATTN_WIDTH // LANES):
            cols = slice(j * LANES, (j + 1) * LANES)
            q_ref[:, cols] = (rope(u[:, cols]) * q_scale).astype(BF16)

    def f_kv():
        k_ref[0:WINDOW, :] = k_ref[t:, :]
        v_ref[0:WINDOW, :] = v_ref[t:, :]
        u = jnp.dot(h_ref[...], win_ref[:, ATTN_WIDTH:ATTN_WIDTH + 2 * KV_WIDTH],
                    preferred_element_type=F32)
        kr = rope(u[:, :KV_WIDTH])
        vv = u[:, KV_WIDTH:]
        for g in range(N_KV_HEADS):
            kg = kr[:, g * HEAD_DIM:(g + 1) * HEAD_DIM]
            vg = vv[:, g * HEAD_DIM:(g + 1) * HEAD_DIM]
            cols = slice(g * LANES, (g + 1) * LANES)
            k_ref[WINDOW:, cols] = jnp.concatenate([kg, kg], axis=-1).astype(BF16)
            v_ref[WINDOW:, cols] = jnp.concatenate([vg, vg], axis=-1).astype(BF16)

    def f_up():
        halo = ext_ref[t:, :]
        ext_ref[0:POOL_HALO, :] = jnp.where(first, jnp.zeros_like(halo), halo)
        ext_ref[POOL_HALO:, :] = jnp.dot(h_ref[...], win_ref[:, ATTN_WIDTH + 2 * KV_WIDTH:],
                                         preferred_element_type=F32)

    def f_pool(gi, w):
        def task():
            cols = slice(gi * POOL_GROUP_WIDTH, (gi + 1) * POOL_GROUP_WIDTH)
            pos_in_seq = ((tile_f % tiles_per_seq) * t
                          + lax.broadcasted_iota(jnp.int32, (t, 1), 0))
            tok = ext_ref[POOL_HALO:, cols]
            total = tok
            for back in range(1, w):
                total = total + ext_ref[POOL_HALO - back:POOL_HALO - back + t, cols]
            count = jnp.minimum(pos_in_seq + 1, w).astype(F32)
            pooled = (total / count - tok).astype(BF16)
            po = jnp.dot(pooled, wpool_ref[cols, :], preferred_element_type=F32) * pscale_ref[:, cols]
            mix_ref[:, ATTN_WIDTH + gi * POOL_GROUP_WIDTH:
                    ATTN_WIDTH + (gi + 1) * POOL_GROUP_WIDTH] = po.astype(BF16)
        return task

    n_blocks = t // WINDOW
    chunks = N_HEADS // 2
    bodies = [(j, chunk) for j in range(n_blocks) for chunk in range(chunks)]
    kv_cache, logits, probs = {}, {}, {}

    def band_mask(j):
        qi = lax.broadcasted_iota(jnp.int32, (WINDOW, 2 * WINDOW), 0)
        kj = lax.broadcasted_iota(jnp.int32, (WINDOW, 2 * WINDOW), 1)
        rel = kj - WINDOW - qi
        band = (rel <= 0) & (rel > -WINDOW)
        if j == 0:
            band = band & (kj >= jnp.where(first, WINDOW, 0))
        return band

    def block_diag_kv(j, g):
        if (j, g) not in kv_cache:
            rows = slice(j * WINDOW, (j + 2) * WINDOW)
            cols = slice(g * LANES, (g + 1) * LANES)
            kd = k_ref[rows, cols]
            vd = v_ref[rows, cols]
            zero = jnp.zeros_like(kd)
            k_bd = jnp.concatenate([jnp.where(low_lanes, kd, zero),
                                    jnp.where(low_lanes, zero, kd)], axis=0)
            v_bd = jnp.concatenate([jnp.where(low_lanes, vd, zero),
                                    jnp.where(low_lanes, zero, vd)], axis=0)
            kv_cache[(j, g)] = (k_bd, v_bd)
        return kv_cache[(j, g)]

    def scores(j, chunk):
        k_bd, _ = block_diag_kv(j, chunk // (GROUP // 2))
        qc = q_ref[j * WINDOW:(j + 1) * WINDOW, chunk * LANES:(chunk + 1) * LANES]
        return lax.dot_general(qc, k_bd, (((1,), (1,)), ((), ())),
                               preferred_element_type=F32)

    def softmax(j, chunk, lg):
        mask = band_mask(j)
        ps, inv_den = [], []
        for hh in range(2):
            sink = sinks_ref[2 * chunk + hh]
            l = jnp.where(mask, lg[:, hh * 2 * WINDOW:(hh + 1) * 2 * WINDOW], -jnp.inf)
            m = jnp.maximum(jnp.max(l, axis=-1, keepdims=True), sink)
            p = jnp.exp(l - m)
            den = jnp.sum(p, axis=-1, keepdims=True) + jnp.exp(sink - m)
            ps.append(p.astype(BF16))
            inv_den.append(1.0 / den)
        return jnp.concatenate(ps, axis=-1), jnp.where(low_lanes, inv_den[0], inv_den[1])

    def values(j, chunk, p, inv_den):
        _, v_bd = block_diag_kv(j, chunk // (GROUP // 2))
        pv = jnp.dot(p, v_bd, preferred_element_type=F32)
        mix_ref[j * WINDOW:(j + 1) * WINDOW, chunk * LANES:(chunk + 1) * LANES] = (
            (pv * inv_den).astype(BF16))

    def out_proj(j):
        rows = slice(j * WINDOW, (j + 1) * WINDOW)
        mixed = jnp.dot(mix_ref[rows, :], wout_ref[...], preferred_element_type=F32)
        x1 = x_ref[0, rows, :] + mod_row(2, seq_f) * mixed
        x1_ref[slot_f, rows, :] = x1
        scale = n2_ref[...] * (1.0 + mod_row(4, seq_f))
        h2n_ref[rows, :] = (x1 * _rms_scale(x1) * scale + mod_row(3, seq_f)).astype(BF16)

    def f_attn(step):
        def task():
            n = len(bodies)
            if step < n:
                logits[step] = scores(*bodies[step])
            if 0 <= step - 1 < n:
                probs[step - 1] = softmax(*bodies[step - 1], logits.pop(step - 1))
            done = step - ATTN_SKEW
            if 0 <= done < n:
                values(*bodies[done], *probs.pop(done))
            ready = step - ATTN_SKEW - OUT_PROJ_LAG
            if 0 <= ready < n and bodies[ready][1] == chunks - 1:
                out_proj(bodies[ready][0])
        return task

    front = [f_norm, f_trig, f_q, f_kv, f_up]
    front += [f_pool(gi, w) for gi, w in enumerate(POOL_WINDOWS)]
    front += [f_attn(step) for step in range(len(bodies) + ATTN_SKEW + OUT_PROJ_LAG)]

    def b_gate_up(n):
        def task():
            cols = slice(n * FF_CHUNK, (n + 1) * FF_CHUNK)
            g = jnp.dot(h2_ref[...], wg_ref[:, cols], preferred_element_type=F32)
            u = jnp.dot(h2_ref[...], wu_ref[:, cols], preferred_element_type=F32)
            act_ref[:, cols] = (g * jax.nn.sigmoid(g) * u).astype(BF16)
        return task

    def b_down(n):
        def task():
            cols = slice(n * DOWN_CHUNK, (n + 1) * DOWN_CHUNK)
            ff = jnp.dot(act_ref[...], wd_ref[:, cols], preferred_element_type=F32)
            o_ref[0, :, cols] = x1_ref[slot_b, :, cols] + mod_row(5, seq_b)[:, cols] * ff
        return task

    def b_final():
        x2 = o_ref[0]
        o_ref[0] = x2 * _rms_scale(x2) * nf_ref[...]

    d_ff = wg_ref.shape[1]
    back = [b_gate_up(n) for n in range(d_ff // FF_CHUNK)]
    back += [b_down(n) for n in range(D_MODEL // DOWN_CHUNK)]
    back += [b_final]

    h2_ref[...] = h2n_ref[...]
    back[0]()
    for task in _interleave(front, back[1:]):
        task()


def _layer(x, mod, positions, sinks, norm1, norm2, norm_f, w_in, w_pool, pool_scale, w_out,
           w_gate, w_up, w_down):
    b, s, d = x.shape
    t = TOKEN_TILE
    d_ff = w_gate.shape[1]
    assert d == D_MODEL and s % t == 0 and t % WINDOW == 0 and d_ff % FF_CHUNK == 0
    assert d_ff % CAST_CHUNK_ROWS == 0 and d % CAST_CHUNK_ROWS == 0
    tiles_per_seq = s // t
    n_tiles = b * tiles_per_seq

    def front_tile(step):
        return jnp.minimum(step, n_tiles - 1)

    def back_tile(step):
        return jnp.maximum(step - 1, 0)

    def tok_map(tile_of):
        return lambda step, *_: (tile_of(step) // tiles_per_seq, tile_of(step) % tiles_per_seq, 0)

    per_row = LANES // HALF
    inv_freq = ROPE_THETA ** (-jnp.arange(HALF, dtype=F32) * (2.0 / HEAD_DIM))
    freq = jnp.tile(inv_freq, per_row).reshape(1, LANES)
    pos_rep = jnp.repeat(positions.reshape(b * s // per_row, per_row), HALF, axis=1)
    grid_spec = pltpu.PrefetchScalarGridSpec(
        num_scalar_prefetch=1,
        grid=(n_tiles + 1,),
        in_specs=[
            pl.BlockSpec((1, t, d), tok_map(front_tile)),
            pl.BlockSpec((t // per_row, LANES), lambda step, *_: (front_tile(step), 0)),
            _const_spec((1, LANES)),
            _const_spec((N_MOD, b, d)),
            _const_spec((1, d)), _const_spec((1, d)), _const_spec((1, d)),
            _const_spec((1, POOL_WIDTH)),
        ] + [pl.BlockSpec(memory_space=pl.ANY)] * 6,
        out_specs=pl.BlockSpec((1, t, d), tok_map(back_tile)),
        scratch_shapes=[
            pltpu.VMEM((d, IN_PROJ_WIDTH), BF16),
            pltpu.VMEM((POOL_WIDTH, POOL_GROUP_WIDTH), BF16),
            pltpu.VMEM((ATTN_WIDTH + POOL_WIDTH, d), BF16),
            pltpu.VMEM((d, d_ff), BF16),
            pltpu.VMEM((d, d_ff), BF16),
            pltpu.VMEM((d_ff, d), BF16),
            pltpu.VMEM((2, CAST_CHUNK_ROWS, max(d_ff, IN_PROJ_WIDTH)), F32),
            pltpu.SemaphoreType.DMA((2,)),
            pltpu.VMEM((t, LANES), F32),
            pltpu.VMEM((t, LANES), F32),
            pltpu.VMEM((t, d), BF16),
            pltpu.VMEM((t, ATTN_WIDTH), BF16),
            pltpu.VMEM((t + WINDOW, 2 * KV_WIDTH), BF16),
            pltpu.VMEM((t + WINDOW, 2 * KV_WIDTH), BF16),
            pltpu.VMEM((t + POOL_HALO, POOL_WIDTH), F32),
            pltpu.VMEM((t, ATTN_WIDTH + POOL_WIDTH), BF16),
            pltpu.VMEM((2, t, d), F32),
            pltpu.VMEM((t, d), BF16),
            pltpu.VMEM((t, d), BF16),
            pltpu.VMEM((t, d_ff), BF16),
        ],
    )
    return pl.pallas_call(
        functools.partial(_layer_kernel, tiles_per_seq=tiles_per_seq),
        grid_spec=grid_spec,
        out_shape=jax.ShapeDtypeStruct((b, s, d), F32),
        compiler_params=pltpu.CompilerParams(
            dimension_semantics=("arbitrary",), vmem_limit_bytes=VMEM_LIMIT_BYTES),
        name="layer",
    )(sinks, x, pos_rep, freq, mod, norm1.reshape(1, d), norm2.reshape(1, d), norm_f.reshape(1, d),
      pool_scale.reshape(1, -1), w_in, w_pool.reshape(POOL_WIDTH, POOL_GROUP_WIDTH), w_out,
      w_gate, w_up, w_down)


def kernel(x, c, positions, w_ada, b_ada, norm1, w_in, sinks, w_pool, pool_scale,
           w_out, norm2, w_gate, w_up, w_down, norm_f):
    mod = _adaln_mod(c, w_ada, b_ada)
    return _layer(x, mod, positions, sinks, norm1, norm2, norm_f, w_in, w_pool, pool_scale,
                  w_out, w_gate, w_up, w_down)
```

```python
import functools
import math

import jax
import jax.numpy as jnp
from jax import lax
from jax.experimental import pallas as pl
from jax.experimental.pallas import tpu as pltpu

F32 = jnp.float32
BF16 = jnp.bfloat16

D_MODEL = 1024
HEAD_DIM = 64
N_HEADS = 8
N_KV_HEADS = 2
GROUP = N_HEADS // N_KV_HEADS
ATTN_WIDTH = N_HEADS * HEAD_DIM
KV_WIDTH = N_KV_HEADS * HEAD_DIM
POOL_WINDOWS = (2, 4, 8, 16)
POOL_GROUP_WIDTH = 128
POOL_WIDTH = POOL_GROUP_WIDTH * len(POOL_WINDOWS)
IN_PROJ_WIDTH = ATTN_WIDTH + 2 * KV_WIDTH + POOL_WIDTH
WINDOW = 128
ROPE_THETA = 10000.0
N_MOD = 6
RMS_EPS = 1e-6
HALF = HEAD_DIM // 2

LANES = 128
POOL_HALO = 16
VMEM_LIMIT_BYTES = 56 * 1024 * 1024

TOKEN_TILE = 512
FF_CHUNK = 256
DOWN_CHUNK = 256
ATTN_SKEW = 3
CAST_CHUNK_ROWS = 256
OUT_PROJ_LAG = 0
BACK_SPAN = 1.0


def _const_spec(shape):
    zeros = (0,) * len(shape)
    return pl.BlockSpec(shape, lambda *_: zeros, pipeline_mode=pl.Buffered(1))


def _rms_scale(x):
    return lax.rsqrt(jnp.mean(x * x, axis=-1, keepdims=True) + RMS_EPS)


def _interleave(front, back):
    keyed = [((k + 0.5) / len(front), 0, k, f) for k, f in enumerate(front)]
    keyed += [(BACK_SPAN * (k + 0.5) / len(back), 1, k, f) for k, f in enumerate(back)]
    return [f for *_, f in sorted(keyed, key=lambda e: e[:3])]


def _mod_kernel(ct_ref, w_ref, b_ref, o_ref):
    ct = ct_ref[...]
    sc = ct * jax.nn.sigmoid(ct)
    w = w_ref[...]
    for r in range(ct.shape[1]):
        o_ref[0, r:r + 1, :] = jnp.sum(w * sc[:, r:r + 1], axis=0, keepdims=True) + b_ref[...]


def _adaln_mod(c, w_ada, b_ada):
    b, d = c.shape
    return pl.pallas_call(
        _mod_kernel,
        grid=(N_MOD,),
        in_specs=[
            pl.BlockSpec((d, b), lambda j: (0, 0)),
            pl.BlockSpec((d, d), lambda j: (0, j)),
            pl.BlockSpec((1, d), lambda j: (0, j)),
        ],
        out_specs=pl.BlockSpec((1, b, d), lambda j: (j, 0, 0)),
        out_shape=jax.ShapeDtypeStruct((N_MOD, b, d), F32),
        compiler_params=pltpu.CompilerParams(dimension_semantics=("arbitrary",)),
        name="adaln_mod",
    )(c.T, w_ada, b_ada.reshape(1, -1))


def _load_weights_as_bf16(pairs, stage_ref, sems):
    chunks = [(src, dst, c) for src, dst in pairs
              for c in range(src.shape[0] // CAST_CHUNK_ROWS)]

    def copy(k):
        src, _, c = chunks[k]
        return pltpu.make_async_copy(
            src.at[pl.ds(c * CAST_CHUNK_ROWS, CAST_CHUNK_ROWS), :],
            stage_ref.at[k % 2, :, pl.ds(0, src.shape[1])],
            sems.at[k % 2])

    copy(0).start()
    for k, (src, dst, c) in enumerate(chunks):
        if k + 1 < len(chunks):
            copy(k + 1).start()
        copy(k).wait()
        dst[c * CAST_CHUNK_ROWS:(c + 1) * CAST_CHUNK_ROWS, :] = (
            stage_ref[k % 2, :, 0:src.shape[1]].astype(BF16))


def _layer_kernel(sinks_ref, x_ref, pos_ref, freq_ref, mod_ref, n1_ref, n2_ref, nf_ref,
                  pscale_ref, win_hbm, wpool_hbm, wout_hbm, wg_hbm, wu_hbm, wd_hbm,
                  o_ref,
                  win_ref, wpool_ref, wout_ref, wg_ref, wu_ref, wd_ref, stage_ref, cast_sems,
                  cos_ref, sin_ref, h_ref, q_ref, k_ref, v_ref, ext_ref, mix_ref, x1_ref, h2_ref,
                  h2n_ref, act_ref, *, tiles_per_seq):
    t = x_ref.shape[1]
    s = pl.program_id(0)
    n_tiles = pl.num_programs(0) - 1
    tile_f = jnp.minimum(s, n_tiles - 1)
    first = (tile_f % tiles_per_seq) == 0
    slot_f = s % 2
    slot_b = 1 - slot_f
    seq_f = tile_f // tiles_per_seq
    seq_b = jnp.maximum(s - 1, 0) // tiles_per_seq

    def mod_row(k, seq):
        return mod_ref[k, pl.ds(seq, 1), :]

    @pl.when(s == 0)
    def _():
        _load_weights_as_bf16(
            [(win_hbm, win_ref), (wpool_hbm, wpool_ref), (wout_hbm, wout_ref),
             (wg_hbm, wg_ref), (wu_hbm, wu_ref), (wd_hbm, wd_ref)], stage_ref, cast_sems)
        x1_ref[1] = jnp.zeros((t, D_MODEL), F32)
        h2n_ref[...] = jnp.zeros((t, D_MODEL), BF16)
        k_ref[t:, :] = jnp.zeros((WINDOW, 2 * KV_WIDTH), BF16)
        v_ref[t:, :] = jnp.zeros((WINDOW, 2 * KV_WIDTH), BF16)
        ext_ref[t:, :] = jnp.zeros((POOL_HALO, POOL_WIDTH), F32)

    lane = lax.broadcasted_iota(jnp.int32, (1, LANES), 1)
    low_lanes = lane < HEAD_DIM
    first_half = (lane % HEAD_DIM) < HALF

    def f_norm():
        x = x_ref[0]
        scale = n1_ref[...] * (1.0 + mod_row(1, seq_f))
        h_ref[...] = (x * _rms_scale(x) * scale + mod_row(0, seq_f)).astype(BF16)

    def f_trig():
        ang = pos_ref[...].astype(F32) * freq_ref[...]
        cos, sin = jnp.cos(ang), jnp.sin(ang)
        per_row = LANES // HALF
        for m in range(per_row):
            c32 = cos[:, m * HALF:(m + 1) * HALF]
            s32 = sin[:, m * HALF:(m + 1) * HALF]
            rows = pl.ds(m, t // per_row, stride=per_row)
            cos_ref[rows, :] = jnp.concatenate([c32, c32, c32, c32], axis=-1)
            sin_ref[rows, :] = jnp.concatenate([-s32, s32, -s32, s32], axis=-1)

    def rope(tile):
        partner = jnp.where(first_half,
                            pltpu.roll(tile, LANES - HALF, 1),
                            pltpu.roll(tile, HALF, 1))
        return tile * cos_ref[...] + partner * sin_ref[...]

    def f_q():
        u = jnp.dot(h_ref[...], win_ref[:, :ATTN_WIDTH], preferred_element_type=F32)
        q_scale = 1.0 / math.sqrt(HEAD_DIM)
        for j in range(ATTN_WIDTH // LANES):
            cols = slice(j * LANES, (j + 1) * LANES)
            q_ref[:, cols] = (rope(u[:, cols]) * q_scale).astype(BF16)

    def f_kv():
        k_ref[0:WINDOW, :] = k_ref[t:, :]
        v_ref[0:WINDOW, :] = v_ref[t:, :]
        u = jnp.dot(h_ref[...], win_ref[:, ATTN_WIDTH:ATTN_WIDTH + 2 * KV_WIDTH],
                    preferred_element_type=F32)
        kr = rope(u[:, :KV_WIDTH])
        vv = u[:, KV_WIDTH:]
        for g in range(N_KV_HEADS):
            kg = kr[:, g * HEAD_DIM:(g + 1) * HEAD_DIM]
            vg = vv[:, g * HEAD_DIM:(g + 1) * HEAD_DIM]
            cols = slice(g * LANES, (g + 1) * LANES)
            k_ref[WINDOW:, cols] = jnp.concatenate([kg, kg], axis=-1).astype(BF16)
            v_ref[WINDOW:, cols] = jnp.concatenate([vg, vg], axis=-1).astype(BF16)

    def f_up():
        halo = ext_ref[t:, :]
        ext_ref[0:POOL_HALO, :] = jnp.where(first, jnp.zeros_like(halo), halo)
        ext_ref[POOL_HALO:, :] = jnp.dot(h_ref[...], win_ref[:, ATTN_WIDTH + 2 * KV_WIDTH:],
                                         preferred_element_type=F32)

    def f_pool(gi, w):
        def task():
            cols = slice(gi * POOL_GROUP_WIDTH, (gi + 1) * POOL_GROUP_WIDTH)
            pos_in_seq = ((tile_f % tiles_per_seq) * t
                          + lax.broadcasted_iota(jnp.int32, (t, 1), 0))
            tok = ext_ref[POOL_HALO:, cols]
            total = tok
            for back in range(1, w):
                total = total + ext_ref[POOL_HALO - back:POOL_HALO - back + t, cols]
            count = jnp.minimum(pos_in_seq + 1, w).astype(F32)
            pooled = (total / count - tok).astype(BF16)
            po = jnp.dot(pooled, wpool_ref[cols, :], preferred_element_type=F32) * pscale_ref[:, cols]
            mix_ref[:, ATTN_WIDTH + gi * POOL_GROUP_WIDTH:
                    ATTN_WIDTH + (gi + 1) * POOL_GROUP_WIDTH] = po.astype(BF16)
        return task

    n_blocks = t // WINDOW
    chunks = N_HEADS // 2
    bodies = [(j, chunk) for j in range(n_blocks) for chunk in range(chunks)]
    kv_cache, logits, probs = {}, {}, {}

    def band_mask(j):
        qi = lax.broadcasted_iota(jnp.int32, (WINDOW, 2 * WINDOW), 0)
        kj = lax.broadcasted_iota(jnp.int32, (WINDOW, 2 * WINDOW), 1)
        rel = kj - WINDOW - qi
        band = (rel <= 0) & (rel > -WINDOW)
        if j == 0:
            band = band & (kj >= jnp.where(first, WINDOW, 0))
        return band

    def block_diag_kv(j, g):
        if (j, g) not in kv_cache:
            rows = slice(j * WINDOW, (j + 2) * WINDOW)
            cols = slice(g * LANES, (g + 1) * LANES)
            kd = k_ref[rows, cols]
            vd = v_ref[rows, cols]
            zero = jnp.zeros_like(kd)
            k_bd = jnp.concatenate([jnp.where(low_lanes, kd, zero),
                                    jnp.where(low_lanes, zero, kd)], axis=0)
            v_bd = jnp.concatenate([jnp.where(low_lanes, vd, zero),
                                    jnp.where(low_lanes, zero, vd)], axis=0)
            kv_cache[(j, g)] = (k_bd, v_bd)
        return kv_cache[(j, g)]

    def scores(j, chunk):
        k_bd, _ = block_diag_kv(j, chunk // (GROUP // 2))
        qc = q_ref[j * WINDOW:(j + 1) * WINDOW, chunk * LANES:(chunk + 1) * LANES]
        return lax.dot_general(qc, k_bd, (((1,), (1,)), ((), ())),
                               preferred_element_type=F32)

    def softmax(j, chunk, lg):
        mask = band_mask(j)
        ps, inv_den = [], []
        for hh in range(2):
            sink = sinks_ref[2 * chunk + hh]
            l = jnp.where(mask, lg[:, hh * 2 * WINDOW:(hh + 1) * 2 * WINDOW], -jnp.inf)
            m = jnp.maximum(jnp.max(l, axis=-1, keepdims=True), sink)
            p = jnp.exp(l - m)
            den = jnp.sum(p, axis=-1, keepdims=True) + jnp.exp(sink - m)
            ps.append(p.astype(BF16))
            inv_den.append(1.0 / den)
        return jnp.concatenate(ps, axis=-1), jnp.where(low_lanes, inv_den[0], inv_den[1])

    def values(j, chunk, p, inv_den):
        _, v_bd = block_diag_kv(j, chunk // (GROUP // 2))
        pv = jnp.dot(p, v_bd, preferred_element_type=F32)
        mix_ref[j * WINDOW:(j + 1) * WINDOW, chunk * LANES:(chunk + 1) * LANES] = (
            (pv * inv_den).astype(BF16))

    def out_proj(j):
        rows = slice(j * WINDOW, (j + 1) * WINDOW)
        mixed = jnp.dot(mix_ref[rows, :], wout_ref[...], preferred_element_type=F32)
        x1 = x_ref[0, rows, :] + mod_row(2, seq_f) * mixed
        x1_ref[slot_f, rows, :] = x1
        scale = n2_ref[...] * (1.0 + mod_row(4, seq_f))
        h2n_ref[rows, :] = (x1 * _rms_scale(x1) * scale + mod_row(3, seq_f)).astype(BF16)

    def f_attn(step):
        def task():
            n = len(bodies)
            if step < n:
                logits[step] = scores(*bodies[step])
            if 0 <= step - 1 < n:
                probs[step - 1] = softmax(*bodies[step - 1], logits.pop(step - 1))
            done = step - ATTN_SKEW
            if 0 <= done < n:
                values(*bodies[done], *probs.pop(done))
            ready = step - ATTN_SKEW - OUT_PROJ_LAG
            if 0 <= ready < n and bodies[ready][1] == chunks - 1:
                out_proj(bodies[ready][0])
        return task

    front = [f_norm, f_trig, f_q, f_kv, f_up]
    front += [f_pool(gi, w) for gi, w in enumerate(POOL_WINDOWS)]
    front += [f_attn(step) for step in range(len(bodies) + ATTN_SKEW + OUT_PROJ_LAG)]

    def b_gate_up(n):
        def task():
            cols = slice(n * FF_CHUNK, (n + 1) * FF_CHUNK)
            g = jnp.dot(h2_ref[...], wg_ref[:, cols], preferred_element_type=F32)
            u = jnp.dot(h2_ref[...], wu_ref[:, cols], preferred_element_type=F32)
            act_ref[:, cols] = (g * jax.nn.sigmoid(g) * u).astype(BF16)
        return task

    def b_down(n):
        def task():
            cols = slice(n * DOWN_CHUNK, (n + 1) * DOWN_CHUNK)
            ff = jnp.dot(act_ref[...], wd_ref[:, cols], preferred_element_type=F32)
            o_ref[0, :, cols] = x1_ref[slot_b, :, cols] + mod_row(5, seq_b)[:, cols] * ff
        return task

    def b_final():
        x2 = o_ref[0]
        o_ref[0] = x2 * _rms_scale(x2) * nf_ref[...]

    d_ff = wg_ref.shape[1]
    back = [b_gate_up(n) for n in range(d_ff // FF_CHUNK)]
    back += [b_down(n) for n in range(D_MODEL // DOWN_CHUNK)]
    back += [b_final]

    h2_ref[...] = h2n_ref[...]
    back[0]()
    for task in _interleave(front, back[1:]):
        task()


def _layer(x, mod, positions, sinks, norm1, norm2, norm_f, w_in, w_pool, pool_scale, w_out,
           w_gate, w_up, w_down):
    b, s, d = x.shape
    t = TOKEN_TILE
    d_ff = w_gate.shape[1]
    assert d == D_MODEL and s % t == 0 and t % WINDOW == 0 and d_ff % FF_CHUNK == 0
    assert d_ff % CAST_CHUNK_ROWS == 0 and d % CAST_CHUNK_ROWS == 0
    tiles_per_seq = s // t
    n_tiles = b * tiles_per_seq

    def front_tile(step):
        return jnp.minimum(step, n_tiles - 1)

    def back_tile(step):
        return jnp.maximum(step - 1, 0)

    def tok_map(tile_of):
        return lambda step, *_: (tile_of(step) // tiles_per_seq, tile_of(step) % tiles_per_seq, 0)

    per_row = LANES // HALF
    inv_freq = ROPE_THETA ** (-jnp.arange(HALF, dtype=F32) * (2.0 / HEAD_DIM))
    freq = jnp.tile(inv_freq, per_row).reshape(1, LANES)
    pos_rep = jnp.repeat(positions.reshape(b * s // per_row, per_row), HALF, axis=1)
    grid_spec = pltpu.PrefetchScalarGridSpec(
        num_scalar_prefetch=1,
        grid=(n_tiles + 1,),
        in_specs=[
            pl.BlockSpec((1, t, d), tok_map(front_tile)),
            pl.BlockSpec((t // per_row, LANES), lambda step, *_: (front_tile(step), 0)),
            _const_spec((1, LANES)),
            _const_spec((N_MOD, b, d)),
            _const_spec((1, d)), _const_spec((1, d)), _const_spec((1, d)),
            _const_spec((1, POOL_WIDTH)),
        ] + [pl.BlockSpec(memory_space=pl.ANY)] * 6,
        out_specs=pl.BlockSpec((1, t, d), tok_map(back_tile)),
        scratch_shapes=[
            pltpu.VMEM((d, IN_PROJ_WIDTH), BF16),
            pltpu.VMEM((POOL_WIDTH, POOL_GROUP_WIDTH), BF16),
            pltpu.VMEM((ATTN_WIDTH + POOL_WIDTH, d), BF16),
            pltpu.VMEM((d, d_ff), BF16),
            pltpu.VMEM((d, d_ff), BF16),
            pltpu.VMEM((d_ff, d), BF16),
            pltpu.VMEM((2, CAST_CHUNK_ROWS, max(d_ff, IN_PROJ_WIDTH)), F32),
            pltpu.SemaphoreType.DMA((2,)),
            pltpu.VMEM((t, LANES), F32),
            pltpu.VMEM((t, LANES), F32),
            pltpu.VMEM((t, d), BF16),
            pltpu.VMEM((t, ATTN_WIDTH), BF16),
            pltpu.VMEM((t + WINDOW, 2 * KV_WIDTH), BF16),
            pltpu.VMEM((t + WINDOW, 2 * KV_WIDTH), BF16),
            pltpu.VMEM((t + POOL_HALO, POOL_WIDTH), F32),
            pltpu.VMEM((t, ATTN_WIDTH + POOL_WIDTH), BF16),
            pltpu.VMEM((2, t, d), F32),
            pltpu.VMEM((t, d), BF16),
            pltpu.VMEM((t, d), BF16),
            pltpu.VMEM((t, d_ff), BF16),
        ],
    )
    return pl.pallas_call(
        functools.partial(_layer_kernel, tiles_per_seq=tiles_per_seq),
        grid_spec=grid_spec,
        out_shape=jax.ShapeDtypeStruct((b, s, d), F32),
        compiler_params=pltpu.CompilerParams(
            dimension_semantics=("arbitrary",), vmem_limit_bytes=VMEM_LIMIT_BYTES),
        name="layer",
    )(sinks, x, pos_rep, freq, mod, norm1.reshape(1, d), norm2.reshape(1, d), norm_f.reshape(1, d),
      pool_scale.reshape(1, -1), w_in, w_pool.reshape(POOL_WIDTH, POOL_GROUP_WIDTH), w_out,
      w_gate, w_up, w_down)


def kernel(x, c, positions, w_ada, b_ada, norm1, w_in, sinks, w_pool, pool_scale,
           w_out, norm2, w_gate, w_up, w_down, norm_f):
    mod = _adaln_mod(c, w_ada, b_ada)
    return _layer(x, mod, positions, sinks, norm1, norm2, norm_f, w_in, w_pool, pool_scale,
                  w_out, w_gate, w_up, w_down)
```

```python
import functools
import math

import jax
import jax.numpy as jnp
from jax import lax
from jax.experimental import pallas as pl
from jax.experimental.pallas import tpu as pltpu

F32 = jnp.float32
BF16 = jnp.bfloat16

D_MODEL = 1024
HEAD_DIM = 64
N_HEADS = 8
N_KV_HEADS = 2
GROUP = N_HEADS // N_KV_HEADS
ATTN_WIDTH = N_HEADS * HEAD_DIM
KV_WIDTH = N_KV_HEADS * HEAD_DIM
POOL_WINDOWS = (2, 4, 8, 16)
POOL_GROUP_WIDTH = 128
POOL_WIDTH = POOL_GROUP_WIDTH * len(POOL_WINDOWS)
IN_PROJ_WIDTH = ATTN_WIDTH + 2 * KV_WIDTH + POOL_WIDTH
WINDOW = 128
ROPE_THETA = 10000.0
N_MOD = 6
RMS_EPS = 1e-6
HALF = HEAD_DIM // 2

LANES = 128
POOL_HALO = 16
VMEM_LIMIT_BYTES = 56 * 1024 * 1024

TOKEN_TILE = 512
FF_CHUNK = 256
DOWN_CHUNK = 256
ATTN_SKEW = 3
CAST_CHUNK_ROWS = 256
OUT_PROJ_LAG = 2
BACK_SPAN = 0.9


def _const_spec(shape):
    zeros = (0,) * len(shape)
    return pl.BlockSpec(shape, lambda *_: zeros, pipeline_mode=pl.Buffered(1))


def _rms_scale(x):
    return lax.rsqrt(jnp.mean(x * x, axis=-1, keepdims=True) + RMS_EPS)


def _interleave(front, back):
    keyed = [((k + 0.5) / len(front), 0, k, f) for k, f in enumerate(front)]
    keyed += [(BACK_SPAN * (k + 0.5) / len(back), 1, k, f) for k, f in enumerate(back)]
    return [f for *_, f in sorted(keyed, key=lambda e: e[:3])]


def _mod_kernel(ct_ref, w_ref, b_ref, o_ref):
    ct = ct_ref[...]
    sc = ct * jax.nn.sigmoid(ct)
    w = w_ref[...]
    for r in range(ct.shape[1]):
        o_ref[0, r:r + 1, :] = jnp.sum(w * sc[:, r:r + 1], axis=0, keepdims=True) + b_ref[...]


def _adaln_mod(c, w_ada, b_ada):
    b, d = c.shape
    return pl.pallas_call(
        _mod_kernel,
        grid=(N_MOD,),
        in_specs=[
            pl.BlockSpec((d, b), lambda j: (0, 0)),
            pl.BlockSpec((d, d), lambda j: (0, j)),
            pl.BlockSpec((1, d), lambda j: (0, j)),
        ],
        out_specs=pl.BlockSpec((1, b, d), lambda j: (j, 0, 0)),
        out_shape=jax.ShapeDtypeStruct((N_MOD, b, d), F32),
        compiler_params=pltpu.CompilerParams(dimension_semantics=("arbitrary",)),
        name="adaln_mod",
    )(c.T, w_ada, b_ada.reshape(1, -1))


def _load_weights_as_bf16(pairs, stage_ref, sems):
    chunks = [(src, dst, c) for src, dst in pairs
              for c in range(src.shape[0] // CAST_CHUNK_ROWS)]

    def copy(k):
        src, _, c = chunks[k]
        return pltpu.make_async_copy(
            src.at[pl.ds(c * CAST_CHUNK_ROWS, CAST_CHUNK_ROWS), :],
            stage_ref.at[k % 2, :, pl.ds(0, src.shape[1])],
            sems.at[k % 2])

    copy(0).start()
    for k, (src, dst, c) in enumerate(chunks):
        if k + 1 < len(chunks):
            copy(k + 1).start()
        copy(k).wait()
        dst[c * CAST_CHUNK_ROWS:(c + 1) * CAST_CHUNK_ROWS, :] = (
            stage_ref[k % 2, :, 0:src.shape[1]].astype(BF16))


def _layer_kernel(sinks_ref, x_ref, pos_ref, freq_ref, mod_ref, n1_ref, n2_ref, nf_ref,
                  pscale_ref, win_hbm, wpool_hbm, wout_hbm, wg_hbm, wu_hbm, wd_hbm,
                  o_ref,
                  win_ref, wpool_ref, wout_ref, wg_ref, wu_ref, wd_ref, stage_ref, cast_sems,
                  cos_ref, sin_ref, h_ref, q_ref, k_ref, v_ref, ext_ref, mix_ref, x1_ref, h2_ref,
                  act_ref, *, tiles_per_seq):
    t = x_ref.shape[1]
    s = pl.program_id(0)
    n_tiles = pl.num_programs(0) - 1
    tile_f = jnp.minimum(s, n_tiles - 1)
    first = (tile_f % tiles_per_seq) == 0
    slot_f = s % 2
    slot_b = 1 - slot_f
    seq_f = tile_f // tiles_per_seq
    seq_b = jnp.maximum(s - 1, 0) // tiles_per_seq

    def mod_row(k, seq):
        return mod_ref[k, pl.ds(seq, 1), :]

    @pl.when(s == 0)
    def _():
        _load_weights_as_bf16(
            [(win_hbm, win_ref), (wpool_hbm, wpool_ref), (wout_hbm, wout_ref),
             (wg_hbm, wg_ref), (wu_hbm, wu_ref), (wd_hbm, wd_ref)], stage_ref, cast_sems)
        x1_ref[1] = jnp.zeros((t, D_MODEL), F32)
        k_ref[t:, :] = jnp.zeros((WINDOW, 2 * KV_WIDTH), BF16)
        v_ref[t:, :] = jnp.zeros((WINDOW, 2 * KV_WIDTH), BF16)
        ext_ref[t:, :] = jnp.zeros((POOL_HALO, POOL_WIDTH), F32)

    lane = lax.broadcasted_iota(jnp.int32, (1, LANES), 1)
    low_lanes = lane < HEAD_DIM
    first_half = (lane % HEAD_DIM) < HALF

    def f_norm():
        x = x_ref[0]
        scale = n1_ref[...] * (1.0 + mod_row(1, seq_f))
        h_ref[...] = (x * _rms_scale(x) * scale + mod_row(0, seq_f)).astype(BF16)

    def f_trig():
        ang = pos_ref[...].astype(F32) * freq_ref[...]
        cos, sin = jnp.cos(ang), jnp.sin(ang)
        per_row = LANES // HALF
        for m in range(per_row):
            c32 = cos[:, m * HALF:(m + 1) * HALF]
            s32 = sin[:, m * HALF:(m + 1) * HALF]
            rows = pl.ds(m, t // per_row, stride=per_row)
            cos_ref[rows, :] = jnp.concatenate([c32, c32, c32, c32], axis=-1)
            sin_ref[rows, :] = jnp.concatenate([-s32, s32, -s32, s32], axis=-1)

    def rope(tile):
        partner = jnp.where(first_half,
                            pltpu.roll(tile, LANES - HALF, 1),
                            pltpu.roll(tile, HALF, 1))
        return tile * cos_ref[...] + partner * sin_ref[...]

    def f_q():
        u = jnp.dot(h_ref[...], win_ref[:, :ATTN_WIDTH], preferred_element_type=F32)
        q_scale = 1.0 / math.sqrt(HEAD_DIM)
        for j in range(ATTN_WIDTH // LANES):
            cols = slice(j * LANES, (j + 1) * LANES)
            q_ref[:, cols] = (rope(u[:, cols]) * q_scale).astype(BF16)

    def f_kv():
        k_ref[0:WINDOW, :] = k_ref[t:, :]
        v_ref[0:WINDOW, :] = v_ref[t:, :]
        u = jnp.dot(h_ref[...], win_ref[:, ATTN_WIDTH:ATTN_WIDTH + 2 * KV_WIDTH],
                    preferred_element_type=F32)
        kr = rope(u[:, :KV_WIDTH])
        vv = u[:, KV_WIDTH:]
        for g in range(N_KV_HEADS):
            kg = kr[:, g * HEAD_DIM:(g + 1) * HEAD_DIM]
            vg = vv[:, g * HEAD_DIM:(g + 1) * HEAD_DIM]
            cols = slice(g * LANES, (g + 1) * LANES)
            k_ref[WINDOW:, cols] = jnp.concatenate([kg, kg], axis=-1).astype(BF16)
            v_ref[WINDOW:, cols] = jnp.concatenate([vg, vg], axis=-1).astype(BF16)

    def f_up():
        halo = ext_ref[t:, :]
        ext_ref[0:POOL_HALO, :] = jnp.where(first, jnp.zeros_like(halo), halo)
        ext_ref[POOL_HALO:, :] = jnp.dot(h_ref[...], win_ref[:, ATTN_WIDTH + 2 * KV_WIDTH:],
                                         preferred_element_type=F32)

    def f_pool(gi, w):
        def task():
            cols = slice(gi * POOL_GROUP_WIDTH, (gi + 1) * POOL_GROUP_WIDTH)
            pos_in_seq = ((tile_f % tiles_per_seq) * t
                          + lax.broadcasted_iota(jnp.int32, (t, 1), 0))
            tok = ext_ref[POOL_HALO:, cols]
            total = tok
            for back in range(1, w):
                total = total + ext_ref[POOL_HALO - back:POOL_HALO - back + t, cols]
            count = jnp.minimum(pos_in_seq + 1, w).astype(F32)
            pooled = (total / count - tok).astype(BF16)
            po = jnp.dot(pooled, wpool_ref[cols, :], preferred_element_type=F32) * pscale_ref[:, cols]
            mix_ref[:, ATTN_WIDTH + gi * POOL_GROUP_WIDTH:
                    ATTN_WIDTH + (gi + 1) * POOL_GROUP_WIDTH] = po.astype(BF16)
        return task

    n_blocks = t // WINDOW
    chunks = N_HEADS // 2
    bodies = [(j, chunk) for j in range(n_blocks) for chunk in range(chunks)]
    kv_cache, logits, probs = {}, {}, {}

    def band_mask(j):
        qi = lax.broadcasted_iota(jnp.int32, (WINDOW, 2 * WINDOW), 0)
        kj = lax.broadcasted_iota(jnp.int32, (WINDOW, 2 * WINDOW), 1)
        rel = kj - WINDOW - qi
        band = (rel <= 0) & (rel > -WINDOW)
        if j == 0:
            band = band & (kj >= jnp.where(first, WINDOW, 0))
        return band

    def block_diag_kv(j, g):
        if (j, g) not in kv_cache:
            rows = slice(j * WINDOW, (j + 2) * WINDOW)
            cols = slice(g * LANES, (g + 1) * LANES)
            kd = k_ref[rows, cols]
            vd = v_ref[rows, cols]
            zero = jnp.zeros_like(kd)
            k_bd = jnp.concatenate([jnp.where(low_lanes, kd, zero),
                                    jnp.where(low_lanes, zero, kd)], axis=0)
            v_bd = jnp.concatenate([jnp.where(low_lanes, vd, zero),
                                    jnp.where(low_lanes, zero, vd)], axis=0)
            kv_cache[(j, g)] = (k_bd, v_bd)
        return kv_cache[(j, g)]

    def scores(j, chunk):
        k_bd, _ = block_diag_kv(j, chunk // (GROUP // 2))
        qc = q_ref[j * WINDOW:(j + 1) * WINDOW, chunk * LANES:(chunk + 1) * LANES]
        return lax.dot_general(qc, k_bd, (((1,), (1,)), ((), ())),
                               preferred_element_type=F32)

    def softmax(j, chunk, lg):
        mask = band_mask(j)
        ps, inv_den = [], []
        for hh in range(2):
            sink = sinks_ref[2 * chunk + hh]
            l = jnp.where(mask, lg[:, hh * 2 * WINDOW:(hh + 1) * 2 * WINDOW], -jnp.inf)
            m = jnp.maximum(jnp.max(l, axis=-1, keepdims=True), sink)
            p = jnp.exp(l - m)
            den = jnp.sum(p, axis=-1, keepdims=True) + jnp.exp(sink - m)
            ps.append(p.astype(BF16))
            inv_den.append(1.0 / den)
        return jnp.concatenate(ps, axis=-1), jnp.where(low_lanes, inv_den[0], inv_den[1])

    def values(j, chunk, p, inv_den):
        _, v_bd = block_diag_kv(j, chunk // (GROUP // 2))
        pv = jnp.dot(p, v_bd, preferred_element_type=F32)
        mix_ref[j * WINDOW:(j + 1) * WINDOW, chunk * LANES:(chunk + 1) * LANES] = (
            (pv * inv_den).astype(BF16))

    def out_proj(j):
        rows = slice(j * WINDOW, (j + 1) * WINDOW)
        mixed = jnp.dot(mix_ref[rows, :], wout_ref[...], preferred_element_type=F32)
        x1_ref[slot_f, rows, :] = x_ref[0, rows, :] + mod_row(2, seq_f) * mixed

    def f_attn(step):
        def task():
            n = len(bodies)
            if step < n:
                logits[step] = scores(*bodies[step])
            if 0 <= step - 1 < n:
                probs[step - 1] = softmax(*bodies[step - 1], logits.pop(step - 1))
            done = step - ATTN_SKEW
            if 0 <= done < n:
                values(*bodies[done], *probs.pop(done))
            ready = step - ATTN_SKEW - OUT_PROJ_LAG
            if 0 <= ready < n and bodies[ready][1] == chunks - 1:
                out_proj(bodies[ready][0])
        return task

    front = [f_norm, f_trig, f_q, f_kv, f_up]
    front += [f_pool(gi, w) for gi, w in enumerate(POOL_WINDOWS)]
    front += [f_attn(step) for step in range(len(bodies) + ATTN_SKEW + OUT_PROJ_LAG)]

    def b_norm():
        x1 = x1_ref[slot_b]
        scale = n2_ref[...] * (1.0 + mod_row(4, seq_b))
        h2_ref[...] = (x1 * _rms_scale(x1) * scale + mod_row(3, seq_b)).astype(BF16)

    def b_gate_up(n):
        def task():
            cols = slice(n * FF_CHUNK, (n + 1) * FF_CHUNK)
            g = jnp.dot(h2_ref[...], wg_ref[:, cols], preferred_element_type=F32)
            u = jnp.dot(h2_ref[...], wu_ref[:, cols], preferred_element_type=F32)
            act_ref[:, cols] = (g * jax.nn.sigmoid(g) * u).astype(BF16)
        return task

    def b_down(n):
        def task():
            cols = slice(n * DOWN_CHUNK, (n + 1) * DOWN_CHUNK)
            ff = jnp.dot(act_ref[...], wd_ref[:, cols], preferred_element_type=F32)
            o_ref[0, :, cols] = x1_ref[slot_b, :, cols] + mod_row(5, seq_b)[:, cols] * ff
        return task

    def b_final():
        x2 = o_ref[0]
        o_ref[0] = x2 * _rms_scale(x2) * nf_ref[...]

    d_ff = wg_ref.shape[1]
    back = [b_norm]
    back += [b_gate_up(n) for n in range(d_ff // FF_CHUNK)]
    back += [b_down(n) for n in range(D_MODEL // DOWN_CHUNK)]
    back += [b_final]

    for task in _interleave(front, back):
        task()


def _layer(x, mod, positions, sinks, norm1, norm2, norm_f, w_in, w_pool, pool_scale, w_out,
           w_gate, w_up, w_down):
    b, s, d = x.shape
    t = TOKEN_TILE
    d_ff = w_gate.shape[1]
    assert d == D_MODEL and s % t == 0 and t % WINDOW == 0 and d_ff % FF_CHUNK == 0
    assert d_ff % CAST_CHUNK_ROWS == 0 and d % CAST_CHUNK_ROWS == 0
    tiles_per_seq = s // t
    n_tiles = b * tiles_per_seq

    def front_tile(step):
        return jnp.minimum(step, n_tiles - 1)

    def back_tile(step):
        return jnp.maximum(step - 1, 0)

    def tok_map(tile_of):
        return lambda step, *_: (tile_of(step) // tiles_per_seq, tile_of(step) % tiles_per_seq, 0)

    per_row = LANES // HALF
    inv_freq = ROPE_THETA ** (-jnp.arange(HALF, dtype=F32) * (2.0 / HEAD_DIM))
    freq = jnp.tile(inv_freq, per_row).reshape(1, LANES)
    pos_rep = jnp.repeat(positions.reshape(b * s // per_row, per_row), HALF, axis=1)
    grid_spec = pltpu.PrefetchScalarGridSpec(
        num_scalar_prefetch=1,
        grid=(n_tiles + 1,),
        in_specs=[
            pl.BlockSpec((1, t, d), tok_map(front_tile)),
            pl.BlockSpec((t // per_row, LANES), lambda step, *_: (front_tile(step), 0)),
            _const_spec((1, LANES)),
            _const_spec((N_MOD, b, d)),
            _const_spec((1, d)), _const_spec((1, d)), _const_spec((1, d)),
            _const_spec((1, POOL_WIDTH)),
        ] + [pl.BlockSpec(memory_space=pl.ANY)] * 6,
        out_specs=pl.BlockSpec((1, t, d), tok_map(back_tile)),
        scratch_shapes=[
            pltpu.VMEM((d, IN_PROJ_WIDTH), BF16),
            pltpu.VMEM((POOL_WIDTH, POOL_GROUP_WIDTH), BF16),
            pltpu.VMEM((ATTN_WIDTH + POOL_WIDTH, d), BF16),
            pltpu.VMEM((d, d_ff), BF16),
            pltpu.VMEM((d, d_ff), BF16),
            pltpu.VMEM((d_ff, d), BF16),
            pltpu.VMEM((2, CAST_CHUNK_ROWS, max(d_ff, IN_PROJ_WIDTH)), F32),
            pltpu.SemaphoreType.DMA((2,)),
            pltpu.VMEM((t, LANES), F32),
            pltpu.VMEM((t, LANES), F32),
            pltpu.VMEM((t, d), BF16),
            pltpu.VMEM((t, ATTN_WIDTH), BF16),
            pltpu.VMEM((t + WINDOW, 2 * KV_WIDTH), BF16),
            pltpu.VMEM((t + WINDOW, 2 * KV_WIDTH), BF16),
            pltpu.VMEM((t + POOL_HALO, POOL_WIDTH), F32),
            pltpu.VMEM((t, ATTN_WIDTH + POOL_WIDTH), BF16),
            pltpu.VMEM((2, t, d), F32),
            pltpu.VMEM((t, d), BF16),
            pltpu.VMEM((t, d_ff), BF16),
        ],
    )
    return pl.pallas_call(
        functools.partial(_layer_kernel, tiles_per_seq=tiles_per_seq),
        grid_spec=grid_spec,
        out_shape=jax.ShapeDtypeStruct((b, s, d), F32),
        compiler_params=pltpu.CompilerParams(
            dimension_semantics=("arbitrary",), vmem_limit_bytes=VMEM_LIMIT_BYTES),
        name="layer",
    )(sinks, x, pos_rep, freq, mod, norm1.reshape(1, d), norm2.reshape(1, d), norm_f.reshape(1, d),
      pool_scale.reshape(1, -1), w_in, w_pool.reshape(POOL_WIDTH, POOL_GROUP_WIDTH), w_out,
      w_gate, w_up, w_down)


def kernel(x, c, positions, w_ada, b_ada, norm1, w_in, sinks, w_pool, pool_scale,
           w_out, norm2, w_gate, w_up, w_down, norm_f):
    mod = _adaln_mod(c, w_ada, b_ada)
    return _layer(x, mod, positions, sinks, norm1, norm2, norm_f, w_in, w_pool, pool_scale,
                  w_out, w_gate, w_up, w_down)
```

```python
import functools
import math

import jax
import jax.numpy as jnp
from jax import lax
from jax.experimental import pallas as pl
from jax.experimental.pallas import tpu as pltpu

F32 = jnp.float32
BF16 = jnp.bfloat16

D_MODEL = 1024
HEAD_DIM = 64
N_HEADS = 8
N_KV_HEADS = 2
GROUP = N_HEADS // N_KV_HEADS
ATTN_WIDTH = N_HEADS * HEAD_DIM
KV_WIDTH = N_KV_HEADS * HEAD_DIM
POOL_WINDOWS = (2, 4, 8, 16)
POOL_GROUP_WIDTH = 128
POOL_WIDTH = POOL_GROUP_WIDTH * len(POOL_WINDOWS)
IN_PROJ_WIDTH = ATTN_WIDTH + 2 * KV_WIDTH + POOL_WIDTH
WINDOW = 128
ROPE_THETA = 10000.0
N_MOD = 6
RMS_EPS = 1e-6
HALF = HEAD_DIM // 2

LANES = 128
POOL_HALO = 16
VMEM_LIMIT_BYTES = 56 * 1024 * 1024

TOKEN_TILE = 512
FF_CHUNK = 256
DOWN_CHUNK = 256
ATTN_SKEW = 3
CAST_CHUNK_ROWS = 128
CAST_SLOTS = 4
OUT_PROJ_LAG = 2
BACK_SPAN = 0.9


def _const_spec(shape):
    zeros = (0,) * len(shape)
    return pl.BlockSpec(shape, lambda *_: zeros, pipeline_mode=pl.Buffered(1))


def _rms_scale(x):
    return lax.rsqrt(jnp.mean(x * x, axis=-1, keepdims=True) + RMS_EPS)


def _interleave(front, back):
    keyed = [((k + 0.5) / len(front), 0, k, f) for k, f in enumerate(front)]
    keyed += [(BACK_SPAN * (k + 0.5) / len(back), 1, k, f) for k, f in enumerate(back)]
    return [f for *_, f in sorted(keyed, key=lambda e: e[:3])]


def _mod_kernel(ct_ref, w_ref, b_ref, o_ref):
    ct = ct_ref[...]
    sc = ct * jax.nn.sigmoid(ct)
    w = w_ref[...]
    for r in range(ct.shape[1]):
        o_ref[0, r:r + 1, :] = jnp.sum(w * sc[:, r:r + 1], axis=0, keepdims=True) + b_ref[...]


def _adaln_mod(c, w_ada, b_ada):
    b, d = c.shape
    return pl.pallas_call(
        _mod_kernel,
        grid=(N_MOD,),
        in_specs=[
            pl.BlockSpec((d, b), lambda j: (0, 0)),
            pl.BlockSpec((d, d), lambda j: (0, j)),
            pl.BlockSpec((1, d), lambda j: (0, j)),
        ],
        out_specs=pl.BlockSpec((1, b, d), lambda j: (j, 0, 0)),
        out_shape=jax.ShapeDtypeStruct((N_MOD, b, d), F32),
        compiler_params=pltpu.CompilerParams(dimension_semantics=("arbitrary",)),
        name="adaln_mod",
    )(c.T, w_ada, b_ada.reshape(1, -1))


def _load_weights_as_bf16(pairs, stage_ref, sems):
    chunks = [(src, dst, c) for src, dst in pairs
              for c in range(src.shape[0] // CAST_CHUNK_ROWS)]

    slots = stage_ref.shape[0]

    def copy(k):
        src, _, c = chunks[k]
        return pltpu.make_async_copy(
            src.at[pl.ds(c * CAST_CHUNK_ROWS, CAST_CHUNK_ROWS), :],
            stage_ref.at[k % slots, :, pl.ds(0, src.shape[1])],
            sems.at[k % slots])

    for k in range(min(slots - 1, len(chunks))):
        copy(k).start()
    for k, (src, dst, c) in enumerate(chunks):
        if k + slots - 1 < len(chunks):
            copy(k + slots - 1).start()
        copy(k).wait()
        dst[c * CAST_CHUNK_ROWS:(c + 1) * CAST_CHUNK_ROWS, :] = (
            stage_ref[k % slots, :, 0:src.shape[1]].astype(BF16))


def _layer_kernel(sinks_ref, x_ref, pos_ref, freq_ref, mod_ref, n1_ref, n2_ref, nf_ref,
                  pscale_ref, win_hbm, wpool_hbm, wout_hbm, wg_hbm, wu_hbm, wd_hbm,
                  o_ref,
                  win_ref, wpool_ref, wout_ref, wg_ref, wu_ref, wd_ref, stage_ref, cast_sems,
                  cos_ref, sin_ref, h_ref, q_ref, k_ref, v_ref, ext_ref, mix_ref, x1_ref, h2_ref,
                  act_ref, *, tiles_per_seq):
    t = x_ref.shape[1]
    s = pl.program_id(0)
    n_tiles = pl.num_programs(0) - 1
    tile_f = jnp.minimum(s, n_tiles - 1)
    first = (tile_f % tiles_per_seq) == 0
    slot_f = s % 2
    slot_b = 1 - slot_f
    seq_f = tile_f // tiles_per_seq
    seq_b = jnp.maximum(s - 1, 0) // tiles_per_seq

    def mod_row(k, seq):
        return mod_ref[k, pl.ds(seq, 1), :]

    @pl.when(s == 0)
    def _():
        _load_weights_as_bf16(
            [(win_hbm, win_ref), (wpool_hbm, wpool_ref), (wout_hbm, wout_ref),
             (wg_hbm, wg_ref), (wu_hbm, wu_ref), (wd_hbm, wd_ref)], stage_ref, cast_sems)
        x1_ref[1] = jnp.zeros((t, D_MODEL), F32)
        k_ref[t:, :] = jnp.zeros((WINDOW, 2 * KV_WIDTH), BF16)
        v_ref[t:, :] = jnp.zeros((WINDOW, 2 * KV_WIDTH), BF16)
        ext_ref[t:, :] = jnp.zeros((POOL_HALO, POOL_WIDTH), F32)

    lane = lax.broadcasted_iota(jnp.int32, (1, LANES), 1)
    low_lanes = lane < HEAD_DIM
    first_half = (lane % HEAD_DIM) < HALF

    def f_norm():
        x = x_ref[0]
        scale = n1_ref[...] * (1.0 + mod_row(1, seq_f))
        h_ref[...] = (x * _rms_scale(x) * scale + mod_row(0, seq_f)).astype(BF16)

    def f_trig():
        ang = pos_ref[...].astype(F32) * freq_ref[...]
        cos, sin = jnp.cos(ang), jnp.sin(ang)
        per_row = LANES // HALF
        for m in range(per_row):
            c32 = cos[:, m * HALF:(m + 1) * HALF]
            s32 = sin[:, m * HALF:(m + 1) * HALF]
            rows = pl.ds(m, t // per_row, stride=per_row)
            cos_ref[rows, :] = jnp.concatenate([c32, c32, c32, c32], axis=-1)
            sin_ref[rows, :] = jnp.concatenate([-s32, s32, -s32, s32], axis=-1)

    def rope(tile):
        partner = jnp.where(first_half,
                            pltpu.roll(tile, LANES - HALF, 1),
                            pltpu.roll(tile, HALF, 1))
        return tile * cos_ref[...] + partner * sin_ref[...]

    def f_q():
        u = jnp.dot(h_ref[...], win_ref[:, :ATTN_WIDTH], preferred_element_type=F32)
        q_scale = 1.0 / math.sqrt(HEAD_DIM)
        for j in range(ATTN_WIDTH // LANES):
            cols = slice(j * LANES, (j + 1) * LANES)
            q_ref[:, cols] = (rope(u[:, cols]) * q_scale).astype(BF16)

    def f_kv():
        k_ref[0:WINDOW, :] = k_ref[t:, :]
        v_ref[0:WINDOW, :] = v_ref[t:, :]
        u = jnp.dot(h_ref[...], win_ref[:, ATTN_WIDTH:ATTN_WIDTH + 2 * KV_WIDTH],
                    preferred_element_type=F32)
        kr = rope(u[:, :KV_WIDTH])
        vv = u[:, KV_WIDTH:]
        for g in range(N_KV_HEADS):
            kg = kr[:, g * HEAD_DIM:(g + 1) * HEAD_DIM]
            vg = vv[:, g * HEAD_DIM:(g + 1) * HEAD_DIM]
            cols = slice(g * LANES, (g + 1) * LANES)
            k_ref[WINDOW:, cols] = jnp.concatenate([kg, kg], axis=-1).astype(BF16)
            v_ref[WINDOW:, cols] = jnp.concatenate([vg, vg], axis=-1).astype(BF16)

    def f_up():
        halo = ext_ref[t:, :]
        ext_ref[0:POOL_HALO, :] = jnp.where(first, jnp.zeros_like(halo), halo)
        ext_ref[POOL_HALO:, :] = jnp.dot(h_ref[...], win_ref[:, ATTN_WIDTH + 2 * KV_WIDTH:],
                                         preferred_element_type=F32)

    def f_pool(gi, w):
        def task():
            cols = slice(gi * POOL_GROUP_WIDTH, (gi + 1) * POOL_GROUP_WIDTH)
            pos_in_seq = ((tile_f % tiles_per_seq) * t
                          + lax.broadcasted_iota(jnp.int32, (t, 1), 0))
            tok = ext_ref[POOL_HALO:, cols]
            total = tok
            for back in range(1, w):
                total = total + ext_ref[POOL_HALO - back:POOL_HALO - back + t, cols]
            count = jnp.minimum(pos_in_seq + 1, w).astype(F32)
            pooled = (total / count - tok).astype(BF16)
            po = jnp.dot(pooled, wpool_ref[cols, :], preferred_element_type=F32) * pscale_ref[:, cols]
            mix_ref[:, ATTN_WIDTH + gi * POOL_GROUP_WIDTH:
                    ATTN_WIDTH + (gi + 1) * POOL_GROUP_WIDTH] = po.astype(BF16)
        return task

    n_blocks = t // WINDOW
    chunks = N_HEADS // 2
    bodies = [(j, chunk) for j in range(n_blocks) for chunk in range(chunks)]
    kv_cache, logits, probs = {}, {}, {}

    def band_mask(j):
        qi = lax.broadcasted_iota(jnp.int32, (WINDOW, 2 * WINDOW), 0)
        kj = lax.broadcasted_iota(jnp.int32, (WINDOW, 2 * WINDOW), 1)
        rel = kj - WINDOW - qi
        band = (rel <= 0) & (rel > -WINDOW)
        if j == 0:
            band = band & (kj >= jnp.where(first, WINDOW, 0))
        return band

    def block_diag_kv(j, g):
        if (j, g) not in kv_cache:
            rows = slice(j * WINDOW, (j + 2) * WINDOW)
            cols = slice(g * LANES, (g + 1) * LANES)
            kd = k_ref[rows, cols]
            vd = v_ref[rows, cols]
            zero = jnp.zeros_like(kd)
            k_bd = jnp.concatenate([jnp.where(low_lanes, kd, zero),
                                    jnp.where(low_lanes, zero, kd)], axis=0)
            v_bd = jnp.concatenate([jnp.where(low_lanes, vd, zero),
                                    jnp.where(low_lanes, zero, vd)], axis=0)
            kv_cache[(j, g)] = (k_bd, v_bd)
        return kv_cache[(j, g)]

    def scores(j, chunk):
        k_bd, _ = block_diag_kv(j, chunk // (GROUP // 2))
        qc = q_ref[j * WINDOW:(j + 1) * WINDOW, chunk * LANES:(chunk + 1) * LANES]
        return lax.dot_general(qc, k_bd, (((1,), (1,)), ((), ())),
                               preferred_element_type=F32)

    def softmax(j, chunk, lg):
        mask = band_mask(j)
        ps, inv_den = [], []
        for hh in range(2):
            sink = sinks_ref[2 * chunk + hh]
            l = jnp.where(mask, lg[:, hh * 2 * WINDOW:(hh + 1) * 2 * WINDOW], -jnp.inf)
            m = jnp.maximum(jnp.max(l, axis=-1, keepdims=True), sink)
            p = jnp.exp(l - m)
            den = jnp.sum(p, axis=-1, keepdims=True) + jnp.exp(sink - m)
            ps.append(p.astype(BF16))
            inv_den.append(1.0 / den)
        return jnp.concatenate(ps, axis=-1), jnp.where(low_lanes, inv_den[0], inv_den[1])

    def values(j, chunk, p, inv_den):
        _, v_bd = block_diag_kv(j, chunk // (GROUP // 2))
        pv = jnp.dot(p, v_bd, preferred_element_type=F32)
        mix_ref[j * WINDOW:(j + 1) * WINDOW, chunk * LANES:(chunk + 1) * LANES] = (
            (pv * inv_den).astype(BF16))

    def out_proj(j):
        rows = slice(j * WINDOW, (j + 1) * WINDOW)
        mixed = jnp.dot(mix_ref[rows, :], wout_ref[...], preferred_element_type=F32)
        x1_ref[slot_f, rows, :] = x_ref[0, rows, :] + mod_row(2, seq_f) * mixed

    def f_attn(step):
        def task():
            n = len(bodies)
            if step < n:
                logits[step] = scores(*bodies[step])
            if 0 <= step - 1 < n:
                probs[step - 1] = softmax(*bodies[step - 1], logits.pop(step - 1))
            done = step - ATTN_SKEW
            if 0 <= done < n:
                values(*bodies[done], *probs.pop(done))
            ready = step - ATTN_SKEW - OUT_PROJ_LAG
            if 0 <= ready < n and bodies[ready][1] == chunks - 1:
                out_proj(bodies[ready][0])
        return task

    front = [f_norm, f_trig, f_q, f_kv, f_up]
    front += [f_pool(gi, w) for gi, w in enumerate(POOL_WINDOWS)]
    front += [f_attn(step) for step in range(len(bodies) + ATTN_SKEW + OUT_PROJ_LAG)]

    def b_norm():
        x1 = x1_ref[slot_b]
        scale = n2_ref[...] * (1.0 + mod_row(4, seq_b))
        h2_ref[...] = (x1 * _rms_scale(x1) * scale + mod_row(3, seq_b)).astype(BF16)

    def b_gate_up(n):
        def task():
            cols = slice(n * FF_CHUNK, (n + 1) * FF_CHUNK)
            g = jnp.dot(h2_ref[...], wg_ref[:, cols], preferred_element_type=F32)
            u = jnp.dot(h2_ref[...], wu_ref[:, cols], preferred_element_type=F32)
            act_ref[:, cols] = (g * jax.nn.sigmoid(g) * u).astype(BF16)
        return task

    def b_down(n):
        def task():
            cols = slice(n * DOWN_CHUNK, (n + 1) * DOWN_CHUNK)
            ff = jnp.dot(act_ref[...], wd_ref[:, cols], preferred_element_type=F32)
            o_ref[0, :, cols] = x1_ref[slot_b, :, cols] + mod_row(5, seq_b)[:, cols] * ff
        return task

    def b_final():
        x2 = o_ref[0]
        o_ref[0] = x2 * _rms_scale(x2) * nf_ref[...]

    d_ff = wg_ref.shape[1]
    back = [b_norm]
    back += [b_gate_up(n) for n in range(d_ff // FF_CHUNK)]
    back += [b_down(n) for n in range(D_MODEL // DOWN_CHUNK)]
    back += [b_final]

    for task in _interleave(front, back):
        task()


def _layer(x, mod, positions, sinks, norm1, norm2, norm_f, w_in, w_pool, pool_scale, w_out,
           w_gate, w_up, w_down):
    b, s, d = x.shape
    t = TOKEN_TILE
    d_ff = w_gate.shape[1]
    assert d == D_MODEL and s % t == 0 and t % WINDOW == 0 and d_ff % FF_CHUNK == 0
    assert d_ff % CAST_CHUNK_ROWS == 0 and d % CAST_CHUNK_ROWS == 0
    tiles_per_seq = s // t
    n_tiles = b * tiles_per_seq

    def front_tile(step):
        return jnp.minimum(step, n_tiles - 1)

    def back_tile(step):
        return jnp.maximum(step - 1, 0)

    def tok_map(tile_of):
        return lambda step, *_: (tile_of(step) // tiles_per_seq, tile_of(step) % tiles_per_seq, 0)

    per_row = LANES // HALF
    inv_freq = ROPE_THETA ** (-jnp.arange(HALF, dtype=F32) * (2.0 / HEAD_DIM))
    freq = jnp.tile(inv_freq, per_row).reshape(1, LANES)
    pos_rep = jnp.repeat(positions.reshape(b * s // per_row, per_row), HALF, axis=1)
    grid_spec = pltpu.PrefetchScalarGridSpec(
        num_scalar_prefetch=1,
        grid=(n_tiles + 1,),
        in_specs=[
            pl.BlockSpec((1, t, d), tok_map(front_tile)),
            pl.BlockSpec((t // per_row, LANES), lambda step, *_: (front_tile(step), 0)),
            _const_spec((1, LANES)),
            _const_spec((N_MOD, b, d)),
            _const_spec((1, d)), _const_spec((1, d)), _const_spec((1, d)),
            _const_spec((1, POOL_WIDTH)),
        ] + [pl.BlockSpec(memory_space=pl.ANY)] * 6,
        out_specs=pl.BlockSpec((1, t, d), tok_map(back_tile)),
        scratch_shapes=[
            pltpu.VMEM((d, IN_PROJ_WIDTH), BF16),
            pltpu.VMEM((POOL_WIDTH, POOL_GROUP_WIDTH), BF16),
            pltpu.VMEM((ATTN_WIDTH + POOL_WIDTH, d), BF16),
            pltpu.VMEM((d, d_ff), BF16),
            pltpu.VMEM((d, d_ff), BF16),
            pltpu.VMEM((d_ff, d), BF16),
            pltpu.VMEM((CAST_SLOTS, CAST_CHUNK_ROWS, max(d_ff, IN_PROJ_WIDTH)), F32),
            pltpu.SemaphoreType.DMA((CAST_SLOTS,)),
            pltpu.VMEM((t, LANES), F32),
            pltpu.VMEM((t, LANES), F32),
            pltpu.VMEM((t, d), BF16),
            pltpu.VMEM((t, ATTN_WIDTH), BF16),
            pltpu.VMEM((t + WINDOW, 2 * KV_WIDTH), BF16),
            pltpu.VMEM((t + WINDOW, 2 * KV_WIDTH), BF16),
            pltpu.VMEM((t + POOL_HALO, POOL_WIDTH), F32),
            pltpu.VMEM((t, ATTN_WIDTH + POOL_WIDTH), BF16),
            pltpu.VMEM((2, t, d), F32),
            pltpu.VMEM((t, d), BF16),
            pltpu.VMEM((t, d_ff), BF16),
        ],
    )
    return pl.pallas_call(
        functools.partial(_layer_kernel, tiles_per_seq=tiles_per_seq),
        grid_spec=grid_spec,
        out_shape=jax.ShapeDtypeStruct((b, s, d), F32),
        compiler_params=pltpu.CompilerParams(
            dimension_semantics=("arbitrary",), vmem_limit_bytes=VMEM_LIMIT_BYTES),
        name="layer",
    )(sinks, x, pos_rep, freq, mod, norm1.reshape(1, d), norm2.reshape(1, d), norm_f.reshape(1, d),
      pool_scale.reshape(1, -1), w_in, w_pool.reshape(POOL_WIDTH, POOL_GROUP_WIDTH), w_out,
      w_gate, w_up, w_down)


def kernel(x, c, positions, w_ada, b_ada, norm1, w_in, sinks, w_pool, pool_scale,
           w_out, norm2, w_gate, w_up, w_down, norm_f):
    mod = _adaln_mod(c, w_ada, b_ada)
    return _layer(x, mod, positions, sinks, norm1, norm2, norm_f, w_in, w_pool, pool_scale,
                  w_out, w_gate, w_up, w_down)
```

```python
import functools
import math

import jax
import jax.numpy as jnp
from jax import lax
from jax.experimental import pallas as pl
from jax.experimental.pallas import tpu as pltpu

F32 = jnp.float32
BF16 = jnp.bfloat16

D_MODEL = 1024
HEAD_DIM = 64
N_HEADS = 8
N_KV_HEADS = 2
GROUP = N_HEADS // N_KV_HEADS
ATTN_WIDTH = N_HEADS * HEAD_DIM
KV_WIDTH = N_KV_HEADS * HEAD_DIM
POOL_WINDOWS = (2, 4, 8, 16)
POOL_GROUP_WIDTH = 128
POOL_WIDTH = POOL_GROUP_WIDTH * len(POOL_WINDOWS)
IN_PROJ_WIDTH = ATTN_WIDTH + 2 * KV_WIDTH + POOL_WIDTH
WINDOW = 128
ROPE_THETA = 10000.0
N_MOD = 6
RMS_EPS = 1e-6
HALF = HEAD_DIM // 2

LANES = 128
POOL_HALO = 16
VMEM_LIMIT_BYTES = 56 * 1024 * 1024

TOKEN_TILE = 512
FF_CHUNK = 256
DOWN_CHUNK = 256
ATTN_SKEW = 3
CAST_CHUNK_ROWS = 128
CAST_SLOTS = 4
OUT_PROJ_LAG = 2
BACK_SPAN = 0.9


def _const_spec(shape):
    zeros = (0,) * len(shape)
    return pl.BlockSpec(shape, lambda *_: zeros, pipeline_mode=pl.Buffered(1))


def _rms_scale(x):
    return lax.rsqrt(jnp.mean(x * x, axis=-1, keepdims=True) + RMS_EPS)


def _interleave(front, back):
    keyed = [((k + 0.5) / len(front), 0, k, f) for k, f in enumerate(front)]
    keyed += [(BACK_SPAN * (k + 0.5) / len(back), 1, k, f) for k, f in enumerate(back)]
    return [f for *_, f in sorted(keyed, key=lambda e: e[:3])]


def _mod_kernel(ct_ref, w_ref, b_ref, o_ref):
    ct = ct_ref[...]
    sc = ct * jax.nn.sigmoid(ct)
    w = w_ref[...]
    for r in range(ct.shape[1]):
        o_ref[0, r:r + 1, :] = jnp.sum(w * sc[:, r:r + 1], axis=0, keepdims=True) + b_ref[...]


def _adaln_mod(c, w_ada, b_ada):
    b, d = c.shape
    return pl.pallas_call(
        _mod_kernel,
        grid=(N_MOD,),
        in_specs=[
            pl.BlockSpec((d, b), lambda j: (0, 0)),
            pl.BlockSpec((d, d), lambda j: (0, j)),
            pl.BlockSpec((1, d), lambda j: (0, j)),
        ],
        out_specs=pl.BlockSpec((1, b, d), lambda j: (j, 0, 0)),
        out_shape=jax.ShapeDtypeStruct((N_MOD, b, d), F32),
        compiler_params=pltpu.CompilerParams(dimension_semantics=("arbitrary",)),
        name="adaln_mod",
    )(c.T, w_ada, b_ada.reshape(1, -1))


def _weight_cast_tasks(pairs, stage_ref, sems):
    chunks = [(src, dst, c) for src, dst in pairs
              for c in range(src.shape[0] // CAST_CHUNK_ROWS)]
    slots = stage_ref.shape[0]

    def copy(k):
        src, _, c = chunks[k]
        return pltpu.make_async_copy(
            src.at[pl.ds(c * CAST_CHUNK_ROWS, CAST_CHUNK_ROWS), :],
            stage_ref.at[k % slots, :, pl.ds(0, src.shape[1])],
            sems.at[k % slots])

    def prime():
        for k in range(min(slots - 1, len(chunks))):
            copy(k).start()

    def cast(k):
        def task():
            src, dst, c = chunks[k]
            if k + slots - 1 < len(chunks):
                copy(k + slots - 1).start()
            copy(k).wait()
            dst[c * CAST_CHUNK_ROWS:(c + 1) * CAST_CHUNK_ROWS, :] = (
                stage_ref[k % slots, :, 0:src.shape[1]].astype(BF16))
        return task

    return prime, [cast(k) for k in range(len(chunks))]


def _layer_kernel(sinks_ref, x_ref, pos_ref, freq_ref, mod_ref, n1_ref, n2_ref, nf_ref,
                  pscale_ref, win_hbm, wpool_hbm, wout_hbm, wg_hbm, wu_hbm, wd_hbm,
                  o_ref,
                  win_ref, wpool_ref, wout_ref, wg_ref, wu_ref, wd_ref, stage_ref, cast_sems,
                  cos_ref, sin_ref, h_ref, q_ref, k_ref, v_ref, ext_ref, mix_ref, x1_ref, h2_ref,
                  act_ref, *, tiles_per_seq):
    t = x_ref.shape[1]
    s = pl.program_id(0)
    n_tiles = pl.num_programs(0) - 1
    tile_f = jnp.minimum(s, n_tiles - 1)
    first = (tile_f % tiles_per_seq) == 0
    slot_f = s % 2
    slot_b = 1 - slot_f
    seq_f = tile_f // tiles_per_seq
    seq_b = jnp.maximum(s - 1, 0) // tiles_per_seq

    def mod_row(k, seq):
        return mod_ref[k, pl.ds(seq, 1), :]

    lane =lax.broadcasted_iota(jnp.int32, (1, LANES), 1)
    low_lanes = lane < HEAD_DIM
    first_half = (lane % HEAD_DIM) < HALF

    def f_norm():
        x = x_ref[0]
        scale = n1_ref[...] * (1.0 + mod_row(1, seq_f))
        h_ref[...] = (x * _rms_scale(x) * scale + mod_row(0, seq_f)).astype(BF16)

    def f_trig():
        ang = pos_ref[...].astype(F32) * freq_ref[...]
        cos, sin = jnp.cos(ang), jnp.sin(ang)
        per_row = LANES // HALF
        for m in range(per_row):
            c32 = cos[:, m * HALF:(m + 1) * HALF]
            s32 = sin[:, m * HALF:(m + 1) * HALF]
            rows = pl.ds(m, t // per_row, stride=per_row)
            cos_ref[rows, :] = jnp.concatenate([c32, c32, c32, c32], axis=-1)
            sin_ref[rows, :] = jnp.concatenate([-s32, s32, -s32, s32], axis=-1)

    def rope(tile):
        partner = jnp.where(first_half,
                            pltpu.roll(tile, LANES - HALF, 1),
                            pltpu.roll(tile, HALF, 1))
        return tile * cos_ref[...] + partner * sin_ref[...]

    def f_q():
        u = jnp.dot(h_ref[...], win_ref[:, :ATTN_WIDTH], preferred_element_type=F32)
        q_scale = 1.0 / math.sqrt(HEAD_DIM)
        for j in range(ATTN_WIDTH // LANES):
            cols = slice(j * LANES, (j + 1) * LANES)
            q_ref[:, cols] = (rope(u[:, cols]) * q_scale).astype(BF16)

    def f_kv():
        k_ref[0:WINDOW, :] = k_ref[t:, :]
        v_ref[0:WINDOW, :] = v_ref[t:, :]
        u = jnp.dot(h_ref[...], win_ref[:, ATTN_WIDTH:ATTN_WIDTH + 2 * KV_WIDTH],
                    preferred_element_type=F32)
        kr = rope(u[:, :KV_WIDTH])
        vv = u[:, KV_WIDTH:]
        for g in range(N_KV_HEADS):
            kg = kr[:, g * HEAD_DIM:(g + 1) * HEAD_DIM]
            vg = vv[:, g * HEAD_DIM:(g + 1) * HEAD_DIM]
            cols = slice(g * LANES, (g + 1) * LANES)
            k_ref[WINDOW:, cols] = jnp.concatenate([kg, kg], axis=-1).astype(BF16)
            v_ref[WINDOW:, cols] = jnp.concatenate([vg, vg], axis=-1).astype(BF16)

    def f_up():
        halo = ext_ref[t:, :]
        ext_ref[0:POOL_HALO, :] = jnp.where(first, jnp.zeros_like(halo), halo)
        ext_ref[POOL_HALO:, :] = jnp.dot(h_ref[...], win_ref[:, ATTN_WIDTH + 2 * KV_WIDTH:],
                                         preferred_element_type=F32)

    def f_pool(gi, w):
        def task():
            cols = slice(gi * POOL_GROUP_WIDTH, (gi + 1) * POOL_GROUP_WIDTH)
            pos_in_seq = ((tile_f % tiles_per_seq) * t
                          + lax.broadcasted_iota(jnp.int32, (t, 1), 0))
            tok = ext_ref[POOL_HALO:, cols]
            total = tok
            for back in range(1, w):
                total = total + ext_ref[POOL_HALO - back:POOL_HALO - back + t, cols]
            count = jnp.minimum(pos_in_seq + 1, w).astype(F32)
            pooled = (total / count - tok).astype(BF16)
            po = jnp.dot(pooled, wpool_ref[cols, :], preferred_element_type=F32) * pscale_ref[:, cols]
            mix_ref[:, ATTN_WIDTH + gi * POOL_GROUP_WIDTH:
                    ATTN_WIDTH + (gi + 1) * POOL_GROUP_WIDTH] = po.astype(BF16)
        return task

    n_blocks = t // WINDOW
    chunks = N_HEADS // 2
    bodies = [(j, chunk) for j in range(n_blocks) for chunk in range(chunks)]
    kv_cache, logits, probs = {}, {}, {}

    def band_mask(j):
        qi = lax.broadcasted_iota(jnp.int32, (WINDOW, 2 * WINDOW), 0)
        kj = lax.broadcasted_iota(jnp.int32, (WINDOW, 2 * WINDOW), 1)
        rel = kj - WINDOW - qi
        band = (rel <= 0) & (rel > -WINDOW)
        if j == 0:
            band = band & (kj >= jnp.where(first, WINDOW, 0))
        return band

    def block_diag_kv(j, g):
        if (j, g) not in kv_cache:
            rows = slice(j * WINDOW, (j + 2) * WINDOW)
            cols = slice(g * LANES, (g + 1) * LANES)
            kd = k_ref[rows, cols]
            vd = v_ref[rows, cols]
            zero = jnp.zeros_like(kd)
            k_bd = jnp.concatenate([jnp.where(low_lanes, kd, zero),
                                    jnp.where(low_lanes, zero, kd)], axis=0)
            v_bd = jnp.concatenate([jnp.where(low_lanes, vd, zero),
                                    jnp.where(low_lanes, zero, vd)], axis=0)
            kv_cache[(j, g)] = (k_bd, v_bd)
        return kv_cache[(j, g)]

    def scores(j, chunk):
        k_bd, _ = block_diag_kv(j, chunk // (GROUP // 2))
        qc = q_ref[j * WINDOW:(j + 1) * WINDOW, chunk * LANES:(chunk + 1) * LANES]
        return lax.dot_general(qc, k_bd, (((1,), (1,)), ((), ())),
                               preferred_element_type=F32)

    def softmax(j, chunk, lg):
        mask = band_mask(j)
        ps, inv_den = [], []
        for hh in range(2):
            sink = sinks_ref[2 * chunk + hh]
            l = jnp.where(mask, lg[:, hh * 2 * WINDOW:(hh + 1) * 2 * WINDOW], -jnp.inf)
            m = jnp.maximum(jnp.max(l, axis=-1, keepdims=True), sink)
            p = jnp.exp(l - m)
            den = jnp.sum(p, axis=-1, keepdims=True) + jnp.exp(sink - m)
            ps.append(p.astype(BF16))
            inv_den.append(1.0 / den)
        return jnp.concatenate(ps, axis=-1), jnp.where(low_lanes, inv_den[0], inv_den[1])

    def values(j, chunk, p, inv_den):
        _, v_bd = block_diag_kv(j, chunk // (GROUP // 2))
        pv = jnp.dot(p, v_bd, preferred_element_type=F32)
        mix_ref[j * WINDOW:(j + 1) * WINDOW, chunk * LANES:(chunk + 1) * LANES] = (
            (pv * inv_den).astype(BF16))

    def out_proj(j):
        rows = slice(j * WINDOW, (j + 1) * WINDOW)
        mixed = jnp.dot(mix_ref[rows, :], wout_ref[...], preferred_element_type=F32)
        x1_ref[slot_f, rows, :] = x_ref[0, rows, :] + mod_row(2, seq_f) * mixed

    def f_attn(step):
        def task():
            n = len(bodies)
            if step < n:
                logits[step] = scores(*bodies[step])
            if 0 <= step - 1 < n:
                probs[step - 1] = softmax(*bodies[step - 1], logits.pop(step - 1))
            done = step - ATTN_SKEW
            if 0 <= done < n:
                values(*bodies[done], *probs.pop(done))
            ready = step - ATTN_SKEW - OUT_PROJ_LAG
            if 0 <= ready < n and bodies[ready][1] == chunks - 1:
                out_proj(bodies[ready][0])
        return task

    front = [f_norm, f_trig, f_q, f_kv, f_up]
    front += [f_pool(gi, w) for gi, w in enumerate(POOL_WINDOWS)]
    front += [f_attn(step) for step in range(len(bodies) + ATTN_SKEW + OUT_PROJ_LAG)]

    def b_norm():
        x1 = x1_ref[slot_b]
        scale = n2_ref[...] * (1.0 + mod_row(4, seq_b))
        h2_ref[...] = (x1 * _rms_scale(x1) * scale + mod_row(3, seq_b)).astype(BF16)

    def b_gate_up(n):
        def task():
            cols = slice(n * FF_CHUNK, (n + 1) * FF_CHUNK)
            g = jnp.dot(h2_ref[...], wg_ref[:, cols], preferred_element_type=F32)
            u = jnp.dot(h2_ref[...], wu_ref[:, cols], preferred_element_type=F32)
            act_ref[:, cols] = (g * jax.nn.sigmoid(g) * u).astype(BF16)
        return task

    def b_down(n):
        def task():
            cols = slice(n * DOWN_CHUNK, (n + 1) * DOWN_CHUNK)
            ff = jnp.dot(act_ref[...], wd_ref[:, cols], preferred_element_type=F32)
            o_ref[0, :, cols] = x1_ref[slot_b, :, cols] + mod_row(5, seq_b)[:, cols] * ff
        return task

    def b_final():
        x2 = o_ref[0]
        o_ref[0] = x2 * _rms_scale(x2) * nf_ref[...]

    d_ff = wg_ref.shape[1]
    back = [b_norm]
    back += [b_gate_up(n) for n in range(d_ff // FF_CHUNK)]
    back += [b_down(n) for n in range(D_MODEL // DOWN_CHUNK)]
    back += [b_final]

    def run(tasks):
        kv_cache.clear(), logits.clear(), probs.clear()
        for task in tasks:
            task()

    @pl.when(s == 0)
    def _():
        k_ref[t:, :] = jnp.zeros((WINDOW, 2 * KV_WIDTH), BF16)
        v_ref[t:, :] = jnp.zeros((WINDOW, 2 * KV_WIDTH), BF16)
        ext_ref[t:, :] = jnp.zeros((POOL_HALO, POOL_WIDTH), F32)
        prime, casts = _weight_cast_tasks(
            [(win_hbm, win_ref), (wpool_hbm, wpool_ref), (wout_hbm, wout_ref)],
            stage_ref, cast_sems)
        prime()
        run(casts)
        prime, casts = _weight_cast_tasks(
            [(wg_hbm, wg_ref), (wu_hbm, wu_ref), (wd_hbm, wd_ref)], stage_ref, cast_sems)
        prime()
        run(_interleave(front, casts))

    @pl.when((s > 0) & (s < n_tiles))
    def _():
        run(_interleave(front, back))

    @pl.when(s == n_tiles)
    def _():
        run(back)


def _layer(x, mod, positions, sinks, norm1, norm2, norm_f, w_in, w_pool, pool_scale, w_out,
           w_gate, w_up, w_down):
    b, s, d = x.shape
    t = TOKEN_TILE
    d_ff = w_gate.shape[1]
    assert d == D_MODEL and s % t == 0 and t % WINDOW == 0 and d_ff % FF_CHUNK == 0
    assert d_ff % CAST_CHUNK_ROWS == 0 and d % CAST_CHUNK_ROWS == 0
    tiles_per_seq = s // t
    n_tiles = b * tiles_per_seq

    def front_tile(step):
        return jnp.minimum(step, n_tiles - 1)

    def back_tile(step):
        return jnp.maximum(step - 1, 0)

    def tok_map(tile_of):
        return lambda step, *_: (tile_of(step) // tiles_per_seq, tile_of(step) % tiles_per_seq, 0)

    per_row = LANES // HALF
    inv_freq = ROPE_THETA ** (-jnp.arange(HALF, dtype=F32) * (2.0 / HEAD_DIM))
    freq = jnp.tile(inv_freq, per_row).reshape(1, LANES)
    pos_rep = jnp.repeat(positions.reshape(b * s // per_row, per_row), HALF, axis=1)
    grid_spec = pltpu.PrefetchScalarGridSpec(
        num_scalar_prefetch=1,
        grid=(n_tiles + 1,),
        in_specs=[
            pl.BlockSpec((1, t, d), tok_map(front_tile)),
            pl.BlockSpec((t // per_row, LANES), lambda step, *_: (front_tile(step), 0)),
            _const_spec((1, LANES)),
            _const_spec((N_MOD, b, d)),
            _const_spec((1, d)), _const_spec((1, d)), _const_spec((1, d)),
            _const_spec((1, POOL_WIDTH)),
        ] + [pl.BlockSpec(memory_space=pl.ANY)] * 6,
        out_specs=pl.BlockSpec((1, t, d), tok_map(back_tile)),
        scratch_shapes=[
            pltpu.VMEM((d, IN_PROJ_WIDTH), BF16),
            pltpu.VMEM((POOL_WIDTH, POOL_GROUP_WIDTH), BF16),
            pltpu.VMEM((ATTN_WIDTH + POOL_WIDTH, d), BF16),
            pltpu.VMEM((d, d_ff), BF16),
            pltpu.VMEM((d, d_ff), BF16),
            pltpu.VMEM((d_ff, d), BF16),
            pltpu.VMEM((CAST_SLOTS, CAST_CHUNK_ROWS, max(d_ff, IN_PROJ_WIDTH)), F32),
            pltpu.SemaphoreType.DMA((CAST_SLOTS,)),
            pltpu.VMEM((t, LANES), F32),
            pltpu.VMEM((t, LANES), F32),
            pltpu.VMEM((t, d), BF16),
            pltpu.VMEM((t, ATTN_WIDTH), BF16),
            pltpu.VMEM((t + WINDOW, 2 * KV_WIDTH), BF16),
            pltpu.VMEM((t + WINDOW, 2 * KV_WIDTH), BF16),
            pltpu.VMEM((t + POOL_HALO, POOL_WIDTH), F32),
            pltpu.VMEM((t, ATTN_WIDTH + POOL_WIDTH), BF16),
            pltpu.VMEM((2, t, d), F32),
            pltpu.VMEM((t, d), BF16),
            pltpu.VMEM((t, d_ff), BF16),
        ],
    )
    return pl.pallas_call(
        functools.partial(_layer_kernel, tiles_per_seq=tiles_per_seq),
        grid_spec=grid_spec,
        out_shape=jax.ShapeDtypeStruct((b, s, d), F32),
        compiler_params=pltpu.CompilerParams(
            dimension_semantics=("arbitrary",), vmem_limit_bytes=VMEM_LIMIT_BYTES),
        name="layer",
    )(sinks, x, pos_rep, freq, mod, norm1.reshape(1, d), norm2.reshape(1, d), norm_f.reshape(1, d),
      pool_scale.reshape(1, -1), w_in, w_pool.reshape(POOL_WIDTH, POOL_GROUP_WIDTH), w_out,
      w_gate, w_up, w_down)


def kernel(x, c, positions, w_ada, b_ada, norm1, w_in, sinks, w_pool, pool_scale,
           w_out, norm2, w_gate, w_up, w_down, norm_f):
    mod = _adaln_mod(c, w_ada, b_ada)
    return _layer(x, mod, positions, sinks, norm1, norm2, norm_f, w_in, w_pool, pool_scale,
                  w_out, w_gate, w_up, w_down)
```

```python
import functools
import math

import jax
import jax.numpy as jnp
from jax import lax
from jax.experimental import pallas as pl
from jax.experimental.pallas import tpu as pltpu

F32 = jnp.float32
BF16 = jnp.bfloat16

D_MODEL = 1024
HEAD_DIM = 64
N_HEADS = 8
N_KV_HEADS = 2
GROUP = N_HEADS // N_KV_HEADS
ATTN_WIDTH = N_HEADS * HEAD_DIM
KV_WIDTH = N_KV_HEADS * HEAD_DIM
POOL_WINDOWS = (2, 4, 8, 16)
POOL_GROUP_WIDTH = 128
POOL_WIDTH = POOL_GROUP_WIDTH * len(POOL_WINDOWS)
IN_PROJ_WIDTH = ATTN_WIDTH + 2 * KV_WIDTH + POOL_WIDTH
WINDOW = 128
ROPE_THETA = 10000.0
N_MOD = 6
RMS_EPS = 1e-6
HALF = HEAD_DIM // 2

LANES = 128
POOL_HALO = 16
VMEM_LIMIT_BYTES = 56 * 1024 * 1024

TOKEN_TILE = 512
FF_CHUNK = 256
DOWN_CHUNK = 256
ATTN_SKEW = 3
CAST_CHUNK_ROWS = 128
CAST_SLOTS = 4
OUT_PROJ_LAG = 2
BACK_SPAN = 0.9


def _const_spec(shape):
    zeros = (0,) * len(shape)
    return pl.BlockSpec(shape, lambda *_: zeros, pipeline_mode=pl.Buffered(1))


def _rms_scale(x):
    return lax.rsqrt(jnp.mean(x * x, axis=-1, keepdims=True) + RMS_EPS)


def _interleave(front, back):
    keyed = [((k + 0.5) / len(front), 0, k, f) for k, f in enumerate(front)]
    keyed += [(BACK_SPAN * (k + 0.5) / len(back), 1, k, f) for k, f in enumerate(back)]
    return [f for *_, f in sorted(keyed, key=lambda e: e[:3])]


def _mod_kernel(ct_ref, w_ref, b_ref, o_ref):
    ct = ct_ref[...]
    sc = ct * jax.nn.sigmoid(ct)
    w = w_ref[...]
    for r in range(ct.shape[1]):
        o_ref[0, r:r + 1, :] = jnp.sum(w * sc[:, r:r + 1], axis=0, keepdims=True) + b_ref[...]


def _adaln_mod(c, w_ada, b_ada):
    b, d = c.shape
    return pl.pallas_call(
        _mod_kernel,
        grid=(N_MOD,),
        in_specs=[
            pl.BlockSpec((d, b), lambda j: (0, 0)),
            pl.BlockSpec((d, d), lambda j: (0, j)),
            pl.BlockSpec((1, d), lambda j: (0, j)),
        ],
        out_specs=pl.BlockSpec((1, b, d), lambda j: (j, 0, 0)),
        out_shape=jax.ShapeDtypeStruct((N_MOD, b, d), F32),
        compiler_params=pltpu.CompilerParams(dimension_semantics=("arbitrary",)),
        name="adaln_mod",
    )(c.T, w_ada, b_ada.reshape(1, -1))


def _weight_cast_tasks(pairs, stage_ref, sems):
    chunks = [(src, dst, c) for src, dst in pairs
              for c in range(src.shape[0] // CAST_CHUNK_ROWS)]
    slots = stage_ref.shape[0]

    def copy(k):
        src, _, c = chunks[k]
        return pltpu.make_async_copy(
            src.at[pl.ds(c * CAST_CHUNK_ROWS, CAST_CHUNK_ROWS), :],
            stage_ref.at[k % slots, :, pl.ds(0, src.shape[1])],
            sems.at[k % slots])

    def prime():
        for k in range(min(slots - 1, len(chunks))):
            copy(k).start()

    def cast(k):
        def task():
            src, dst, c = chunks[k]
            if k + slots - 1 < len(chunks):
                copy(k + slots - 1).start()
            copy(k).wait()
            dst[c * CAST_CHUNK_ROWS:(c + 1) * CAST_CHUNK_ROWS, :] = (
                stage_ref[k % slots, :, 0:src.shape[1]].astype(BF16))
        return task

    return prime, [cast(k) for k in range(len(chunks))]


def _layer_kernel(sinks_ref, x_ref, pos_ref, freq_ref, mod_ref, n1_ref, n2_ref, nf_ref,
                  pscale_ref, win_hbm, wpool_hbm, wout_hbm, wg_hbm, wu_hbm, wd_hbm,
                  o_ref,
                  win_ref, wpool_ref, wout_ref, wg_ref, wu_ref, wd_ref, stage_ref, cast_sems,
                  cos_ref, sin_ref, h_ref, q_ref, k_ref, v_ref, ext_ref, mix_ref, x1_ref, h2_ref,
                  act_ref, *, tiles_per_seq):
    t = x_ref.shape[1]
    s = pl.program_id(0)
    n_tiles = pl.num_programs(0) - 1
    tile_f = jnp.minimum(s, n_tiles - 1)
    first = (tile_f % tiles_per_seq) == 0
    slot_f = s % 2
    slot_b = 1 - slot_f
    seq_f = tile_f // tiles_per_seq
    seq_b = jnp.maximum(s - 1, 0) // tiles_per_seq

    def mod_row(k, seq):
        return mod_ref[k, pl.ds(seq, 1), :]

    lane =lax.broadcasted_iota(jnp.int32, (1, LANES), 1)
    low_lanes = lane < HEAD_DIM
    first_half = (lane % HEAD_DIM) < HALF

    def f_norm():
        x = x_ref[0]
        scale = n1_ref[...] * (1.0 + mod_row(1, seq_f))
        h_ref[...] = (x * _rms_scale(x) * scale + mod_row(0, seq_f)).astype(BF16)

    def f_trig():
        ang = pos_ref[...].astype(F32) * freq_ref[...]
        cos, sin = jnp.cos(ang), jnp.sin(ang)
        per_row = LANES // HALF
        for m in range(per_row):
            c32 = cos[:, m * HALF:(m + 1) * HALF]
            s32 = sin[:, m * HALF:(m + 1) * HALF]
            rows = pl.ds(m, t // per_row, stride=per_row)
            cos_ref[rows, :] = jnp.concatenate([c32, c32, c32, c32], axis=-1)
            sin_ref[rows, :] = jnp.concatenate([-s32, s32, -s32, s32], axis=-1)

    def rope(tile):
        partner = jnp.where(first_half,
                            pltpu.roll(tile, LANES - HALF, 1),
                            pltpu.roll(tile, HALF, 1))
        return tile * cos_ref[...] + partner * sin_ref[...]

    def f_q():
        u = jnp.dot(h_ref[...], win_ref[:, :ATTN_WIDTH], preferred_element_type=F32)
        q_scale = 1.0 / math.sqrt(HEAD_DIM)
        for j in range(ATTN_WIDTH // LANES):
            cols = slice(j * LANES, (j + 1) * LANES)
            q_ref[:, cols] = (rope(u[:, cols]) * q_scale).astype(BF16)

    def f_kv():
        k_ref[0:WINDOW, :] = k_ref[t:, :]
        v_ref[0:WINDOW, :] = v_ref[t:, :]
        u = jnp.dot(h_ref[...], win_ref[:, ATTN_WIDTH:ATTN_WIDTH + 2 * KV_WIDTH],
                    preferred_element_type=F32)
        kr = rope(u[:, :KV_WIDTH])
        vv = u[:, KV_WIDTH:]
        for g in range(N_KV_HEADS):
            kg = kr[:, g * HEAD_DIM:(g + 1) * HEAD_DIM]
            vg = vv[:, g * HEAD_DIM:(g + 1) * HEAD_DIM]
            cols = slice(g * LANES, (g + 1) * LANES)
            k_ref[WINDOW:, cols] = jnp.concatenate([kg, kg], axis=-1).astype(BF16)
            v_ref[WINDOW:, cols] = jnp.concatenate([vg, vg], axis=-1).astype(BF16)

    def f_up():
        halo = ext_ref[t:, :]
        ext_ref[0:POOL_HALO, :] = jnp.where(first, jnp.zeros_like(halo), halo)
        ext_ref[POOL_HALO:, :] = jnp.dot(h_ref[...], win_ref[:, ATTN_WIDTH + 2 * KV_WIDTH:],
                                         preferred_element_type=F32)

    def f_pool(gi, w):
        def task():
            cols = slice(gi * POOL_GROUP_WIDTH, (gi + 1) * POOL_GROUP_WIDTH)
            pos_in_seq = ((tile_f % tiles_per_seq) * t
                          + lax.broadcasted_iota(jnp.int32, (t, 1), 0))
            tok = ext_ref[POOL_HALO:, cols]
            total = tok
            for back in range(1, w):
                total = total + ext_ref[POOL_HALO - back:POOL_HALO - back + t, cols]
            count = jnp.minimum(pos_in_seq + 1, w).astype(F32)
            pooled = (total / count - tok).astype(BF16)
            po = jnp.dot(pooled, wpool_ref[cols, :], preferred_element_type=F32) * pscale_ref[:, cols]
            mix_ref[:, ATTN_WIDTH + gi * POOL_GROUP_WIDTH:
                    ATTN_WIDTH + (gi + 1) * POOL_GROUP_WIDTH] = po.astype(BF16)
        return task

    n_blocks = t // WINDOW
    chunks = N_HEADS // 2
    bodies = [(j, chunk) for j in range(n_blocks) for chunk in range(chunks)]
    kv_cache, logits, probs = {}, {}, {}

    def band_mask(j):
        qi = lax.broadcasted_iota(jnp.int32, (WINDOW, 2 * WINDOW), 0)
        kj = lax.broadcasted_iota(jnp.int32, (WINDOW, 2 * WINDOW), 1)
        rel = kj - WINDOW - qi
        band = (rel <= 0) & (rel > -WINDOW)
        if j == 0:
            band = band & (kj >= jnp.where(first, WINDOW, 0))
        return band

    def block_diag_kv(j, g):
        if (j, g) not in kv_cache:
            rows = slice(j * WINDOW, (j + 2) * WINDOW)
            cols = slice(g * LANES, (g + 1) * LANES)
            kd = k_ref[rows, cols]
            vd = v_ref[rows, cols]
            zero = jnp.zeros_like(kd)
            k_bd = jnp.concatenate([jnp.where(low_lanes, kd, zero),
                                    jnp.where(low_lanes, zero, kd)], axis=0)
            v_bd = jnp.concatenate([jnp.where(low_lanes, vd, zero),
                                    jnp.where(low_lanes, zero, vd)], axis=0)
            kv_cache[(j, g)] = (k_bd, v_bd)
        return kv_cache[(j, g)]

    def scores(j, chunk):
        k_bd, _ = block_diag_kv(j, chunk // (GROUP // 2))
        qc = q_ref[j * WINDOW:(j + 1) * WINDOW, chunk * LANES:(chunk + 1) * LANES]
        return lax.dot_general(qc, k_bd, (((1,), (1,)), ((), ())),
                               preferred_element_type=F32)

    def softmax(j, chunk, lg):
        mask = band_mask(j)
        ps, inv_den = [], []
        for hh in range(2):
            sink = sinks_ref[2 * chunk + hh]
            l = jnp.where(mask, lg[:, hh * 2 * WINDOW:(hh + 1) * 2 * WINDOW], -jnp.inf)
            m = jnp.maximum(jnp.max(l, axis=-1, keepdims=True), sink)
            p = jnp.exp(l - m)
            den = jnp.sum(p, axis=-1, keepdims=True) + jnp.exp(sink - m)
            ps.append(p.astype(BF16))
            inv_den.append(1.0 / den)
        return jnp.concatenate(ps, axis=-1), jnp.where(low_lanes, inv_den[0], inv_den[1])

    def values(j, chunk, p, inv_den):
        _, v_bd = block_diag_kv(j, chunk // (GROUP // 2))
        pv = jnp.dot(p, v_bd, preferred_element_type=F32)
        mix_ref[j * WINDOW:(j + 1) * WINDOW, chunk * LANES:(chunk + 1) * LANES] = (
            (pv * inv_den).astype(BF16))

    def out_proj(j):
        rows = slice(j * WINDOW, (j + 1) * WINDOW)
        mixed = jnp.dot(mix_ref[rows, :], wout_ref[...], preferred_element_type=F32)
        x1_ref[slot_f, rows, :] = x_ref[0, rows, :] + mod_row(2, seq_f) * mixed

    def f_attn(step):
        def task():
            n = len(bodies)
            if step < n:
                logits[step] = scores(*bodies[step])
            if 0 <= step - 1 < n:
                probs[step - 1] = softmax(*bodies[step - 1], logits.pop(step - 1))
            done = step - ATTN_SKEW
            if 0 <= done < n:
                values(*bodies[done], *probs.pop(done))
            ready = step - ATTN_SKEW - OUT_PROJ_LAG
            if 0 <= ready < n and bodies[ready][1] == chunks - 1:
                out_proj(bodies[ready][0])
        return task

    front = [f_norm, f_trig, f_q, f_kv, f_up]
    front += [f_pool(gi, w) for gi, w in enumerate(POOL_WINDOWS)]
    front += [f_attn(step) for step in range(len(bodies) + ATTN_SKEW + OUT_PROJ_LAG)]

    def b_norm():
        x1 = x1_ref[slot_b]
        scale = n2_ref[...] * (1.0 + mod_row(4, seq_b))
        h2_ref[...] = (x1 * _rms_scale(x1) * scale + mod_row(3, seq_b)).astype(BF16)

    def b_gate_up(n):
        def task():
            cols = slice(n * FF_CHUNK, (n + 1) * FF_CHUNK)
            g = jnp.dot(h2_ref[...], wg_ref[:, cols], preferred_element_type=F32)
            u = jnp.dot(h2_ref[...], wu_ref[:, cols], preferred_element_type=F32)
            act_ref[:, cols] = (g * jax.nn.sigmoid(g) * u).astype(BF16)
        return task

    def b_down(n):
        def task():
            cols = slice(n * DOWN_CHUNK, (n + 1) * DOWN_CHUNK)
            ff = jnp.dot(act_ref[...], wd_ref[:, cols], preferred_element_type=F32)
            o_ref[0, :, cols] = x1_ref[slot_b, :, cols] + mod_row(5, seq_b)[:, cols] * ff
        return task

    def b_final():
        x2 = o_ref[0]
        o_ref[0] = x2 * _rms_scale(x2) * nf_ref[...]

    d_ff = wg_ref.shape[1]
    back = [b_norm]
    back += [b_gate_up(n) for n in range(d_ff // FF_CHUNK)]
    back += [b_down(n) for n in range(D_MODEL // DOWN_CHUNK)]
    back += [b_final]

    def run(tasks):
        kv_cache.clear(), logits.clear(), probs.clear()
        for task in tasks:
            task()

    @pl.when(s == 0)
    def _():
        k_ref[t:, :] = jnp.zeros((WINDOW, 2 * KV_WIDTH), BF16)
        v_ref[t:, :] = jnp.zeros((WINDOW, 2 * KV_WIDTH), BF16)
        ext_ref[t:, :] = jnp.zeros((POOL_HALO, POOL_WIDTH), F32)
        prime, casts = _weight_cast_tasks(
            [(win_hbm, win_ref), (wpool_hbm, wpool_ref), (wout_hbm, wout_ref)],
            stage_ref, cast_sems)
        prime()
        run(casts)
        prime, casts = _weight_cast_tasks(
            [(wg_hbm, wg_ref), (wu_hbm, wu_ref), (wd_hbm, wd_ref)], stage_ref, cast_sems)
        prime()
        run(_interleave(front, casts))

    @pl.when(s > 0)
    def _():
        run(_interleave(front, back))


def _layer(x, mod, positions, sinks, norm1, norm2, norm_f, w_in, w_pool, pool_scale, w_out,
           w_gate, w_up, w_down):
    b, s, d = x.shape
    t = TOKEN_TILE
    d_ff = w_gate.shape[1]
    assert d == D_MODEL and s % t == 0 and t % WINDOW == 0 and d_ff % FF_CHUNK == 0
    assert d_ff % CAST_CHUNK_ROWS == 0 and d % CAST_CHUNK_ROWS == 0
    tiles_per_seq = s // t
    n_tiles = b * tiles_per_seq

    def front_tile(step):
        return jnp.minimum(step, n_tiles - 1)

    def back_tile(step):
        return jnp.maximum(step - 1, 0)

    def tok_map(tile_of):
        return lambda step, *_: (tile_of(step) // tiles_per_seq, tile_of(step) % tiles_per_seq, 0)

    per_row = LANES // HALF
    inv_freq = ROPE_THETA ** (-jnp.arange(HALF, dtype=F32) * (2.0 / HEAD_DIM))
    freq = jnp.tile(inv_freq, per_row).reshape(1, LANES)
    pos_rep = jnp.repeat(positions.reshape(b * s // per_row, per_row), HALF, axis=1)
    grid_spec = pltpu.PrefetchScalarGridSpec(
        num_scalar_prefetch=1,
        grid=(n_tiles + 1,),
        in_specs=[
            pl.BlockSpec((1, t, d), tok_map(front_tile)),
            pl.BlockSpec((t // per_row, LANES), lambda step, *_: (front_tile(step), 0)),
            _const_spec((1, LANES)),
            _const_spec((N_MOD, b, d)),
            _const_spec((1, d)), _const_spec((1, d)), _const_spec((1, d)),
            _const_spec((1, POOL_WIDTH)),
        ] + [pl.BlockSpec(memory_space=pl.ANY)] * 6,
        out_specs=pl.BlockSpec((1, t, d), tok_map(back_tile)),
        scratch_shapes=[
            pltpu.VMEM((d, IN_PROJ_WIDTH), BF16),
            pltpu.VMEM((POOL_WIDTH, POOL_GROUP_WIDTH), BF16),
            pltpu.VMEM((ATTN_WIDTH + POOL_WIDTH, d), BF16),
            pltpu.VMEM((d, d_ff), BF16),
            pltpu.VMEM((d, d_ff), BF16),
            pltpu.VMEM((d_ff, d), BF16),
            pltpu.VMEM((CAST_SLOTS, CAST_CHUNK_ROWS, max(d_ff, IN_PROJ_WIDTH)), F32),
            pltpu.SemaphoreType.DMA((CAST_SLOTS,)),
            pltpu.VMEM((t, LANES), F32),
            pltpu.VMEM((t, LANES), F32),
            pltpu.VMEM((t, d), BF16),
            pltpu.VMEM((t, ATTN_WIDTH), BF16),
            pltpu.VMEM((t + WINDOW, 2 * KV_WIDTH), BF16),
            pltpu.VMEM((t + WINDOW, 2 * KV_WIDTH), BF16),
            pltpu.VMEM((t + POOL_HALO, POOL_WIDTH), F32),
            pltpu.VMEM((t, ATTN_WIDTH + POOL_WIDTH), BF16),
            pltpu.VMEM((2, t, d), F32),
            pltpu.VMEM((t, d), BF16),
            pltpu.VMEM((t, d_ff), BF16),
        ],
    )
    return pl.pallas_call(
        functools.partial(_layer_kernel, tiles_per_seq=tiles_per_seq),
        grid_spec=grid_spec,
        out_shape=jax.ShapeDtypeStruct((b, s, d), F32),
        compiler_params=pltpu.CompilerParams(
            dimension_semantics=("arbitrary",), vmem_limit_bytes=VMEM_LIMIT_BYTES),
        name="layer",
    )(sinks, x, pos_rep, freq, mod, norm1.reshape(1, d), norm2.reshape(1, d), norm_f.reshape(1, d),
      pool_scale.reshape(1, -1), w_in, w_pool.reshape(POOL_WIDTH, POOL_GROUP_WIDTH), w_out,
      w_gate, w_up, w_down)


def kernel(x, c, positions, w_ada, b_ada, norm1, w_in, sinks, w_pool, pool_scale,
           w_out, norm2, w_gate, w_up, w_down, norm_f):
    mod = _adaln_mod(c, w_ada, b_ada)
    return _layer(x, mod, positions, sinks, norm1, norm2, norm_f, w_in, w_pool, pool_scale,
                  w_out, w_gate, w_up, w_down)
```

```python
import functools
import math

import jax
import jax.numpy as jnp
from jax import lax
from jax.experimental import pallas as pl
from jax.experimental.pallas import tpu as pltpu

F32 = jnp.float32
BF16 = jnp.bfloat16

D_MODEL = 1024
HEAD_DIM = 64
N_HEADS = 8
N_KV_HEADS = 2
GROUP = N_HEADS // N_KV_HEADS
ATTN_WIDTH = N_HEADS * HEAD_DIM
KV_WIDTH = N_KV_HEADS * HEAD_DIM
POOL_WINDOWS = (2, 4, 8, 16)
POOL_GROUP_WIDTH = 128
POOL_WIDTH = POOL_GROUP_WIDTH * len(POOL_WINDOWS)
IN_PROJ_WIDTH = ATTN_WIDTH + 2 * KV_WIDTH + POOL_WIDTH
WINDOW = 128
ROPE_THETA = 10000.0
N_MOD = 6
RMS_EPS = 1e-6
HALF = HEAD_DIM // 2

LANES = 128
POOL_HALO = 16
VMEM_LIMIT_BYTES = 56 * 1024 * 1024

TOKEN_TILE = 512
FF_CHUNK = 256
DOWN_CHUNK = 256
ATTN_SKEW = 3
CAST_CHUNK_ROWS = 128
CAST_SLOTS = 4
CAST_ROWS_PER_ITER = 16
OUT_PROJ_LAG = 2
BACK_SPAN = 0.9


def _const_spec(shape):
    zeros = (0,) * len(shape)
    return pl.BlockSpec(shape, lambda *_: zeros, pipeline_mode=pl.Buffered(1))


def _rms_scale(x):
    return lax.rsqrt(jnp.mean(x * x, axis=-1, keepdims=True) + RMS_EPS)


def _interleave(front, back):
    keyed = [((k + 0.5) / len(front), 0, k, f) for k, f in enumerate(front)]
    keyed += [(BACK_SPAN * (k + 0.5) / len(back), 1, k, f) for k, f in enumerate(back)]
    return [f for *_, f in sorted(keyed, key=lambda e: e[:3])]


def _mod_kernel(ct_ref, w_ref, b_ref, o_ref):
    ct = ct_ref[...]
    sc = ct * jax.nn.sigmoid(ct)
    w = w_ref[...]
    for r in range(ct.shape[1]):
        o_ref[0, r:r + 1, :] = jnp.sum(w * sc[:, r:r + 1], axis=0, keepdims=True) + b_ref[...]


def _adaln_mod(c, w_ada, b_ada):
    b, d = c.shape
    return pl.pallas_call(
        _mod_kernel,
        grid=(N_MOD,),
        in_specs=[
            pl.BlockSpec((d, b), lambda j: (0, 0)),
            pl.BlockSpec((d, d), lambda j: (0, j)),
            pl.BlockSpec((1, d), lambda j: (0, j)),
        ],
        out_specs=pl.BlockSpec((1, b, d), lambda j: (j, 0, 0)),
        out_shape=jax.ShapeDtypeStruct((N_MOD, b, d), F32),
        compiler_params=pltpu.CompilerParams(dimension_semantics=("arbitrary",)),
        name="adaln_mod",
    )(c.T, w_ada, b_ada.reshape(1, -1))


def _weight_cast_tasks(pairs, stage_ref, sems):
    chunks = [(src, dst, c) for src, dst in pairs
              for c in range(src.shape[0] // CAST_CHUNK_ROWS)]
    slots = stage_ref.shape[0]

    def copy(k):
        src, _, c = chunks[k]
        return pltpu.make_async_copy(
            src.at[pl.ds(c * CAST_CHUNK_ROWS, CAST_CHUNK_ROWS), :],
            stage_ref.at[k % slots, :, pl.ds(0, src.shape[1])],
            sems.at[k % slots])

    def prime():
        for k in range(min(slots - 1, len(chunks))):
            copy(k).start()

    def cast(k):
        def task():
            src, dst, c = chunks[k]
            if k + slots - 1 < len(chunks):
                copy(k + slots - 1).start()
            copy(k).wait()

            def cast_rows(r, carry):
                off = pl.multiple_of(r * CAST_ROWS_PER_ITER, CAST_ROWS_PER_ITER)
                dst[pl.ds(c * CAST_CHUNK_ROWS + off, CAST_ROWS_PER_ITER), :] = (
                    stage_ref[k % slots, pl.ds(off, CAST_ROWS_PER_ITER), 0:src.shape[1]]
                    .astype(BF16))
                return carry

            lax.fori_loop(0, CAST_CHUNK_ROWS // CAST_ROWS_PER_ITER, cast_rows, 0)
        return task

    return prime, [cast(k) for k in range(len(chunks))]


def _layer_kernel(sinks_ref, x_ref, pos_ref, freq_ref, mod_ref, n1_ref, n2_ref, nf_ref,
                  pscale_ref, win_hbm, wpool_hbm, wout_hbm, wg_hbm, wu_hbm, wd_hbm,
                  o_ref,
                  win_ref, wpool_ref, wout_ref, wg_ref, wu_ref, wd_ref, stage_ref, cast_sems,
                  cos_ref, sin_ref, h_ref, q_ref, k_ref, v_ref, ext_ref, mix_ref, x1_ref, h2_ref,
                  act_ref, *, tiles_per_seq):
    t = x_ref.shape[1]
    s = pl.program_id(0)
    n_tiles = pl.num_programs(0) - 1
    tile_f = jnp.minimum(s, n_tiles - 1)
    first = (tile_f % tiles_per_seq) == 0
    slot_f = s % 2
    slot_b = 1 - slot_f
    seq_f = tile_f // tiles_per_seq
    seq_b = jnp.maximum(s - 1, 0) // tiles_per_seq

    def mod_row(k, seq):
        return mod_ref[k, pl.ds(seq, 1), :]

    lane =lax.broadcasted_iota(jnp.int32, (1, LANES), 1)
    low_lanes = lane < HEAD_DIM
    first_half = (lane % HEAD_DIM) < HALF

    def f_norm():
        x = x_ref[0]
        scale = n1_ref[...] * (1.0 + mod_row(1, seq_f))
        h_ref[...] = (x * _rms_scale(x) * scale + mod_row(0, seq_f)).astype(BF16)

    def f_trig():
        ang = pos_ref[...].astype(F32) * freq_ref[...]
        cos, sin = jnp.cos(ang), jnp.sin(ang)
        per_row = LANES // HALF
        for m in range(per_row):
            c32 = cos[:, m * HALF:(m + 1) * HALF]
            s32 = sin[:, m * HALF:(m + 1) * HALF]
            rows = pl.ds(m, t // per_row, stride=per_row)
            cos_ref[rows, :] = jnp.concatenate([c32, c32, c32, c32], axis=-1)
            sin_ref[rows, :] = jnp.concatenate([-s32, s32, -s32, s32], axis=-1)

    def rope(tile):
        partner = jnp.where(first_half,
                            pltpu.roll(tile, LANES - HALF, 1),
                            pltpu.roll(tile, HALF, 1))
        return tile * cos_ref[...] + partner * sin_ref[...]

    def f_q():
        u = jnp.dot(h_ref[...], win_ref[:, :ATTN_WIDTH], preferred_element_type=F32)
        q_scale = 1.0 / math.sqrt(HEAD_DIM)
        for j in range(ATTN_WIDTH // LANES):
            cols = slice(j * LANES, (j + 1) * LANES)
            q_ref[:, cols] = (rope(u[:, cols]) * q_scale).astype(BF16)

    def f_kv():
        k_ref[0:WINDOW, :] = k_ref[t:, :]
        v_ref[0:WINDOW, :] = v_ref[t:, :]
        u = jnp.dot(h_ref[...], win_ref[:, ATTN_WIDTH:ATTN_WIDTH + 2 * KV_WIDTH],
                    preferred_element_type=F32)
        kr = rope(u[:, :KV_WIDTH])
        vv = u[:, KV_WIDTH:]
        for g in range(N_KV_HEADS):
            kg = kr[:, g * HEAD_DIM:(g + 1) * HEAD_DIM]
            vg = vv[:, g * HEAD_DIM:(g + 1) * HEAD_DIM]
            cols = slice(g * LANES, (g + 1) * LANES)
            k_ref[WINDOW:, cols] = jnp.concatenate([kg, kg], axis=-1).astype(BF16)
            v_ref[WINDOW:, cols] = jnp.concatenate([vg, vg], axis=-1).astype(BF16)

    def f_up():
        halo = ext_ref[t:, :]
        ext_ref[0:POOL_HALO, :] = jnp.where(first, jnp.zeros_like(halo), halo)
        ext_ref[POOL_HALO:, :] = jnp.dot(h_ref[...], win_ref[:, ATTN_WIDTH + 2 * KV_WIDTH:],
                                         preferred_element_type=F32)

    def f_pool(gi, w):
        def task():
            cols = slice(gi * POOL_GROUP_WIDTH, (gi + 1) * POOL_GROUP_WIDTH)
            pos_in_seq = ((tile_f % tiles_per_seq) * t
                          + lax.broadcasted_iota(jnp.int32, (t, 1), 0))
            tok = ext_ref[POOL_HALO:, cols]
            total = tok
            for back in range(1, w):
                total = total + ext_ref[POOL_HALO - back:POOL_HALO - back + t, cols]
            count = jnp.minimum(pos_in_seq + 1, w).astype(F32)
            pooled = (total / count - tok).astype(BF16)
            po = jnp.dot(pooled, wpool_ref[cols, :], preferred_element_type=F32) * pscale_ref[:, cols]
            mix_ref[:, ATTN_WIDTH + gi * POOL_GROUP_WIDTH:
                    ATTN_WIDTH + (gi + 1) * POOL_GROUP_WIDTH] = po.astype(BF16)
        return task

    n_blocks = t // WINDOW
    chunks = N_HEADS // 2
    bodies = [(j, chunk) for j in range(n_blocks) for chunk in range(chunks)]
    kv_cache, logits, probs = {}, {}, {}

    def band_mask(j):
        qi = lax.broadcasted_iota(jnp.int32, (WINDOW, 2 * WINDOW), 0)
        kj = lax.broadcasted_iota(jnp.int32, (WINDOW, 2 * WINDOW), 1)
        rel = kj - WINDOW - qi
        band = (rel <= 0) & (rel > -WINDOW)
        if j == 0:
            band = band & (kj >= jnp.where(first, WINDOW, 0))
        return band

    def block_diag_kv(j, g):
        if (j, g) not in kv_cache:
            rows = slice(j * WINDOW, (j + 2) * WINDOW)
            cols = slice(g * LANES, (g + 1) * LANES)
            kd = k_ref[rows, cols]
            vd = v_ref[rows, cols]
            zero = jnp.zeros_like(kd)
            k_bd = jnp.concatenate([jnp.where(low_lanes, kd, zero),
                                    jnp.where(low_lanes, zero, kd)], axis=0)
            v_bd = jnp.concatenate([jnp.where(low_lanes, vd, zero),
                                    jnp.where(low_lanes, zero, vd)], axis=0)
            kv_cache[(j, g)] = (k_bd, v_bd)
        return kv_cache[(j, g)]

    def scores(j, chunk):
        k_bd, _ = block_diag_kv(j, chunk // (GROUP // 2))
        qc = q_ref[j * WINDOW:(j + 1) * WINDOW, chunk * LANES:(chunk + 1) * LANES]
        return lax.dot_general(qc, k_bd, (((1,), (1,)), ((), ())),
                               preferred_element_type=F32)

    def softmax(j, chunk, lg):
        mask = band_mask(j)
        ps, inv_den = [], []
        for hh in range(2):
            sink = sinks_ref[2 * chunk + hh]
            l = jnp.where(mask, lg[:, hh * 2 * WINDOW:(hh + 1) * 2 * WINDOW], -jnp.inf)
            m = jnp.maximum(jnp.max(l, axis=-1, keepdims=True), sink)
            p = jnp.exp(l - m)
            den = jnp.sum(p, axis=-1, keepdims=True) + jnp.exp(sink - m)
            ps.append(p.astype(BF16))
            inv_den.append(1.0 / den)
        return jnp.concatenate(ps, axis=-1), jnp.where(low_lanes, inv_den[0], inv_den[1])

    def values(j, chunk, p, inv_den):
        _, v_bd = block_diag_kv(j, chunk // (GROUP // 2))
        pv = jnp.dot(p, v_bd, preferred_element_type=F32)
        mix_ref[j * WINDOW:(j + 1) * WINDOW, chunk * LANES:(chunk + 1) * LANES] = (
            (pv * inv_den).astype(BF16))

    def out_proj(j):
        rows = slice(j * WINDOW, (j + 1) * WINDOW)
        mixed = jnp.dot(mix_ref[rows, :], wout_ref[...], preferred_element_type=F32)
        x1_ref[slot_f, rows, :] = x_ref[0, rows, :] + mod_row(2, seq_f) * mixed

    def f_attn(step):
        def task():
            n = len(bodies)
            if step < n:
                logits[step] = scores(*bodies[step])
            if 0 <= step - 1 < n:
                probs[step - 1] = softmax(*bodies[step - 1], logits.pop(step - 1))
            done = step - ATTN_SKEW
            if 0 <= done < n:
                values(*bodies[done], *probs.pop(done))
            ready = step - ATTN_SKEW - OUT_PROJ_LAG
            if 0 <= ready < n and bodies[ready][1] == chunks - 1:
                out_proj(bodies[ready][0])
        return task

    front = [f_norm, f_trig, f_q, f_kv, f_up]
    front += [f_pool(gi, w) for gi, w in enumerate(POOL_WINDOWS)]
    front += [f_attn(step) for step in range(len(bodies) + ATTN_SKEW + OUT_PROJ_LAG)]

    def b_norm():
        x1 = x1_ref[slot_b]
        scale = n2_ref[...] * (1.0 + mod_row(4, seq_b))
        h2_ref[...] = (x1 * _rms_scale(x1) * scale + mod_row(3, seq_b)).astype(BF16)

    def b_gate_up(n):
        def task():
            cols = slice(n * FF_CHUNK, (n + 1) * FF_CHUNK)
            g = jnp.dot(h2_ref[...], wg_ref[:, cols], preferred_element_type=F32)
            u = jnp.dot(h2_ref[...], wu_ref[:, cols], preferred_element_type=F32)
            act_ref[:, cols] = (g * jax.nn.sigmoid(g) * u).astype(BF16)
        return task

    def b_down(n):
        def task():
            cols = slice(n * DOWN_CHUNK, (n + 1) * DOWN_CHUNK)
            ff = jnp.dot(act_ref[...], wd_ref[:, cols], preferred_element_type=F32)
            o_ref[0, :, cols] = x1_ref[slot_b, :, cols] + mod_row(5, seq_b)[:, cols] * ff
        return task

    def b_final():
        x2 = o_ref[0]
        o_ref[0] = x2 * _rms_scale(x2) * nf_ref[...]

    d_ff = wg_ref.shape[1]
    back = [b_norm]
    back += [b_gate_up(n) for n in range(d_ff // FF_CHUNK)]
    back += [b_down(n) for n in range(D_MODEL // DOWN_CHUNK)]
    back += [b_final]

    def run(tasks):
        kv_cache.clear(), logits.clear(), probs.clear()
        for task in tasks:
            task()

    @pl.when(s == 0)
    def _():
        k_ref[t:, :] = jnp.zeros((WINDOW, 2 * KV_WIDTH), BF16)
        v_ref[t:, :] = jnp.zeros((WINDOW, 2 * KV_WIDTH), BF16)
        ext_ref[t:, :] = jnp.zeros((POOL_HALO, POOL_WIDTH), F32)
        prime, casts = _weight_cast_tasks(
            [(win_hbm, win_ref), (wpool_hbm, wpool_ref), (wout_hbm, wout_ref)],
            stage_ref, cast_sems)
        prime()
        run(casts)
        prime, casts = _weight_cast_tasks(
            [(wg_hbm, wg_ref), (wu_hbm, wu_ref), (wd_hbm, wd_ref)], stage_ref, cast_sems)
        prime()
        run(_interleave(front, casts))

    @pl.when(s > 0)
    def _():
        run(_interleave(front, back))


def _layer(x, mod, positions, sinks, norm1, norm2, norm_f, w_in, w_pool, pool_scale, w_out,
           w_gate, w_up, w_down):
    b, s, d = x.shape
    t = TOKEN_TILE
    d_ff = w_gate.shape[1]
    assert d == D_MODEL and s % t == 0 and t % WINDOW == 0 and d_ff % FF_CHUNK == 0
    assert d_ff % CAST_CHUNK_ROWS == 0 and d % CAST_CHUNK_ROWS == 0
    tiles_per_seq = s // t
    n_tiles = b * tiles_per_seq

    def front_tile(step):
        return jnp.minimum(step, n_tiles - 1)

    def back_tile(step):
        return jnp.maximum(step - 1, 0)

    def tok_map(tile_of):
        return lambda step, *_: (tile_of(step) // tiles_per_seq, tile_of(step) % tiles_per_seq, 0)

    per_row = LANES // HALF
    inv_freq = ROPE_THETA ** (-jnp.arange(HALF, dtype=F32) * (2.0 / HEAD_DIM))
    freq = jnp.tile(inv_freq, per_row).reshape(1, LANES)
    pos_rep = jnp.repeat(positions.reshape(b * s // per_row, per_row), HALF, axis=1)
    grid_spec = pltpu.PrefetchScalarGridSpec(
        num_scalar_prefetch=1,
        grid=(n_tiles + 1,),
        in_specs=[
            pl.BlockSpec((1, t, d), tok_map(front_tile)),
            pl.BlockSpec((t // per_row, LANES), lambda step, *_: (front_tile(step), 0)),
            _const_spec((1, LANES)),
            _const_spec((N_MOD, b, d)),
            _const_spec((1, d)), _const_spec((1, d)), _const_spec((1, d)),
            _const_spec((1, POOL_WIDTH)),
        ] + [pl.BlockSpec(memory_space=pl.ANY)] * 6,
        out_specs=pl.BlockSpec((1, t, d), tok_map(back_tile)),
        scratch_shapes=[
            pltpu.VMEM((d, IN_PROJ_WIDTH), BF16),
            pltpu.VMEM((POOL_WIDTH, POOL_GROUP_WIDTH), BF16),
            pltpu.VMEM((ATTN_WIDTH + POOL_WIDTH, d), BF16),
            pltpu.VMEM((d, d_ff), BF16),
            pltpu.VMEM((d, d_ff), BF16),
            pltpu.VMEM((d_ff, d), BF16),
            pltpu.VMEM((CAST_SLOTS, CAST_CHUNK_ROWS, max(d_ff, IN_PROJ_WIDTH)), F32),
            pltpu.SemaphoreType.DMA((CAST_SLOTS,)),
            pltpu.VMEM((t, LANES), F32),
            pltpu.VMEM((t, LANES), F32),
            pltpu.VMEM((t, d), BF16),
            pltpu.VMEM((t, ATTN_WIDTH), BF16),
            pltpu.VMEM((t + WINDOW, 2 * KV_WIDTH), BF16),
            pltpu.VMEM((t + WINDOW, 2 * KV_WIDTH), BF16),
            pltpu.VMEM((t + POOL_HALO, POOL_WIDTH), F32),
            pltpu.VMEM((t, ATTN_WIDTH + POOL_WIDTH), BF16),
            pltpu.VMEM((2, t, d), F32),
            pltpu.VMEM((t, d), BF16),
            pltpu.VMEM((t, d_ff), BF16),
        ],
    )
    return pl.pallas_call(
        functools.partial(_layer_kernel, tiles_per_seq=tiles_per_seq),
        grid_spec=grid_spec,
        out_shape=jax.ShapeDtypeStruct((b, s, d), F32),
        compiler_params=pltpu.CompilerParams(
            dimension_semantics=("arbitrary",), vmem_limit_bytes=VMEM_LIMIT_BYTES),
        name="layer",
    )(sinks, x, pos_rep, freq, mod, norm1.reshape(1, d), norm2.reshape(1, d), norm_f.reshape(1, d),
      pool_scale.reshape(1, -1), w_in, w_pool.reshape(POOL_WIDTH, POOL_GROUP_WIDTH), w_out,
      w_gate, w_up, w_down)


def kernel(x, c, positions, w_ada, b_ada, norm1, w_in, sinks, w_pool, pool_scale,
           w_out, norm2, w_gate, w_up, w_down, norm_f):
    mod = _adaln_mod(c, w_ada, b_ada)
    return _layer(x, mod, positions, sinks, norm1, norm2, norm_f, w_in, w_pool, pool_scale,
                  w_out, w_gate, w_up, w_down)
```

```python
import functools
import math

import jax
import jax.numpy as jnp
from jax import lax
from jax.experimental import pallas as pl
from jax.experimental.pallas import tpu as pltpu

F32 = jnp.float32
BF16 = jnp.bfloat16

D_MODEL = 1024
HEAD_DIM = 64
N_HEADS = 8
N_KV_HEADS = 2
GROUP = N_HEADS // N_KV_HEADS
ATTN_WIDTH = N_HEADS * HEAD_DIM
KV_WIDTH = N_KV_HEADS * HEAD_DIM
POOL_WINDOWS = (2, 4, 8, 16)
POOL_GROUP_WIDTH = 128
POOL_WIDTH = POOL_GROUP_WIDTH * len(POOL_WINDOWS)
IN_PROJ_WIDTH = ATTN_WIDTH + 2 * KV_WIDTH + POOL_WIDTH
WINDOW = 128
ROPE_THETA = 10000.0
N_MOD = 6
RMS_EPS = 1e-6
HALF = HEAD_DIM // 2

LANES = 128
POOL_HALO = 16
VMEM_LIMIT_BYTES = 56 * 1024 * 1024

TOKEN_TILE = 512
FF_CHUNK = 256
DOWN_CHUNK = 256
ATTN_SKEW = 3
CAST_CHUNK_ROWS = 128
CAST_SLOTS = 4
CAST_ROWS_PER_ITER = 16
OUT_PROJ_LAG = 2
BACK_SPAN = 0.9


def _const_spec(shape):
    zeros = (0,) * len(shape)
    return pl.BlockSpec(shape, lambda *_: zeros, pipeline_mode=pl.Buffered(1))


def _rms_scale(x):
    return lax.rsqrt(jnp.mean(x * x, axis=-1, keepdims=True) + RMS_EPS)


def _interleave(front, back):
    keyed = [((k + 0.5) / len(front), 0, k, f) for k, f in enumerate(front)]
    keyed += [(BACK_SPAN * (k + 0.5) / len(back), 1, k, f) for k, f in enumerate(back)]
    return [f for *_, f in sorted(keyed, key=lambda e: e[:3])]


def _mod_kernel(ct_ref, w_ref, b_ref, o_ref):
    ct = ct_ref[...]
    sc = ct * jax.nn.sigmoid(ct)
    w = w_ref[...]
    for r in range(ct.shape[1]):
        o_ref[0, r:r + 1, :] = jnp.sum(w * sc[:, r:r + 1], axis=0, keepdims=True) + b_ref[...]


def _adaln_mod(c, w_ada, b_ada):
    b, d = c.shape
    return pl.pallas_call(
        _mod_kernel,
        grid=(N_MOD,),
        in_specs=[
            pl.BlockSpec((d, b), lambda j: (0, 0)),
            pl.BlockSpec((d, d), lambda j: (0, j)),
            pl.BlockSpec((1, d), lambda j: (0, j)),
        ],
        out_specs=pl.BlockSpec((1, b, d), lambda j: (j, 0, 0)),
        out_shape=jax.ShapeDtypeStruct((N_MOD, b, d), F32),
        compiler_params=pltpu.CompilerParams(dimension_semantics=("arbitrary",)),
        name="adaln_mod",
    )(c.T, w_ada, b_ada.reshape(1, -1))


def _weight_cast_tasks(pairs, stage_ref, sems):
    chunks = [(src, dst, c) for src, dst in pairs
              for c in range(src.shape[0] // CAST_CHUNK_ROWS)]
    slots = stage_ref.shape[0]

    def copy(k):
        src, _, c = chunks[k]
        return pltpu.make_async_copy(
            src.at[pl.ds(c * CAST_CHUNK_ROWS, CAST_CHUNK_ROWS), :],
            stage_ref.at[k % slots, :, pl.ds(0, src.shape[1])],
            sems.at[k % slots])

    def prime():
        for k in range(min(slots - 1, len(chunks))):
            copy(k).start()

    def cast(k):
        def task():
            src, dst, c = chunks[k]
            if k + slots - 1 < len(chunks):
                copy(k + slots - 1).start()
            copy(k).wait()

            def cast_rows(r, carry):
                off = pl.multiple_of(r * CAST_ROWS_PER_ITER, CAST_ROWS_PER_ITER)
                dst[pl.ds(c * CAST_CHUNK_ROWS + off, CAST_ROWS_PER_ITER), :] = (
                    stage_ref[k % slots, pl.ds(off, CAST_ROWS_PER_ITER), 0:src.shape[1]]
                    .astype(BF16))
                return carry

            lax.fori_loop(0, CAST_CHUNK_ROWS // CAST_ROWS_PER_ITER, cast_rows, 0)
        return task

    return prime, [cast(k) for k in range(len(chunks))]


def _layer_kernel(sinks_ref, x_ref, pos_ref, freq_ref, mod_ref, n1_ref, n2_ref, nf_ref,
                  pscale_ref, win_hbm, wpool_hbm, wout_hbm, wg_hbm, wu_hbm, wd_hbm,
                  o_ref,
                  win_ref, wpool_ref, wout_ref, wg_ref, wu_ref, wd_ref, stage_ref, cast_sems,
                  cos_ref, sin_ref, h_ref, q_ref, kt_ref, v_ref, ext_ref, mix_ref, x1_ref, h2_ref,
                  act_ref, *, tiles_per_seq):
    t = x_ref.shape[1]
    s = pl.program_id(0)
    n_tiles = pl.num_programs(0) - 1
    tile_f = jnp.minimum(s, n_tiles - 1)
    first = (tile_f % tiles_per_seq) == 0
    slot_f = s % 2
    slot_b = 1 - slot_f
    seq_f = tile_f // tiles_per_seq
    seq_b = jnp.maximum(s - 1, 0) // tiles_per_seq

    def mod_row(k, seq):
        return mod_ref[k, pl.ds(seq, 1), :]

    lane =lax.broadcasted_iota(jnp.int32, (1, LANES), 1)
    low_lanes = lane < HEAD_DIM
    first_half = (lane % HEAD_DIM) < HALF

    def f_norm():
        x = x_ref[0]
        scale = n1_ref[...] * (1.0 + mod_row(1, seq_f))
        h_ref[...] = (x * _rms_scale(x) * scale + mod_row(0, seq_f)).astype(BF16)

    def f_trig():
        ang = pos_ref[...].astype(F32) * freq_ref[...]
        cos, sin = jnp.cos(ang), jnp.sin(ang)
        per_row = LANES // HALF
        for m in range(per_row):
            c32 = cos[:, m * HALF:(m + 1) * HALF]
            s32 = sin[:, m * HALF:(m + 1) * HALF]
            rows = pl.ds(m, t // per_row, stride=per_row)
            cos_ref[rows, :] = jnp.concatenate([c32, c32, c32, c32], axis=-1)
            sin_ref[rows, :] = jnp.concatenate([-s32, s32, -s32, s32], axis=-1)

    def rope(tile):
        partner = jnp.where(first_half,
                            pltpu.roll(tile, LANES - HALF, 1),
                            pltpu.roll(tile, HALF, 1))
        return tile * cos_ref[...] + partner * sin_ref[...]

    def f_q():
        u = jnp.dot(h_ref[...], win_ref[:, :ATTN_WIDTH], preferred_element_type=F32)
        q_scale = 1.0 / math.sqrt(HEAD_DIM)
        for j in range(ATTN_WIDTH // LANES):
            cols = slice(j * LANES, (j + 1) * LANES)
            q_ref[:, cols] = (rope(u[:, cols]) * q_scale).astype(BF16)

    def f_kv():
        kt_ref[:, 0:WINDOW] = kt_ref[:, t:]
        v_ref[0:WINDOW, :] = v_ref[t:, :]
        u = jnp.dot(h_ref[...], win_ref[:, ATTN_WIDTH:ATTN_WIDTH + 2 * KV_WIDTH],
                    preferred_element_type=F32)
        kt_ref[:, WINDOW:] = rope(u[:, :KV_WIDTH]).T.astype(BF16)
        vv = u[:, KV_WIDTH:]
        for g in range(N_KV_HEADS):
            vg = vv[:, g * HEAD_DIM:(g + 1) * HEAD_DIM]
            cols = slice(g * LANES, (g + 1) * LANES)
            v_ref[WINDOW:, cols] = jnp.concatenate([vg, vg], axis=-1).astype(BF16)

    def f_up():
        halo = ext_ref[t:, :]
        ext_ref[0:POOL_HALO, :] = jnp.where(first, jnp.zeros_like(halo), halo)
        ext_ref[POOL_HALO:, :] = jnp.dot(h_ref[...], win_ref[:, ATTN_WIDTH + 2 * KV_WIDTH:],
                                         preferred_element_type=F32)

    def f_pool(gi, w):
        def task():
            cols = slice(gi * POOL_GROUP_WIDTH, (gi + 1) * POOL_GROUP_WIDTH)
            pos_in_seq = ((tile_f % tiles_per_seq) * t
                          + lax.broadcasted_iota(jnp.int32, (t, 1), 0))
            tok = ext_ref[POOL_HALO:, cols]
            total = tok
            for back in range(1, w):
                total = total + ext_ref[POOL_HALO - back:POOL_HALO - back + t, cols]
            count = jnp.minimum(pos_in_seq + 1, w).astype(F32)
            pooled = (total / count - tok).astype(BF16)
            po = jnp.dot(pooled, wpool_ref[cols, :], preferred_element_type=F32) * pscale_ref[:, cols]
            mix_ref[:, ATTN_WIDTH + gi * POOL_GROUP_WIDTH:
                    ATTN_WIDTH + (gi + 1) * POOL_GROUP_WIDTH] = po.astype(BF16)
        return task

    n_blocks = t // WINDOW
    chunks = N_HEADS // 2
    bodies = [(j, chunk) for j in range(n_blocks) for chunk in range(chunks)]
    kv_cache, logits, probs = {}, {}, {}

    def band_mask(j):
        qi = lax.broadcasted_iota(jnp.int32, (WINDOW, 2 * WINDOW), 0)
        kj = lax.broadcasted_iota(jnp.int32, (WINDOW, 2 * WINDOW), 1)
        rel = kj - WINDOW - qi
        band = (rel <= 0) & (rel > -WINDOW)
        if j == 0:
            band = band & (kj >= jnp.where(first, WINDOW, 0))
        return band

    def block_diag_kv(j, g):
        if (j, g) not in kv_cache:
            keys = slice(j * WINDOW, (j + 2) * WINDOW)
            ktg = kt_ref[g * HEAD_DIM:(g + 1) * HEAD_DIM, keys]
            zk = jnp.zeros_like(ktg)
            k_bd = jnp.concatenate([jnp.concatenate([ktg, zk], axis=1),
                                    jnp.concatenate([zk, ktg], axis=1)], axis=0)
            vd = v_ref[keys, g * LANES:(g + 1) * LANES]
            zv = jnp.zeros_like(vd)
            v_bd = jnp.concatenate([jnp.where(low_lanes, vd, zv),
                                    jnp.where(low_lanes, zv, vd)], axis=0)
            kv_cache[(j, g)] = (k_bd, v_bd)
        return kv_cache[(j, g)]

    def scores(j, chunk):
        k_bd, _ = block_diag_kv(j, chunk // (GROUP // 2))
        qc = q_ref[j * WINDOW:(j + 1) * WINDOW, chunk * LANES:(chunk + 1) * LANES]
        return jnp.dot(qc, k_bd, preferred_element_type=F32)

    def softmax(j, chunk, lg):
        mask = band_mask(j)
        ps, inv_den = [], []
        for hh in range(2):
            sink = sinks_ref[2 * chunk + hh]
            l = jnp.where(mask, lg[:, hh * 2 * WINDOW:(hh + 1) * 2 * WINDOW], -jnp.inf)
            m = jnp.maximum(jnp.max(l, axis=-1, keepdims=True), sink)
            p = jnp.exp(l - m)
            den = jnp.sum(p, axis=-1, keepdims=True) + jnp.exp(sink - m)
            ps.append(p.astype(BF16))
            inv_den.append(1.0 / den)
        return jnp.concatenate(ps, axis=-1), jnp.where(low_lanes, inv_den[0], inv_den[1])

    def values(j, chunk, p, inv_den):
        _, v_bd = block_diag_kv(j, chunk // (GROUP // 2))
        pv = jnp.dot(p, v_bd, preferred_element_type=F32)
        mix_ref[j * WINDOW:(j + 1) * WINDOW, chunk * LANES:(chunk + 1) * LANES] = (
            (pv * inv_den).astype(BF16))

    def out_proj(j):
        rows = slice(j * WINDOW, (j + 1) * WINDOW)
        mixed = jnp.dot(mix_ref[rows, :], wout_ref[...], preferred_element_type=F32)
        x1_ref[slot_f, rows, :] = x_ref[0, rows, :] + mod_row(2, seq_f) * mixed

    def f_attn(step):
        def task():
            n = len(bodies)
            if step < n:
                logits[step] = scores(*bodies[step])
            if 0 <= step - 1 < n:
                probs[step - 1] = softmax(*bodies[step - 1], logits.pop(step - 1))
            done = step - ATTN_SKEW
            if 0 <= done < n:
                values(*bodies[done], *probs.pop(done))
            ready = step - ATTN_SKEW - OUT_PROJ_LAG
            if 0 <= ready < n and bodies[ready][1] == chunks - 1:
                out_proj(bodies[ready][0])
        return task

    front = [f_norm, f_trig, f_q, f_kv, f_up]
    front += [f_pool(gi, w) for gi, w in enumerate(POOL_WINDOWS)]
    front += [f_attn(step) for step in range(len(bodies) + ATTN_SKEW + OUT_PROJ_LAG)]

    def b_norm():
        x1 = x1_ref[slot_b]
        scale = n2_ref[...] * (1.0 + mod_row(4, seq_b))
        h2_ref[...] = (x1 * _rms_scale(x1) * scale + mod_row(3, seq_b)).astype(BF16)

    def b_gate_up(n):
        def task():
            cols = slice(n * FF_CHUNK, (n + 1) * FF_CHUNK)
            g = jnp.dot(h2_ref[...], wg_ref[:, cols], preferred_element_type=F32)
            u = jnp.dot(h2_ref[...], wu_ref[:, cols], preferred_element_type=F32)
            act_ref[:, cols] = (g * jax.nn.sigmoid(g) * u).astype(BF16)
        return task

    def b_down(n):
        def task():
            cols = slice(n * DOWN_CHUNK, (n + 1) * DOWN_CHUNK)
            ff = jnp.dot(act_ref[...], wd_ref[:, cols], preferred_element_type=F32)
            o_ref[0, :, cols] = x1_ref[slot_b, :, cols] + mod_row(5, seq_b)[:, cols] * ff
        return task

    def b_final():
        x2 = o_ref[0]
        o_ref[0] = x2 * _rms_scale(x2) * nf_ref[...]

    d_ff = wg_ref.shape[1]
    back = [b_norm]
    back += [b_gate_up(n) for n in range(d_ff // FF_CHUNK)]
    back += [b_down(n) for n in range(D_MODEL // DOWN_CHUNK)]
    back += [b_final]

    def run(tasks):
        kv_cache.clear(), logits.clear(), probs.clear()
        for task in tasks:
            task()

    @pl.when(s == 0)
    def _():
        kt_ref[:, t:] = jnp.zeros((KV_WIDTH, WINDOW), BF16)
        v_ref[t:, :] = jnp.zeros((WINDOW, 2 * KV_WIDTH), BF16)
        ext_ref[t:, :] = jnp.zeros((POOL_HALO, POOL_WIDTH), F32)
        prime, casts = _weight_cast_tasks(
            [(win_hbm, win_ref), (wpool_hbm, wpool_ref), (wout_hbm, wout_ref)],
            stage_ref, cast_sems)
        prime()
        run(casts)
        prime, casts = _weight_cast_tasks(
            [(wg_hbm, wg_ref), (wu_hbm, wu_ref), (wd_hbm, wd_ref)], stage_ref, cast_sems)
        prime()
        run(_interleave(front, casts))

    @pl.when(s > 0)
    def _():
        run(_interleave(front, back))


def _layer(x, mod, positions, sinks, norm1, norm2, norm_f, w_in, w_pool, pool_scale, w_out,
           w_gate, w_up, w_down):
    b, s, d = x.shape
    t = TOKEN_TILE
    d_ff = w_gate.shape[1]
    assert d == D_MODEL and s % t == 0 and t % WINDOW == 0 and d_ff % FF_CHUNK == 0
    assert d_ff % CAST_CHUNK_ROWS == 0 and d % CAST_CHUNK_ROWS == 0
    tiles_per_seq = s // t
    n_tiles = b * tiles_per_seq

    def front_tile(step):
        return jnp.minimum(step, n_tiles - 1)

    def back_tile(step):
        return jnp.maximum(step - 1, 0)

    def tok_map(tile_of):
        return lambda step, *_: (tile_of(step) // tiles_per_seq, tile_of(step) % tiles_per_seq, 0)

    per_row = LANES // HALF
    inv_freq = ROPE_THETA ** (-jnp.arange(HALF, dtype=F32) * (2.0 / HEAD_DIM))
    freq = jnp.tile(inv_freq, per_row).reshape(1, LANES)
    pos_rep = jnp.repeat(positions.reshape(b * s // per_row, per_row), HALF, axis=1)
    grid_spec = pltpu.PrefetchScalarGridSpec(
        num_scalar_prefetch=1,
        grid=(n_tiles + 1,),
        in_specs=[
            pl.BlockSpec((1, t, d), tok_map(front_tile)),
            pl.BlockSpec((t // per_row, LANES), lambda step, *_: (front_tile(step), 0)),
            _const_spec((1, LANES)),
            _const_spec((N_MOD, b, d)),
            _const_spec((1, d)), _const_spec((1, d)), _const_spec((1, d)),
            _const_spec((1, POOL_WIDTH)),
        ] + [pl.BlockSpec(memory_space=pl.ANY)] * 6,
        out_specs=pl.BlockSpec((1, t, d), tok_map(back_tile)),
        scratch_shapes=[
            pltpu.VMEM((d, IN_PROJ_WIDTH), BF16),
            pltpu.VMEM((POOL_WIDTH, POOL_GROUP_WIDTH), BF16),
            pltpu.VMEM((ATTN_WIDTH + POOL_WIDTH, d), BF16),
            pltpu.VMEM((d, d_ff), BF16),
            pltpu.VMEM((d, d_ff), BF16),
            pltpu.VMEM((d_ff, d), BF16),
            pltpu.VMEM((CAST_SLOTS, CAST_CHUNK_ROWS, max(d_ff, IN_PROJ_WIDTH)), F32),
            pltpu.SemaphoreType.DMA((CAST_SLOTS,)),
            pltpu.VMEM((t, LANES), F32),
            pltpu.VMEM((t, LANES), F32),
            pltpu.VMEM((t, d), BF16),
            pltpu.VMEM((t, ATTN_WIDTH), BF16),
            pltpu.VMEM((KV_WIDTH, t + WINDOW), BF16),
            pltpu.VMEM((t + WINDOW, 2 * KV_WIDTH), BF16),
            pltpu.VMEM((t + POOL_HALO, POOL_WIDTH), F32),
            pltpu.VMEM((t, ATTN_WIDTH + POOL_WIDTH), BF16),
            pltpu.VMEM((2, t, d), F32),
            pltpu.VMEM((t, d), BF16),
            pltpu.VMEM((t, d_ff), BF16),
        ],
    )
    return pl.pallas_call(
        functools.partial(_layer_kernel, tiles_per_seq=tiles_per_seq),
        grid_spec=grid_spec,
        out_shape=jax.ShapeDtypeStruct((b, s, d), F32),
        compiler_params=pltpu.CompilerParams(
            dimension_semantics=("arbitrary",), vmem_limit_bytes=VMEM_LIMIT_BYTES),
        name="layer",
    )(sinks, x, pos_rep, freq, mod, norm1.reshape(1, d), norm2.reshape(1, d), norm_f.reshape(1, d),
      pool_scale.reshape(1, -1), w_in, w_pool.reshape(POOL_WIDTH, POOL_GROUP_WIDTH), w_out,
      w_gate, w_up, w_down)


def kernel(x, c, positions, w_ada, b_ada, norm1, w_in, sinks, w_pool, pool_scale,
           w_out, norm2, w_gate, w_up, w_down, norm_f):
    mod = _adaln_mod(c, w_ada, b_ada)
    return _layer(x, mod, positions, sinks, norm1, norm2, norm_f, w_in, w_pool, pool_scale,
                  w_out, w_gate, w_up, w_down)
```

```python
import functools
import math

import jax
import jax.numpy as jnp
from jax import lax
from jax.experimental import pallas as pl
from jax.experimental.pallas import tpu as pltpu

F32 = jnp.float32
BF16 = jnp.bfloat16

D_MODEL = 1024
HEAD_DIM = 64
N_HEADS = 8
N_KV_HEADS = 2
GROUP = N_HEADS // N_KV_HEADS
ATTN_WIDTH = N_HEADS * HEAD_DIM
KV_WIDTH = N_KV_HEADS * HEAD_DIM
POOL_WINDOWS = (2, 4, 8, 16)
POOL_GROUP_WIDTH = 128
POOL_WIDTH = POOL_GROUP_WIDTH * len(POOL_WINDOWS)
IN_PROJ_WIDTH = ATTN_WIDTH + 2 * KV_WIDTH + POOL_WIDTH
WINDOW = 128
ROPE_THETA = 10000.0
N_MOD = 6
RMS_EPS = 1e-6
HALF = HEAD_DIM // 2

LANES = 128
POOL_HALO = 16
VMEM_LIMIT_BYTES = 56 * 1024 * 1024

TOKEN_TILE = 512
FF_CHUNK = 256
DOWN_CHUNK = 256
ATTN_SKEW = 3
CAST_CHUNK_ROWS = 128
CAST_SLOTS = 4
CAST_ROWS_PER_ITER = 16
OUT_PROJ_LAG = 2
BACK_SPAN = 0.9


def _const_spec(shape):
    zeros = (0,) * len(shape)
    return pl.BlockSpec(shape, lambda *_: zeros, pipeline_mode=pl.Buffered(1))


def _rms_scale(x):
    return lax.rsqrt(jnp.mean(x * x, axis=-1, keepdims=True) + RMS_EPS)


def _interleave(front, back):
    keyed = [((k + 0.5) / len(front), 0, k, f) for k, f in enumerate(front)]
    keyed += [(BACK_SPAN * (k + 0.5) / len(back), 1, k, f) for k, f in enumerate(back)]
    return [f for *_, f in sorted(keyed, key=lambda e: e[:3])]


def _mod_kernel(ct_ref, w_ref, b_ref, o_ref):
    ct = ct_ref[...]
    sc = ct * jax.nn.sigmoid(ct)
    w = w_ref[...]
    for r in range(ct.shape[1]):
        o_ref[0, r:r + 1, :] = jnp.sum(w * sc[:, r:r + 1], axis=0, keepdims=True) + b_ref[...]


def _adaln_mod(c, w_ada, b_ada):
    b, d = c.shape
    return pl.pallas_call(
        _mod_kernel,
        grid=(N_MOD,),
        in_specs=[
            pl.BlockSpec((d, b), lambda j: (0, 0)),
            pl.BlockSpec((d, d), lambda j: (0, j)),
            pl.BlockSpec((1, d), lambda j: (0, j)),
        ],
        out_specs=pl.BlockSpec((1, b, d), lambda j: (j, 0, 0)),
        out_shape=jax.ShapeDtypeStruct((N_MOD, b, d), F32),
        compiler_params=pltpu.CompilerParams(dimension_semantics=("arbitrary",)),
        name="adaln_mod",
    )(c.T, w_ada, b_ada.reshape(1, -1))


def _dot_f32(a, b):
    ah = a.astype(BF16)
    al = (a - ah.astype(F32)).astype(BF16)
    bh = b.astype(BF16)
    bl = (b - bh.astype(F32)).astype(BF16)
    dot = functools.partial(jnp.dot, preferred_element_type=F32)
    return dot(ah, bh) + dot(ah, bl) + dot(al, bh)


def _weight_cast_tasks(triples, stage_ref, sems):
    chunks = [(src, dst, c, fold) for src, dst, fold in triples
              for c in range(src.shape[0] // CAST_CHUNK_ROWS)]
    slots = stage_ref.shape[0]

    def copy(k):
        src, _, c, _ = chunks[k]
        return pltpu.make_async_copy(
            src.at[pl.ds(c * CAST_CHUNK_ROWS, CAST_CHUNK_ROWS), :],
            stage_ref.at[k % slots, :, pl.ds(0, src.shape[1])],
            sems.at[k % slots])

    def prime():
        for k in range(min(slots - 1, len(chunks))):
            copy(k).start()

    def cast(k):
        def task():
            src, dst, c, fold = chunks[k]
            if k + slots - 1 < len(chunks):
                copy(k + slots - 1).start()
            copy(k).wait()
            folded = fold(c, stage_ref.at[k % slots, :, pl.ds(0, src.shape[1])]) if fold else None
            if folded is not None:
                dst[c * CAST_CHUNK_ROWS:(c + 1) * CAST_CHUNK_ROWS, :] = folded.astype(BF16)
                return

            def cast_rows(r, carry):
                off = pl.multiple_of(r * CAST_ROWS_PER_ITER, CAST_ROWS_PER_ITER)
                dst[pl.ds(c * CAST_CHUNK_ROWS + off, CAST_ROWS_PER_ITER), :] = (
                    stage_ref[k % slots, pl.ds(off, CAST_ROWS_PER_ITER), 0:src.shape[1]]
                    .astype(BF16))
                return carry

            lax.fori_loop(0, CAST_CHUNK_ROWS // CAST_ROWS_PER_ITER, cast_rows, 0)
        return task

    return prime, [cast(k) for k in range(len(chunks))]


def _layer_kernel(sinks_ref, x_ref, pos_ref, freq_ref, mod_ref, n1_ref, n2_ref, nf_ref,
                  pscale_ref, wpool_ref, win_hbm, wout_hbm, wg_hbm, wu_hbm, wd_hbm,
                  o_ref,
                  win_ref, wout_ref, wg_ref, wu_ref, wd_ref, stage_ref, cast_sems,
                  cos_ref, sin_ref, h_ref, q_ref, kt_ref, v_ref, ext_ref, mix_ref, x1_ref, h2_ref,
                  act_ref, *, tiles_per_seq):
    t = x_ref.shape[1]
    s = pl.program_id(0)
    n_tiles = pl.num_programs(0) - 1
    tile_f = jnp.minimum(s, n_tiles - 1)
    first = (tile_f % tiles_per_seq) == 0
    slot_f = s % 2
    slot_b = 1 - slot_f
    seq_f = tile_f // tiles_per_seq
    seq_b = jnp.maximum(s - 1, 0) // tiles_per_seq

    def mod_row(k, seq):
        return mod_ref[k, pl.ds(seq, 1), :]

    lane =lax.broadcasted_iota(jnp.int32, (1, LANES), 1)
    low_lanes = lane < HEAD_DIM
    first_half = (lane % HEAD_DIM) < HALF

    def f_norm():
        x = x_ref[0]
        scale = n1_ref[...] * (1.0 + mod_row(1, seq_f))
        h_ref[...] = (x * _rms_scale(x) * scale + mod_row(0, seq_f)).astype(BF16)

    def f_trig():
        ang = pos_ref[...].astype(F32) * freq_ref[...]
        cos, sin = jnp.cos(ang), jnp.sin(ang)
        per_row = LANES // HALF
        for m in range(per_row):
            c32 = cos[:, m * HALF:(m + 1) * HALF]
            s32 = sin[:, m * HALF:(m + 1) * HALF]
            rows = pl.ds(m, t // per_row, stride=per_row)
            cos_ref[rows, :] = jnp.concatenate([c32, c32, c32, c32], axis=-1)
            sin_ref[rows, :] = jnp.concatenate([-s32, s32, -s32, s32], axis=-1)

    def rope(tile):
        partner = jnp.where(first_half,
                            pltpu.roll(tile, LANES - HALF, 1),
                            pltpu.roll(tile, HALF, 1))
        return tile * cos_ref[...] + partner * sin_ref[...]

    def f_q():
        u = jnp.dot(h_ref[...], win_ref[:, :ATTN_WIDTH], preferred_element_type=F32)
        q_scale = 1.0 / math.sqrt(HEAD_DIM)
        for j in range(ATTN_WIDTH // LANES):
            cols = slice(j * LANES, (j + 1) * LANES)
            q_ref[:, cols] = (rope(u[:, cols]) * q_scale).astype(BF16)

    def f_kv():
        kt_ref[:, 0:WINDOW] = kt_ref[:, t:]
        v_ref[0:WINDOW, :] = v_ref[t:, :]
        u = jnp.dot(h_ref[...], win_ref[:, ATTN_WIDTH:ATTN_WIDTH + 2 * KV_WIDTH],
                    preferred_element_type=F32)
        kt_ref[:, WINDOW:] = rope(u[:, :KV_WIDTH]).T.astype(BF16)
        vv = u[:, KV_WIDTH:]
        for g in range(N_KV_HEADS):
            vg = vv[:, g * HEAD_DIM:(g + 1) * HEAD_DIM]
            cols = slice(g * LANES, (g + 1) * LANES)
            v_ref[WINDOW:, cols] = jnp.concatenate([vg, vg], axis=-1).astype(BF16)

    def f_up():
        halo = ext_ref[t:, :]
        ext_ref[0:POOL_HALO, :] = jnp.where(first, jnp.zeros_like(halo), halo)
        ext_ref[POOL_HALO:, :] = jnp.dot(h_ref[...], win_ref[:, ATTN_WIDTH + 2 * KV_WIDTH:],
                                         preferred_element_type=F32)

    def f_pool(gi, w):
        def task():
            cols = slice(gi * POOL_GROUP_WIDTH, (gi + 1) * POOL_GROUP_WIDTH)
            pos_in_seq = ((tile_f % tiles_per_seq) * t
                          + lax.broadcasted_iota(jnp.int32, (t, 1), 0))
            tok = ext_ref[POOL_HALO:, cols]
            total = tok
            for back in range(1, w):
                total = total + ext_ref[POOL_HALO - back:POOL_HALO - back + t, cols]
            count = jnp.minimum(pos_in_seq + 1, w).astype(F32)
            mix_ref[:, ATTN_WIDTH + gi * POOL_GROUP_WIDTH:
                    ATTN_WIDTH + (gi + 1) * POOL_GROUP_WIDTH] = (total / count - tok).astype(BF16)
        return task

    n_blocks = t // WINDOW
    chunks = N_HEADS // 2
    bodies = [(j, chunk) for j in range(n_blocks) for chunk in range(chunks)]
    kv_cache, logits, probs = {}, {}, {}

    def band_mask(j):
        qi = lax.broadcasted_iota(jnp.int32, (WINDOW, 2 * WINDOW), 0)
        kj = lax.broadcasted_iota(jnp.int32, (WINDOW, 2 * WINDOW), 1)
        rel = kj - WINDOW - qi
        band = (rel <= 0) & (rel > -WINDOW)
        if j == 0:
            band = band & (kj >= jnp.where(first, WINDOW, 0))
        return band

    def block_diag_kv(j, g):
        if (j, g) not in kv_cache:
            keys = slice(j * WINDOW, (j + 2) * WINDOW)
            ktg = kt_ref[g * HEAD_DIM:(g + 1) * HEAD_DIM, keys]
            zk = jnp.zeros_like(ktg)
            k_bd = jnp.concatenate([jnp.concatenate([ktg, zk], axis=1),
                                    jnp.concatenate([zk, ktg], axis=1)], axis=0)
            vd = v_ref[keys, g * LANES:(g + 1) * LANES]
            zv = jnp.zeros_like(vd)
            v_bd = jnp.concatenate([jnp.where(low_lanes, vd, zv),
                                    jnp.where(low_lanes, zv, vd)], axis=0)
            kv_cache[(j, g)] = (k_bd, v_bd)
        return kv_cache[(j, g)]

    def scores(j, chunk):
        k_bd, _ = block_diag_kv(j, chunk // (GROUP // 2))
        qc = q_ref[j * WINDOW:(j + 1) * WINDOW, chunk * LANES:(chunk + 1) * LANES]
        return jnp.dot(qc, k_bd, preferred_element_type=F32)

    def softmax(j, chunk, lg):
        mask = band_mask(j)
        ps, inv_den = [], []
        for hh in range(2):
            sink = sinks_ref[2 * chunk + hh]
            l = jnp.where(mask, lg[:, hh * 2 * WINDOW:(hh + 1) * 2 * WINDOW], -jnp.inf)
            m = jnp.maximum(jnp.max(l, axis=-1, keepdims=True), sink)
            p = jnp.exp(l - m)
            den = jnp.sum(p, axis=-1, keepdims=True) + jnp.exp(sink - m)
            ps.append(p.astype(BF16))
            inv_den.append(1.0 / den)
        return jnp.concatenate(ps, axis=-1), jnp.where(low_lanes, inv_den[0], inv_den[1])

    def values(j, chunk, p, inv_den):
        _, v_bd = block_diag_kv(j, chunk // (GROUP // 2))
        pv = jnp.dot(p, v_bd, preferred_element_type=F32)
        mix_ref[j * WINDOW:(j + 1) * WINDOW, chunk * LANES:(chunk + 1) * LANES] = (
            (pv * inv_den).astype(BF16))

    def out_proj(j):
        rows = slice(j * WINDOW, (j + 1) * WINDOW)
        mixed = jnp.dot(mix_ref[rows, :], wout_ref[...], preferred_element_type=F32)
        x1_ref[slot_f, rows, :] = x_ref[0, rows, :] + mod_row(2, seq_f) * mixed

    def f_attn(step):
        def task():
            n = len(bodies)
            if step < n:
                logits[step] = scores(*bodies[step])
            if 0 <= step - 1 < n:
                probs[step - 1] = softmax(*bodies[step - 1], logits.pop(step - 1))
            done = step - ATTN_SKEW
            if 0 <= done < n:
                values(*bodies[done], *probs.pop(done))
            ready = step - ATTN_SKEW - OUT_PROJ_LAG
            if 0 <= ready < n and bodies[ready][1] == chunks - 1:
                out_proj(bodies[ready][0])
        return task

    front = [f_norm, f_trig, f_q, f_kv, f_up]
    front += [f_pool(gi, w) for gi, w in enumerate(POOL_WINDOWS)]
    front += [f_attn(step) for step in range(len(bodies) + ATTN_SKEW + OUT_PROJ_LAG)]

    def b_norm():
        x1 = x1_ref[slot_b]
        scale = n2_ref[...] * (1.0 + mod_row(4, seq_b))
        h2_ref[...] = (x1 * _rms_scale(x1) * scale + mod_row(3, seq_b)).astype(BF16)

    def b_gate_up(n):
        def task():
            cols = slice(n * FF_CHUNK, (n + 1) * FF_CHUNK)
            g = jnp.dot(h2_ref[...], wg_ref[:, cols], preferred_element_type=F32)
            u = jnp.dot(h2_ref[...], wu_ref[:, cols], preferred_element_type=F32)
            act_ref[:, cols] = (g * jax.nn.sigmoid(g) * u).astype(BF16)
        return task

    def b_down(n):
        def task():
            cols = slice(n * DOWN_CHUNK, (n + 1) * DOWN_CHUNK)
            ff = jnp.dot(act_ref[...], wd_ref[:, cols], preferred_element_type=F32)
            o_ref[0, :, cols] = x1_ref[slot_b, :, cols] + mod_row(5, seq_b)[:, cols] * ff
        return task

    def b_final():
        x2 = o_ref[0]
        o_ref[0] = x2 * _rms_scale(x2) * nf_ref[...]

    d_ff = wg_ref.shape[1]
    back = [b_norm]
    back += [b_gate_up(n) for n in range(d_ff // FF_CHUNK)]
    back += [b_down(n) for n in range(D_MODEL // DOWN_CHUNK)]
    back += [b_final]

    def run(tasks):
        kv_cache.clear(), logits.clear(), probs.clear()
        for task in tasks:
            task()

    @pl.when(s == 0)
    def _():
        kt_ref[:, t:] = jnp.zeros((KV_WIDTH, WINDOW), BF16)
        v_ref[t:, :] = jnp.zeros((WINDOW, 2 * KV_WIDTH), BF16)
        ext_ref[t:, :] = jnp.zeros((POOL_HALO, POOL_WIDTH), F32)

        def fold_pool(c, rows_ref):
            g = c - ATTN_WIDTH // CAST_CHUNK_ROWS
            if g < 0:
                return None
            cols = slice(g * POOL_GROUP_WIDTH, (g + 1) * POOL_GROUP_WIDTH)
            return _dot_f32(wpool_ref[cols, :] * pscale_ref[:, cols], rows_ref[...])

        prime, casts = _weight_cast_tasks(
            [(win_hbm, win_ref, None), (wout_hbm, wout_ref, fold_pool)], stage_ref, cast_sems)
        prime()
        run(casts)
        prime, casts = _weight_cast_tasks(
            [(wg_hbm, wg_ref, None), (wu_hbm, wu_ref, None), (wd_hbm, wd_ref, None)],
            stage_ref, cast_sems)
        prime()
        run(_interleave(front, casts))

    @pl.when(s > 0)
    def _():
        run(_interleave(front, back))


def _layer(x, mod, positions, sinks, norm1, norm2, norm_f, w_in, w_pool, pool_scale, w_out,
           w_gate, w_up, w_down):
    b, s, d = x.shape
    t = TOKEN_TILE
    d_ff = w_gate.shape[1]
    assert d == D_MODEL and s % t == 0 and t % WINDOW == 0 and d_ff % FF_CHUNK == 0
    assert d_ff % CAST_CHUNK_ROWS == 0 and d % CAST_CHUNK_ROWS == 0
    tiles_per_seq = s // t
    n_tiles = b * tiles_per_seq

    def front_tile(step):
        return jnp.minimum(step, n_tiles - 1)

    def back_tile(step):
        return jnp.maximum(step - 1, 0)

    def tok_map(tile_of):
        return lambda step, *_: (tile_of(step) // tiles_per_seq, tile_of(step) % tiles_per_seq, 0)

    per_row = LANES // HALF
    inv_freq = ROPE_THETA ** (-jnp.arange(HALF, dtype=F32) * (2.0 / HEAD_DIM))
    freq = jnp.tile(inv_freq, per_row).reshape(1, LANES)
    pos_rep = jnp.repeat(positions.reshape(b * s // per_row, per_row), HALF, axis=1)
    grid_spec = pltpu.PrefetchScalarGridSpec(
        num_scalar_prefetch=1,
        grid=(n_tiles + 1,),
        in_specs=[
            pl.BlockSpec((1, t, d), tok_map(front_tile)),
            pl.BlockSpec((t // per_row, LANES), lambda step, *_: (front_tile(step), 0)),
            _const_spec((1, LANES)),
            _const_spec((N_MOD, b, d)),
            _const_spec((1, d)), _const_spec((1, d)), _const_spec((1, d)),
            _const_spec((1, POOL_WIDTH)),
            _const_spec((POOL_WIDTH, POOL_GROUP_WIDTH)),
        ] + [pl.BlockSpec(memory_space=pl.ANY)] * 5,
        out_specs=pl.BlockSpec((1, t, d), tok_map(back_tile)),
        scratch_shapes=[
            pltpu.VMEM((d, IN_PROJ_WIDTH), BF16),
            pltpu.VMEM((ATTN_WIDTH + POOL_WIDTH, d), BF16),
            pltpu.VMEM((d, d_ff), BF16),
            pltpu.VMEM((d, d_ff), BF16),
            pltpu.VMEM((d_ff, d), BF16),
            pltpu.VMEM((CAST_SLOTS, CAST_CHUNK_ROWS, max(d_ff, IN_PROJ_WIDTH)), F32),
            pltpu.SemaphoreType.DMA((CAST_SLOTS,)),
            pltpu.VMEM((t, LANES), F32),
            pltpu.VMEM((t, LANES), F32),
            pltpu.VMEM((t, d), BF16),
            pltpu.VMEM((t, ATTN_WIDTH), BF16),
            pltpu.VMEM((KV_WIDTH, t + WINDOW), BF16),
            pltpu.VMEM((t + WINDOW, 2 * KV_WIDTH), BF16),
            pltpu.VMEM((t + POOL_HALO, POOL_WIDTH), F32),
            pltpu.VMEM((t, ATTN_WIDTH + POOL_WIDTH), BF16),
            pltpu.VMEM((2, t, d), F32),
            pltpu.VMEM((t, d), BF16),
            pltpu.VMEM((t, d_ff), BF16),
        ],
    )
    return pl.pallas_call(
        functools.partial(_layer_kernel, tiles_per_seq=tiles_per_seq),
        grid_spec=grid_spec,
        out_shape=jax.ShapeDtypeStruct((b, s, d), F32),
        compiler_params=pltpu.CompilerParams(
            dimension_semantics=("arbitrary",), vmem_limit_bytes=VMEM_LIMIT_BYTES),
        name="layer",
    )(sinks, x, pos_rep, freq, mod, norm1.reshape(1, d), norm2.reshape(1, d), norm_f.reshape(1, d),
      pool_scale.reshape(1, -1), w_pool.reshape(POOL_WIDTH, POOL_GROUP_WIDTH), w_in, w_out,
      w_gate, w_up, w_down)


def kernel(x, c, positions, w_ada, b_ada, norm1, w_in, sinks, w_pool, pool_scale,
           w_out, norm2, w_gate, w_up, w_down, norm_f):
    mod = _adaln_mod(c, w_ada, b_ada)
    return _layer(x, mod, positions, sinks, norm1, norm2, norm_f, w_in, w_pool, pool_scale,
                  w_out, w_gate, w_up, w_down)
```

```python
import functools
import math

import jax
import jax.numpy as jnp
from jax import lax
from jax.experimental import pallas as pl
from jax.experimental.pallas import tpu as pltpu

F32 = jnp.float32
BF16 = jnp.bfloat16

D_MODEL = 1024
HEAD_DIM = 64
N_HEADS = 8
N_KV_HEADS = 2
GROUP = N_HEADS // N_KV_HEADS
ATTN_WIDTH = N_HEADS * HEAD_DIM
KV_WIDTH = N_KV_HEADS * HEAD_DIM
POOL_WINDOWS = (2, 4, 8, 16)
POOL_GROUP_WIDTH = 128
POOL_WIDTH = POOL_GROUP_WIDTH * len(POOL_WINDOWS)
IN_PROJ_WIDTH = ATTN_WIDTH + 2 * KV_WIDTH + POOL_WIDTH
WINDOW = 128
ROPE_THETA = 10000.0
N_MOD = 6
RMS_EPS = 1e-6
HALF = HEAD_DIM // 2

LANES = 128
POOL_HALO = 16
VMEM_LIMIT_BYTES = 56 * 1024 * 1024

TOKEN_TILE = 512
FF_CHUNK = 256
DOWN_CHUNK = 256
ATTN_SKEW = 3
CAST_CHUNK_ROWS = 128
CAST_SLOTS = 4
CAST_ROWS_PER_ITER = 16
OUT_PROJ_LAG = 2
BACK_SPAN = 0.9


def _const_spec(shape):
    zeros = (0,) * len(shape)
    return pl.BlockSpec(shape, lambda *_: zeros, pipeline_mode=pl.Buffered(1))


def _rms_scale(x):
    return lax.rsqrt(jnp.mean(x * x, axis=-1, keepdims=True) + RMS_EPS)


def _interleave(front, back):
    keyed = [((k + 0.5) / len(front), 0, k, f) for k, f in enumerate(front)]
    keyed += [(BACK_SPAN * (k + 0.5) / len(back), 1, k, f) for k, f in enumerate(back)]
    return [f for *_, f in sorted(keyed, key=lambda e: e[:3])]


def _mod_kernel(ct_ref, w_ref, b_ref, o_ref):
    ct = ct_ref[...]
    sc = ct * jax.nn.sigmoid(ct)
    w = w_ref[...]
    for r in range(ct.shape[1]):
        o_ref[0, r:r + 1, :] = jnp.sum(w * sc[:, r:r + 1], axis=0, keepdims=True) + b_ref[...]


def _adaln_mod(c, w_ada, b_ada):
    b, d = c.shape
    return pl.pallas_call(
        _mod_kernel,
        grid=(N_MOD,),
        in_specs=[
            pl.BlockSpec((d, b), lambda j: (0, 0)),
            pl.BlockSpec((d, d), lambda j: (0, j)),
            pl.BlockSpec((1, d), lambda j: (0, j)),
        ],
        out_specs=pl.BlockSpec((1, b, d), lambda j: (j, 0, 0)),
        out_shape=jax.ShapeDtypeStruct((N_MOD, b, d), F32),
        compiler_params=pltpu.CompilerParams(dimension_semantics=("arbitrary",)),
        name="adaln_mod",
    )(c.T, w_ada, b_ada.reshape(1, -1))


def _dot_f32(a, b):
    ah = a.astype(BF16)
    al = (a - ah.astype(F32)).astype(BF16)
    bh = b.astype(BF16)
    bl = (b - bh.astype(F32)).astype(BF16)
    dot = functools.partial(jnp.dot, preferred_element_type=F32)
    return dot(ah, bh) + dot(ah, bl) + dot(al, bh)


def _weight_cast_tasks(triples, stage_ref, sems):
    chunks = [(src, dst, c, fold) for src, dst, fold in triples
              for c in range(src.shape[0] // CAST_CHUNK_ROWS)]
    slots = stage_ref.shape[0]

    def copy(k):
        src, _, c, _ = chunks[k]
        return pltpu.make_async_copy(
            src.at[pl.ds(c * CAST_CHUNK_ROWS, CAST_CHUNK_ROWS), :],
            stage_ref.at[k % slots, :, pl.ds(0, src.shape[1])],
            sems.at[k % slots])

    def start(k):
        copy(k).start(priority=k % 2)

    def prime():
        for k in range(min(slots - 1, len(chunks))):
            start(k)

    def cast(k):
        def task():
            src, dst, c, fold = chunks[k]
            if k + slots - 1 < len(chunks):
                start(k + slots - 1)
            copy(k).wait()
            folded = fold(c, stage_ref.at[k % slots, :, pl.ds(0, src.shape[1])]) if fold else None
            if folded is not None:
                dst[c * CAST_CHUNK_ROWS:(c + 1) * CAST_CHUNK_ROWS, :] = folded.astype(BF16)
                return

            def cast_rows(r, carry):
                off = pl.multiple_of(r * CAST_ROWS_PER_ITER, CAST_ROWS_PER_ITER)
                dst[pl.ds(c * CAST_CHUNK_ROWS + off, CAST_ROWS_PER_ITER), :] = (
                    stage_ref[k % slots, pl.ds(off, CAST_ROWS_PER_ITER), 0:src.shape[1]]
                    .astype(BF16))
                return carry

            lax.fori_loop(0, CAST_CHUNK_ROWS // CAST_ROWS_PER_ITER, cast_rows, 0)
        return task

    return prime, [cast(k) for k in range(len(chunks))]


def _layer_kernel(sinks_ref, x_ref, pos_ref, freq_ref, mod_ref, n1_ref, n2_ref, nf_ref,
                  pscale_ref, wpool_ref, win_hbm, wout_hbm, wg_hbm, wu_hbm, wd_hbm,
                  o_ref,
                  win_ref, wout_ref, wg_ref, wu_ref, wd_ref, stage_ref, cast_sems,
                  cos_ref, sin_ref, h_ref, q_ref, kt_ref, v_ref, ext_ref, mix_ref, x1_ref, h2_ref,
                  act_ref, *, tiles_per_seq):
    t = x_ref.shape[1]
    s = pl.program_id(0)
    n_tiles = pl.num_programs(0) - 1
    tile_f = jnp.minimum(s, n_tiles - 1)
    first = (tile_f % tiles_per_seq) == 0
    slot_f = s % 2
    slot_b = 1 - slot_f
    seq_f = tile_f // tiles_per_seq
    seq_b = jnp.maximum(s - 1, 0) // tiles_per_seq

    def mod_row(k, seq):
        return mod_ref[k, pl.ds(seq, 1), :]

    lane =lax.broadcasted_iota(jnp.int32, (1, LANES), 1)
    low_lanes = lane < HEAD_DIM
    first_half = (lane % HEAD_DIM) < HALF

    def f_norm():
        x = x_ref[0]
        scale = n1_ref[...] * (1.0 + mod_row(1, seq_f))
        h_ref[...] = (x * _rms_scale(x) * scale + mod_row(0, seq_f)).astype(BF16)

    def f_trig():
        ang = pos_ref[...].astype(F32) * freq_ref[...]
        cos, sin = jnp.cos(ang), jnp.sin(ang)
        per_row = LANES // HALF
        for m in range(per_row):
            c32 = cos[:, m * HALF:(m + 1) * HALF]
            s32 = sin[:, m * HALF:(m + 1) * HALF]
            rows = pl.ds(m, t // per_row, stride=per_row)
            cos_ref[rows, :] = jnp.concatenate([c32, c32, c32, c32], axis=-1)
            sin_ref[rows, :] = jnp.concatenate([-s32, s32, -s32, s32], axis=-1)

    def rope(tile):
        partner = jnp.where(first_half,
                            pltpu.roll(tile, LANES - HALF, 1),
                            pltpu.roll(tile, HALF, 1))
        return tile * cos_ref[...] + partner * sin_ref[...]

    def f_q():
        u = jnp.dot(h_ref[...], win_ref[:, :ATTN_WIDTH], preferred_element_type=F32)
        q_scale = 1.0 / math.sqrt(HEAD_DIM)
        for j in range(ATTN_WIDTH // LANES):
            cols = slice(j * LANES, (j + 1) * LANES)
            q_ref[:, cols] = (rope(u[:, cols]) * q_scale).astype(BF16)

    def f_kv():
        kt_ref[:, 0:WINDOW] = kt_ref[:, t:]
        v_ref[0:WINDOW, :] = v_ref[t:, :]
        u = jnp.dot(h_ref[...], win_ref[:, ATTN_WIDTH:ATTN_WIDTH + 2 * KV_WIDTH],
                    preferred_element_type=F32)
        kt_ref[:, WINDOW:] = rope(u[:, :KV_WIDTH]).T.astype(BF16)
        vv = u[:, KV_WIDTH:]
        for g in range(N_KV_HEADS):
            vg = vv[:, g * HEAD_DIM:(g + 1) * HEAD_DIM]
            cols = slice(g * LANES, (g + 1) * LANES)
            v_ref[WINDOW:, cols] = jnp.concatenate([vg, vg], axis=-1).astype(BF16)

    def f_up():
        halo = ext_ref[t:, :]
        ext_ref[0:POOL_HALO, :] = jnp.where(first, jnp.zeros_like(halo), halo)
        ext_ref[POOL_HALO:, :] = jnp.dot(h_ref[...], win_ref[:, ATTN_WIDTH + 2 * KV_WIDTH:],
                                         preferred_element_type=F32)

    def f_pool(gi, w):
        def task():
            cols = slice(gi * POOL_GROUP_WIDTH, (gi + 1) * POOL_GROUP_WIDTH)
            pos_in_seq = ((tile_f % tiles_per_seq) * t
                          + lax.broadcasted_iota(jnp.int32, (t, 1), 0))
            tok = ext_ref[POOL_HALO:, cols]
            total = tok
            for back in range(1, w):
                total = total + ext_ref[POOL_HALO - back:POOL_HALO - back + t, cols]
            count = jnp.minimum(pos_in_seq + 1, w).astype(F32)
            mix_ref[:, ATTN_WIDTH + gi * POOL_GROUP_WIDTH:
                    ATTN_WIDTH + (gi + 1) * POOL_GROUP_WIDTH] = (total / count - tok).astype(BF16)
        return task

    n_blocks = t // WINDOW
    chunks = N_HEADS // 2
    bodies = [(j, chunk) for j in range(n_blocks) for chunk in range(chunks)]
    kv_cache, logits, probs = {}, {}, {}

    def band_mask(j):
        qi = lax.broadcasted_iota(jnp.int32, (WINDOW, 2 * WINDOW), 0)
        kj = lax.broadcasted_iota(jnp.int32, (WINDOW, 2 * WINDOW), 1)
        rel = kj - WINDOW - qi
        band = (rel <= 0) & (rel > -WINDOW)
        if j == 0:
            band = band & (kj >= jnp.where(first, WINDOW, 0))
        return band

    def block_diag_kv(j, g):
        if (j, g) not in kv_cache:
            keys = slice(j * WINDOW, (j + 2) * WINDOW)
            ktg = kt_ref[g * HEAD_DIM:(g + 1) * HEAD_DIM, keys]
            zk = jnp.zeros_like(ktg)
            k_bd = jnp.concatenate([jnp.concatenate([ktg, zk], axis=1),
                                    jnp.concatenate([zk, ktg], axis=1)], axis=0)
            vd = v_ref[keys, g * LANES:(g + 1) * LANES]
            zv = jnp.zeros_like(vd)
            v_bd = jnp.concatenate([jnp.where(low_lanes, vd, zv),
                                    jnp.where(low_lanes, zv, vd)], axis=0)
            kv_cache[(j, g)] = (k_bd, v_bd)
        return kv_cache[(j, g)]

    def scores(j, chunk):
        k_bd, _ = block_diag_kv(j, chunk // (GROUP // 2))
        qc = q_ref[j * WINDOW:(j + 1) * WINDOW, chunk * LANES:(chunk + 1) * LANES]
        return jnp.dot(qc, k_bd, preferred_element_type=F32)

    def softmax(j, chunk, lg):
        mask = band_mask(j)
        ps, inv_den = [], []
        for hh in range(2):
            sink = sinks_ref[2 * chunk + hh]
            l = jnp.where(mask, lg[:, hh * 2 * WINDOW:(hh + 1) * 2 * WINDOW], -jnp.inf)
            m = jnp.maximum(jnp.max(l, axis=-1, keepdims=True), sink)
            p = jnp.exp(l - m)
            den = jnp.sum(p, axis=-1, keepdims=True) + jnp.exp(sink - m)
            ps.append(p.astype(BF16))
            inv_den.append(1.0 / den)
        return jnp.concatenate(ps, axis=-1), jnp.where(low_lanes, inv_den[0], inv_den[1])

    def values(j, chunk, p, inv_den):
        _, v_bd = block_diag_kv(j, chunk // (GROUP // 2))
        pv = jnp.dot(p, v_bd, preferred_element_type=F32)
        mix_ref[j * WINDOW:(j + 1) * WINDOW, chunk * LANES:(chunk + 1) * LANES] = (
            (pv * inv_den).astype(BF16))

    def out_proj(j):
        rows = slice(j * WINDOW, (j + 1) * WINDOW)
        mixed = jnp.dot(mix_ref[rows, :], wout_ref[...], preferred_element_type=F32)
        x1_ref[slot_f, rows, :] = x_ref[0, rows, :] + mod_row(2, seq_f) * mixed

    def f_attn(step):
        def task():
            n = len(bodies)
            if step < n:
                logits[step] = scores(*bodies[step])
            if 0 <= step - 1 < n:
                probs[step - 1] = softmax(*bodies[step - 1], logits.pop(step - 1))
            done = step - ATTN_SKEW
            if 0 <= done < n:
                values(*bodies[done], *probs.pop(done))
            ready = step - ATTN_SKEW - OUT_PROJ_LAG
            if 0 <= ready < n and bodies[ready][1] == chunks - 1:
                out_proj(bodies[ready][0])
        return task

    front = [f_norm, f_trig, f_q, f_kv, f_up]
    front += [f_pool(gi, w) for gi, w in enumerate(POOL_WINDOWS)]
    front += [f_attn(step) for step in range(len(bodies) + ATTN_SKEW + OUT_PROJ_LAG)]

    def b_norm():
        x1 = x1_ref[slot_b]
        scale = n2_ref[...] * (1.0 + mod_row(4, seq_b))
        h2_ref[...] = (x1 * _rms_scale(x1) * scale + mod_row(3, seq_b)).astype(BF16)

    def b_gate_up(n):
        def task():
            cols = slice(n * FF_CHUNK, (n + 1) * FF_CHUNK)
            g = jnp.dot(h2_ref[...], wg_ref[:, cols], preferred_element_type=F32)
            u = jnp.dot(h2_ref[...], wu_ref[:, cols], preferred_element_type=F32)
            act_ref[:, cols] = (g * jax.nn.sigmoid(g) * u).astype(BF16)
        return task

    def b_down(n):
        def task():
            cols = slice(n * DOWN_CHUNK, (n + 1) * DOWN_CHUNK)
            ff = jnp.dot(act_ref[...], wd_ref[:, cols], preferred_element_type=F32)
            o_ref[0, :, cols] = x1_ref[slot_b, :, cols] + mod_row(5, seq_b)[:, cols] * ff
        return task

    def b_final():
        x2 = o_ref[0]
        o_ref[0] = x2 * _rms_scale(x2) * nf_ref[...]

    d_ff = wg_ref.shape[1]
    back = [b_norm]
    back += [b_gate_up(n) for n in range(d_ff // FF_CHUNK)]
    back += [b_down(n) for n in range(D_MODEL // DOWN_CHUNK)]
    back += [b_final]

    def run(tasks):
        kv_cache.clear(), logits.clear(), probs.clear()
        for task in tasks:
            task()

    @pl.when(s == 0)
    def _():
        kt_ref[:, t:] = jnp.zeros((KV_WIDTH, WINDOW), BF16)
        v_ref[t:, :] = jnp.zeros((WINDOW, 2 * KV_WIDTH), BF16)
        ext_ref[t:, :] = jnp.zeros((POOL_HALO, POOL_WIDTH), F32)

        def fold_pool(c, rows_ref):
            g = c - ATTN_WIDTH // CAST_CHUNK_ROWS
            if g < 0:
                return None
            cols = slice(g * POOL_GROUP_WIDTH, (g + 1) * POOL_GROUP_WIDTH)
            return _dot_f32(wpool_ref[cols, :] * pscale_ref[:, cols], rows_ref[...])

        prime, casts = _weight_cast_tasks(
            [(win_hbm, win_ref, None), (wout_hbm, wout_ref, fold_pool),
             (wg_hbm, wg_ref, None), (wu_hbm, wu_ref, None), (wd_hbm, wd_ref, None)],
            stage_ref, cast_sems)
        front_casts = (win_hbm.shape[0] + wout_hbm.shape[0]) // CAST_CHUNK_ROWS
        prime()
        run(casts[:front_casts])
        run(_interleave(front, casts[front_casts:]))

    @pl.when(s > 0)
    def _():
        run(_interleave(front, back))


def _layer(x, mod, positions, sinks, norm1, norm2, norm_f, w_in, w_pool, pool_scale, w_out,
           w_gate, w_up, w_down):
    b, s, d = x.shape
    t = TOKEN_TILE
    d_ff = w_gate.shape[1]
    assert d == D_MODEL and s % t == 0 and t % WINDOW == 0 and d_ff % FF_CHUNK == 0
    assert d_ff % CAST_CHUNK_ROWS == 0 and d % CAST_CHUNK_ROWS == 0
    tiles_per_seq = s // t
    n_tiles = b * tiles_per_seq

    def front_tile(step):
        return jnp.minimum(step, n_tiles - 1)

    def back_tile(step):
        return jnp.maximum(step - 1, 0)

    def tok_map(tile_of):
        return lambda step, *_: (tile_of(step) // tiles_per_seq, tile_of(step) % tiles_per_seq, 0)

    per_row = LANES // HALF
    inv_freq = ROPE_THETA ** (-jnp.arange(HALF, dtype=F32) * (2.0 / HEAD_DIM))
    freq = jnp.tile(inv_freq, per_row).reshape(1, LANES)
    pos_rep = jnp.repeat(positions.reshape(b * s // per_row, per_row), HALF, axis=1)
    grid_spec = pltpu.PrefetchScalarGridSpec(
        num_scalar_prefetch=1,
        grid=(n_tiles + 1,),
        in_specs=[
            pl.BlockSpec((1, t, d), tok_map(front_tile)),
            pl.BlockSpec((t // per_row, LANES), lambda step, *_: (front_tile(step), 0)),
            _const_spec((1, LANES)),
            _const_spec((N_MOD, b, d)),
            _const_spec((1, d)), _const_spec((1, d)), _const_spec((1, d)),
            _const_spec((1, POOL_WIDTH)),
            _const_spec((POOL_WIDTH, POOL_GROUP_WIDTH)),
        ] + [pl.BlockSpec(memory_space=pl.ANY)] * 5,
        out_specs=pl.BlockSpec((1, t, d), tok_map(back_tile)),
        scratch_shapes=[
            pltpu.VMEM((d, IN_PROJ_WIDTH), BF16),
            pltpu.VMEM((ATTN_WIDTH + POOL_WIDTH, d), BF16),
            pltpu.VMEM((d, d_ff), BF16),
            pltpu.VMEM((d, d_ff), BF16),
            pltpu.VMEM((d_ff, d), BF16),
            pltpu.VMEM((CAST_SLOTS, CAST_CHUNK_ROWS, max(d_ff, IN_PROJ_WIDTH)), F32),
            pltpu.SemaphoreType.DMA((CAST_SLOTS,)),
            pltpu.VMEM((t, LANES), F32),
            pltpu.VMEM((t, LANES), F32),
            pltpu.VMEM((t, d), BF16),
            pltpu.VMEM((t, ATTN_WIDTH), BF16),
            pltpu.VMEM((KV_WIDTH, t + WINDOW), BF16),
            pltpu.VMEM((t + WINDOW, 2 * KV_WIDTH), BF16),
            pltpu.VMEM((t + POOL_HALO, POOL_WIDTH), F32),
            pltpu.VMEM((t, ATTN_WIDTH + POOL_WIDTH), BF16),
            pltpu.VMEM((2, t, d), F32),
            pltpu.VMEM((t, d), BF16),
            pltpu.VMEM((t, d_ff), BF16),
        ],
    )
    return pl.pallas_call(
        functools.partial(_layer_kernel, tiles_per_seq=tiles_per_seq),
        grid_spec=grid_spec,
        out_shape=jax.ShapeDtypeStruct((b, s, d), F32),
        compiler_params=pltpu.CompilerParams(
            dimension_semantics=("arbitrary",), vmem_limit_bytes=VMEM_LIMIT_BYTES),
        name="layer",
    )(sinks, x, pos_rep, freq, mod, norm1.reshape(1, d), norm2.reshape(1, d), norm_f.reshape(1, d),
      pool_scale.reshape(1, -1), w_pool.reshape(POOL_WIDTH, POOL_GROUP_WIDTH), w_in, w_out,
      w_gate, w_up, w_down)


def kernel(x, c, positions, w_ada, b_ada, norm1, w_in, sinks, w_pool, pool_scale,
           w_out, norm2, w_gate, w_up, w_down, norm_f):
    mod = _adaln_mod(c, w_ada, b_ada)
    return _layer(x, mod, positions, sinks, norm1, norm2, norm_f, w_in, w_pool, pool_scale,
                  w_out, w_gate, w_up, w_down)
```

```python
import functools
import math

import jax
import jax.numpy as jnp
from jax import lax
from jax.experimental import pallas as pl
from jax.experimental.pallas import tpu as pltpu

F32 = jnp.float32
BF16 = jnp.bfloat16

D_MODEL = 1024
HEAD_DIM = 64
N_HEADS = 8
N_KV_HEADS = 2
GROUP = N_HEADS // N_KV_HEADS
ATTN_WIDTH = N_HEADS * HEAD_DIM
KV_WIDTH = N_KV_HEADS * HEAD_DIM
POOL_WINDOWS = (2, 4, 8, 16)
POOL_GROUP_WIDTH = 128
POOL_WIDTH = POOL_GROUP_WIDTH * len(POOL_WINDOWS)
IN_PROJ_WIDTH = ATTN_WIDTH + 2 * KV_WIDTH + POOL_WIDTH
WINDOW = 128
ROPE_THETA = 10000.0
N_MOD = 6
RMS_EPS = 1e-6
HALF = HEAD_DIM // 2

LANES = 128
POOL_HALO = 16
VMEM_LIMIT_BYTES = 56 * 1024 * 1024

TOKEN_TILE = 512
FF_CHUNK = 256
DOWN_CHUNK = 256
ATTN_SKEW = 3
CAST_CHUNK_ROWS = 128
CAST_SLOTS = 4
CAST_ROWS_PER_ITER = 16
OUT_PROJ_LAG = 2
LATE_OUT_PROJ = 2
BACK_SPAN = 0.9


def _const_spec(shape):
    zeros = (0,) * len(shape)
    return pl.BlockSpec(shape, lambda *_: zeros, pipeline_mode=pl.Buffered(1))


def _rms_scale(x):
    return lax.rsqrt(jnp.mean(x * x, axis=-1, keepdims=True) + RMS_EPS)


def _interleave(front, back):
    keyed = [((k + 0.5) / len(front), 0, k, f) for k, f in enumerate(front)]
    keyed += [(BACK_SPAN * (k + 0.5) / len(back), 1, k, f) for k, f in enumerate(back)]
    return [f for *_, f in sorted(keyed, key=lambda e: e[:3])]


def _mod_kernel(ct_ref, w_ref, b_ref, o_ref):
    ct = ct_ref[...]
    sc = ct * jax.nn.sigmoid(ct)
    w = w_ref[...]
    for r in range(ct.shape[1]):
        o_ref[0, r:r + 1, :] = jnp.sum(w * sc[:, r:r + 1], axis=0, keepdims=True) + b_ref[...]


def _adaln_mod(c, w_ada, b_ada):
    b, d = c.shape
    return pl.pallas_call(
        _mod_kernel,
        grid=(N_MOD,),
        in_specs=[
            pl.BlockSpec((d, b), lambda j: (0, 0)),
            pl.BlockSpec((d, d), lambda j: (0, j)),
            pl.BlockSpec((1, d), lambda j: (0, j)),
        ],
        out_specs=pl.BlockSpec((1, b, d), lambda j: (j, 0, 0)),
        out_shape=jax.ShapeDtypeStruct((N_MOD, b, d), F32),
        compiler_params=pltpu.CompilerParams(dimension_semantics=("arbitrary",)),
        name="adaln_mod",
    )(c.T, w_ada, b_ada.reshape(1, -1))


def _dot_f32(a, b):
    ah = a.astype(BF16)
    al = (a - ah.astype(F32)).astype(BF16)
    bh = b.astype(BF16)
    bl = (b - bh.astype(F32)).astype(BF16)
    dot = functools.partial(jnp.dot, preferred_element_type=F32)
    return dot(ah, bh) + dot(ah, bl) + dot(al, bh)


def _weight_cast_tasks(triples, stage_ref, sems):
    chunks = [(src, dst, c, fold) for src, dst, fold in triples
              for c in range(src.shape[0] // CAST_CHUNK_ROWS)]
    slots = stage_ref.shape[0]

    def copy(k):
        src, _, c, _ = chunks[k]
        return pltpu.make_async_copy(
            src.at[pl.ds(c * CAST_CHUNK_ROWS, CAST_CHUNK_ROWS), :],
            stage_ref.at[k % slots, :, pl.ds(0, src.shape[1])],
            sems.at[k % slots])

    def start(k):
        copy(k).start(priority=k % 2)

    def prime():
        for k in range(min(slots - 1, len(chunks))):
            start(k)

    def cast(k):
        def task():
            src, dst, c, fold = chunks[k]
            if k + slots - 1 < len(chunks):
                start(k + slots - 1)
            copy(k).wait()
            folded = fold(c, stage_ref.at[k % slots, :, pl.ds(0, src.shape[1])]) if fold else None
            if folded is not None:
                dst[c * CAST_CHUNK_ROWS:(c + 1) * CAST_CHUNK_ROWS, :] = folded.astype(BF16)
                return

            def cast_rows(r, carry):
                off = pl.multiple_of(r * CAST_ROWS_PER_ITER, CAST_ROWS_PER_ITER)
                dst[pl.ds(c * CAST_CHUNK_ROWS + off, CAST_ROWS_PER_ITER), :] = (
                    stage_ref[k % slots, pl.ds(off, CAST_ROWS_PER_ITER), 0:src.shape[1]]
                    .astype(BF16))
                return carry

            lax.fori_loop(0, CAST_CHUNK_ROWS // CAST_ROWS_PER_ITER, cast_rows, 0)
        return task

    return prime, [cast(k) for k in range(len(chunks))]


def _layer_kernel(sinks_ref, x_ref, pos_ref, freq_ref, mod_ref, n1_ref, n2_ref, nf_ref,
                  pscale_ref, wpool_ref, win_hbm, wout_hbm, wg_hbm, wu_hbm, wd_hbm,
                  o_ref,
                  win_ref, wout_ref, wg_ref, wu_ref, wd_ref, stage_ref, cast_sems,
                  cos_ref, sin_ref, h_ref, q_ref, kt_ref, v_ref, ext_ref, mix_ref, x1_ref, h2_ref,
                  act_ref, *, tiles_per_seq):
    t = x_ref.shape[1]
    s = pl.program_id(0)
    n_tiles = pl.num_programs(0) - 1
    tile_f = jnp.minimum(s, n_tiles - 1)
    first = (tile_f % tiles_per_seq) == 0
    slot_f = s % 2
    slot_b = 1 - slot_f
    seq_f = tile_f // tiles_per_seq
    seq_b = jnp.maximum(s - 1, 0) // tiles_per_seq

    def mod_row(k, seq):
        return mod_ref[k, pl.ds(seq, 1), :]

    lane =lax.broadcasted_iota(jnp.int32, (1, LANES), 1)
    low_lanes = lane < HEAD_DIM
    first_half = (lane % HEAD_DIM) < HALF

    def f_norm():
        x = x_ref[0]
        scale = n1_ref[...] * (1.0 + mod_row(1, seq_f))
        h_ref[...] = (x * _rms_scale(x) * scale + mod_row(0, seq_f)).astype(BF16)

    def f_trig():
        ang = pos_ref[...].astype(F32) * freq_ref[...]
        cos, sin = jnp.cos(ang), jnp.sin(ang)
        per_row = LANES // HALF
        for m in range(per_row):
            c32 = cos[:, m * HALF:(m + 1) * HALF]
            s32 = sin[:, m * HALF:(m + 1) * HALF]
            rows = pl.ds(m, t // per_row, stride=per_row)
            cos_ref[rows, :] = jnp.concatenate([c32, c32, c32, c32], axis=-1)
            sin_ref[rows, :] = jnp.concatenate([-s32, s32, -s32, s32], axis=-1)

    def rope(tile):
        partner = jnp.where(first_half,
                            pltpu.roll(tile, LANES - HALF, 1),
                            pltpu.roll(tile, HALF, 1))
        return tile * cos_ref[...] + partner * sin_ref[...]

    def f_q():
        u = jnp.dot(h_ref[...], win_ref[:, :ATTN_WIDTH], preferred_element_type=F32)
        q_scale = 1.0 / math.sqrt(HEAD_DIM)
        for j in range(ATTN_WIDTH // LANES):
            cols = slice(j * LANES, (j + 1) * LANES)
            q_ref[:, cols] = (rope(u[:, cols]) * q_scale).astype(BF16)

    def f_kv():
        kt_ref[:, 0:WINDOW] = kt_ref[:, t:]
        v_ref[0:WINDOW, :] = v_ref[t:, :]
        u = jnp.dot(h_ref[...], win_ref[:, ATTN_WIDTH:ATTN_WIDTH + 2 * KV_WIDTH],
                    preferred_element_type=F32)
        kt_ref[:, WINDOW:] = rope(u[:, :KV_WIDTH]).T.astype(BF16)
        vv = u[:, KV_WIDTH:]
        for g in range(N_KV_HEADS):
            vg = vv[:, g * HEAD_DIM:(g + 1) * HEAD_DIM]
            cols = slice(g * LANES, (g + 1) * LANES)
            v_ref[WINDOW:, cols] = jnp.concatenate([vg, vg], axis=-1).astype(BF16)

    def f_up():
        halo = ext_ref[t:, :]
        ext_ref[0:POOL_HALO, :] = jnp.where(first, jnp.zeros_like(halo), halo)
        ext_ref[POOL_HALO:, :] = jnp.dot(h_ref[...], win_ref[:, ATTN_WIDTH + 2 * KV_WIDTH:],
                                         preferred_element_type=F32)

    def f_pool(gi, w):
        def task():
            cols = slice(gi * POOL_GROUP_WIDTH, (gi + 1) * POOL_GROUP_WIDTH)
            pos_in_seq = ((tile_f % tiles_per_seq) * t
                          + lax.broadcasted_iota(jnp.int32, (t, 1), 0))
            tok = ext_ref[POOL_HALO:, cols]
            total = tok
            for back in range(1, w):
                total = total + ext_ref[POOL_HALO - back:POOL_HALO - back + t, cols]
            count = jnp.minimum(pos_in_seq + 1, w).astype(F32)
            mix_ref[:, ATTN_WIDTH + gi * POOL_GROUP_WIDTH:
                    ATTN_WIDTH + (gi + 1) * POOL_GROUP_WIDTH] = (total / count - tok).astype(BF16)
        return task

    n_blocks = t // WINDOW
    chunks = N_HEADS // 2
    bodies = [(j, chunk) for j in range(n_blocks) for chunk in range(chunks)]
    kv_cache, logits, probs = {}, {}, {}

    def band_mask(j):
        qi = lax.broadcasted_iota(jnp.int32, (WINDOW, 2 * WINDOW), 0)
        kj = lax.broadcasted_iota(jnp.int32, (WINDOW, 2 * WINDOW), 1)
        rel = kj - WINDOW - qi
        band = (rel <= 0) & (rel > -WINDOW)
        if j == 0:
            band = band & (kj >= jnp.where(first, WINDOW, 0))
        return band

    def block_diag_kv(j, g):
        if (j, g) not in kv_cache:
            keys = slice(j * WINDOW, (j + 2) * WINDOW)
            ktg = kt_ref[g * HEAD_DIM:(g + 1) * HEAD_DIM, keys]
            zk = jnp.zeros_like(ktg)
            k_bd = jnp.concatenate([jnp.concatenate([ktg, zk], axis=1),
                                    jnp.concatenate([zk, ktg], axis=1)], axis=0)
            vd = v_ref[keys, g * LANES:(g + 1) * LANES]
            zv = jnp.zeros_like(vd)
            v_bd = jnp.concatenate([jnp.where(low_lanes, vd, zv),
                                    jnp.where(low_lanes, zv, vd)], axis=0)
            kv_cache[(j, g)] = (k_bd, v_bd)
        return kv_cache[(j, g)]

    def scores(j, chunk):
        k_bd, _ = block_diag_kv(j, chunk // (GROUP // 2))
        qc = q_ref[j * WINDOW:(j + 1) * WINDOW, chunk * LANES:(chunk + 1) * LANES]
        return jnp.dot(qc, k_bd, preferred_element_type=F32)

    def softmax(j, chunk, lg):
        mask = band_mask(j)
        ps, inv_den = [], []
        for hh in range(2):
            sink = sinks_ref[2 * chunk + hh]
            l = jnp.where(mask, lg[:, hh * 2 * WINDOW:(hh + 1) * 2 * WINDOW], -jnp.inf)
            m = jnp.maximum(jnp.max(l, axis=-1, keepdims=True), sink)
            p = jnp.exp(l - m)
            den = jnp.sum(p, axis=-1, keepdims=True) + jnp.exp(sink - m)
            ps.append(p.astype(BF16))
            inv_den.append(1.0 / den)
        return jnp.concatenate(ps, axis=-1), jnp.where(low_lanes, inv_den[0], inv_den[1])

    def values(j, chunk, p, inv_den):
        _, v_bd = block_diag_kv(j, chunk // (GROUP // 2))
        pv = jnp.dot(p, v_bd, preferred_element_type=F32)
        mix_ref[j * WINDOW:(j + 1) * WINDOW, chunk * LANES:(chunk + 1) * LANES] = (
            (pv * inv_den).astype(BF16))

    def out_proj(j):
        rows = slice(j * WINDOW, (j + 1) * WINDOW)
        mixed = jnp.dot(mix_ref[rows, :], wout_ref[...], preferred_element_type=F32)
        x1_ref[slot_f, rows, :] = x_ref[0, rows, :] + mod_row(2, seq_f) * mixed

    def f_attn(step):
        def task():
            n = len(bodies)
            if step < n:
                logits[step] = scores(*bodies[step])
            if 0 <= step - 1 < n:
                probs[step - 1] = softmax(*bodies[step - 1], logits.pop(step - 1))
            done = step - ATTN_SKEW
            if 0 <= done < n:
                values(*bodies[done], *probs.pop(done))
            ready = step - ATTN_SKEW - OUT_PROJ_LAG
            if (0 <= ready < n and bodies[ready][1] == chunks - 1
                    and bodies[ready][0] < n_blocks - LATE_OUT_PROJ):
                out_proj(bodies[ready][0])
        return task

    late_out_proj = [functools.partial(out_proj, j)
                     for j in range(n_blocks - LATE_OUT_PROJ, n_blocks)]

    front = [f_norm, f_trig, f_q, f_kv, f_up]
    front += [f_pool(gi, w) for gi, w in enumerate(POOL_WINDOWS)]
    front += [f_attn(step) for step in range(len(bodies) + ATTN_SKEW + OUT_PROJ_LAG)]

    def b_norm():
        x1 = x1_ref[slot_b]
        scale = n2_ref[...] * (1.0 + mod_row(4, seq_b))
        h2_ref[...] = (x1 * _rms_scale(x1) * scale + mod_row(3, seq_b)).astype(BF16)

    def b_gate_up(n):
        def task():
            cols = slice(n * FF_CHUNK, (n + 1) * FF_CHUNK)
            g = jnp.dot(h2_ref[...], wg_ref[:, cols], preferred_element_type=F32)
            u = jnp.dot(h2_ref[...], wu_ref[:, cols], preferred_element_type=F32)
            act_ref[:, cols] = (g * jax.nn.sigmoid(g) * u).astype(BF16)
        return task

    def b_down(n):
        def task():
            cols = slice(n * DOWN_CHUNK, (n + 1) * DOWN_CHUNK)
            ff = jnp.dot(act_ref[...], wd_ref[:, cols], preferred_element_type=F32)
            o_ref[0, :, cols] = x1_ref[slot_b, :, cols] + mod_row(5, seq_b)[:, cols] * ff
        return task

    def b_final():
        x2 = o_ref[0]
        o_ref[0] = x2 * _rms_scale(x2) * nf_ref[...]

    d_ff = wg_ref.shape[1]
    back = [b_norm]
    back += [b_gate_up(n) for n in range(d_ff // FF_CHUNK)]
    back += [b_down(n) for n in range(D_MODEL // DOWN_CHUNK)]
    back += [b_final]

    def run(tasks):
        kv_cache.clear(), logits.clear(), probs.clear()
        for task in tasks:
            task()

    @pl.when(s == 0)
    def _():
        kt_ref[:, t:] = jnp.zeros((KV_WIDTH, WINDOW), BF16)
        v_ref[t:, :] = jnp.zeros((WINDOW, 2 * KV_WIDTH), BF16)
        ext_ref[t:, :] = jnp.zeros((POOL_HALO, POOL_WIDTH), F32)

        def fold_pool(c, rows_ref):
            g = c - ATTN_WIDTH // CAST_CHUNK_ROWS
            if g < 0:
                return None
            cols = slice(g * POOL_GROUP_WIDTH, (g + 1) * POOL_GROUP_WIDTH)
            return _dot_f32(wpool_ref[cols, :] * pscale_ref[:, cols], rows_ref[...])

        prime, casts = _weight_cast_tasks(
            [(win_hbm, win_ref, None), (wout_hbm, wout_ref, fold_pool),
             (wg_hbm, wg_ref, None), (wu_hbm, wu_ref, None), (wd_hbm, wd_ref, None)],
            stage_ref, cast_sems)
        front_casts = (win_hbm.shape[0] + wout_hbm.shape[0]) // CAST_CHUNK_ROWS
        prime()
        run(casts[:front_casts])
        run(_interleave(front, casts[front_casts:]) + late_out_proj)

    @pl.when(s > 0)
    def _():
        order = [task for task in _interleave(front, back) if task not in back[-2:]]
        run(order + [back[-2], late_out_proj[0], back[-1], late_out_proj[1]])


def _layer(x, mod, positions, sinks, norm1, norm2, norm_f, w_in, w_pool, pool_scale, w_out,
           w_gate, w_up, w_down):
    b, s, d = x.shape
    t = TOKEN_TILE
    d_ff = w_gate.shape[1]
    assert d == D_MODEL and s % t == 0 and t % WINDOW == 0 and d_ff % FF_CHUNK == 0
    assert d_ff % CAST_CHUNK_ROWS == 0 and d % CAST_CHUNK_ROWS == 0
    tiles_per_seq = s // t
    n_tiles = b * tiles_per_seq

    def front_tile(step):
        return jnp.minimum(step, n_tiles - 1)

    def back_tile(step):
        return jnp.maximum(step - 1, 0)

    def tok_map(tile_of):
        return lambda step, *_: (tile_of(step) // tiles_per_seq, tile_of(step) % tiles_per_seq, 0)

    per_row = LANES // HALF
    inv_freq = ROPE_THETA ** (-jnp.arange(HALF, dtype=F32) * (2.0 / HEAD_DIM))
    freq = jnp.tile(inv_freq, per_row).reshape(1, LANES)
    pos_rep = jnp.repeat(positions.reshape(b * s // per_row, per_row), HALF, axis=1)
    grid_spec = pltpu.PrefetchScalarGridSpec(
        num_scalar_prefetch=1,
        grid=(n_tiles + 1,),
        in_specs=[
            pl.BlockSpec((1, t, d), tok_map(front_tile)),
            pl.BlockSpec((t // per_row, LANES), lambda step, *_: (front_tile(step), 0)),
            _const_spec((1, LANES)),
            _const_spec((N_MOD, b, d)),
            _const_spec((1, d)), _const_spec((1, d)), _const_spec((1, d)),
            _const_spec((1, POOL_WIDTH)),
            _const_spec((POOL_WIDTH, POOL_GROUP_WIDTH)),
        ] + [pl.BlockSpec(memory_space=pl.ANY)] * 5,
        out_specs=pl.BlockSpec((1, t, d), tok_map(back_tile)),
        scratch_shapes=[
            pltpu.VMEM((d, IN_PROJ_WIDTH), BF16),
            pltpu.VMEM((ATTN_WIDTH + POOL_WIDTH, d), BF16),
            pltpu.VMEM((d, d_ff), BF16),
            pltpu.VMEM((d, d_ff), BF16),
            pltpu.VMEM((d_ff, d), BF16),
            pltpu.VMEM((CAST_SLOTS, CAST_CHUNK_ROWS, max(d_ff, IN_PROJ_WIDTH)), F32),
            pltpu.SemaphoreType.DMA((CAST_SLOTS,)),
            pltpu.VMEM((t, LANES), F32),
            pltpu.VMEM((t, LANES), F32),
            pltpu.VMEM((t, d), BF16),
            pltpu.VMEM((t, ATTN_WIDTH), BF16),
            pltpu.VMEM((KV_WIDTH, t + WINDOW), BF16),
            pltpu.VMEM((t + WINDOW, 2 * KV_WIDTH), BF16),
            pltpu.VMEM((t + POOL_HALO, POOL_WIDTH), F32),
            pltpu.VMEM((t, ATTN_WIDTH + POOL_WIDTH), BF16),
            pltpu.VMEM((2, t, d), F32),
            pltpu.VMEM((t, d), BF16),
            pltpu.VMEM((t, d_ff), BF16),
        ],
    )
    return pl.pallas_call(
        functools.partial(_layer_kernel, tiles_per_seq=tiles_per_seq),
        grid_spec=grid_spec,
        out_shape=jax.ShapeDtypeStruct((b, s, d), F32),
        compiler_params=pltpu.CompilerParams(
            dimension_semantics=("arbitrary",), vmem_limit_bytes=VMEM_LIMIT_BYTES),
        name="layer",
    )(sinks, x, pos_rep, freq, mod, norm1.reshape(1, d), norm2.reshape(1, d), norm_f.reshape(1, d),
      pool_scale.reshape(1, -1), w_pool.reshape(POOL_WIDTH, POOL_GROUP_WIDTH), w_in, w_out,
      w_gate, w_up, w_down)


def kernel(x, c, positions, w_ada, b_ada, norm1, w_in, sinks, w_pool, pool_scale,
           w_out, norm2, w_gate, w_up, w_down, norm_f):
    mod = _adaln_mod(c, w_ada, b_ada)
    return _layer(x, mod, positions, sinks, norm1, norm2, norm_f, w_in, w_pool, pool_scale,
                  w_out, w_gate, w_up, w_down)
```

```python
import functools
import math

import jax
import jax.numpy as jnp
from jax import lax
from jax.experimental import pallas as pl
from jax.experimental.pallas import tpu as pltpu

F32 = jnp.float32
BF16 = jnp.bfloat16

D_MODEL = 1024
HEAD_DIM = 64
N_HEADS = 8
N_KV_HEADS = 2
GROUP = N_HEADS // N_KV_HEADS
ATTN_WIDTH = N_HEADS * HEAD_DIM
KV_WIDTH = N_KV_HEADS * HEAD_DIM
POOL_WINDOWS = (2, 4, 8, 16)
POOL_GROUP_WIDTH = 128
POOL_WIDTH = POOL_GROUP_WIDTH * len(POOL_WINDOWS)
IN_PROJ_WIDTH = ATTN_WIDTH + 2 * KV_WIDTH + POOL_WIDTH
WINDOW = 128
ROPE_THETA = 10000.0
N_MOD = 6
RMS_EPS = 1e-6
LOG2_E = math.log2(math.e)
HALF = HEAD_DIM // 2

LANES = 128
POOL_HALO = 16
VMEM_LIMIT_BYTES = 56 * 1024 * 1024

TOKEN_TILE = 512
FF_CHUNK = 256
DOWN_CHUNK = 256
ATTN_SKEW = 3
CAST_CHUNK_ROWS = 128
CAST_SLOTS = 4
CAST_ROWS_PER_ITER = 16
OUT_PROJ_LAG = 2
LATE_OUT_PROJ = 2
BACK_SPAN = 0.9


def _const_spec(shape):
    zeros = (0,) * len(shape)
    return pl.BlockSpec(shape, lambda *_: zeros, pipeline_mode=pl.Buffered(1))


def _rms_scale(x):
    return lax.rsqrt(jnp.mean(x * x, axis=-1, keepdims=True) + RMS_EPS)


def _interleave(front, back):
    keyed = [((k + 0.5) / len(front), 0, k, f) for k, f in enumerate(front)]
    keyed += [(BACK_SPAN * (k + 0.5) / len(back), 1, k, f) for k, f in enumerate(back)]
    return [f for *_, f in sorted(keyed, key=lambda e: e[:3])]


def _mod_kernel(ct_ref, w_ref, b_ref, o_ref):
    ct = ct_ref[...]
    sc = ct * jax.nn.sigmoid(ct)
    w = w_ref[...]
    for r in range(ct.shape[1]):
        o_ref[0, r:r + 1, :] = jnp.sum(w * sc[:, r:r + 1], axis=0, keepdims=True) + b_ref[...]


def _adaln_mod(c, w_ada, b_ada):
    b, d = c.shape
    return pl.pallas_call(
        _mod_kernel,
        grid=(N_MOD,),
        in_specs=[
            pl.BlockSpec((d, b), lambda j: (0, 0)),
            pl.BlockSpec((d, d), lambda j: (0, j)),
            pl.BlockSpec((1, d), lambda j: (0, j)),
        ],
        out_specs=pl.BlockSpec((1, b, d), lambda j: (j, 0, 0)),
        out_shape=jax.ShapeDtypeStruct((N_MOD, b, d), F32),
        compiler_params=pltpu.CompilerParams(dimension_semantics=("arbitrary",)),
        name="adaln_mod",
    )(c.T, w_ada, b_ada.reshape(1, -1))


def _dot_f32(a, b):
    ah = a.astype(BF16)
    al = (a - ah.astype(F32)).astype(BF16)
    bh = b.astype(BF16)
    bl = (b - bh.astype(F32)).astype(BF16)
    dot = functools.partial(jnp.dot, preferred_element_type=F32)
    return dot(ah, bh) + dot(ah, bl) + dot(al, bh)


def _weight_cast_tasks(triples, stage_ref, sems):
    chunks = [(src, dst, c, fold) for src, dst, fold in triples
              for c in range(src.shape[0] // CAST_CHUNK_ROWS)]
    slots = stage_ref.shape[0]

    def copy(k):
        src, _, c, _ = chunks[k]
        return pltpu.make_async_copy(
            src.at[pl.ds(c * CAST_CHUNK_ROWS, CAST_CHUNK_ROWS), :],
            stage_ref.at[k % slots, :, pl.ds(0, src.shape[1])],
            sems.at[k % slots])

    def start(k):
        copy(k).start(priority=k % 2)

    def prime():
        for k in range(min(slots - 1, len(chunks))):
            start(k)

    def cast(k):
        def task():
            src, dst, c, fold = chunks[k]
            if k + slots - 1 < len(chunks):
                start(k + slots - 1)
            copy(k).wait()
            folded = fold(c, stage_ref.at[k % slots, :, pl.ds(0, src.shape[1])]) if fold else None
            if folded is not None:
                dst[c * CAST_CHUNK_ROWS:(c + 1) * CAST_CHUNK_ROWS, :] = folded.astype(BF16)
                return

            def cast_rows(r, carry):
                off = pl.multiple_of(r * CAST_ROWS_PER_ITER, CAST_ROWS_PER_ITER)
                dst[pl.ds(c * CAST_CHUNK_ROWS + off, CAST_ROWS_PER_ITER), :] = (
                    stage_ref[k % slots, pl.ds(off, CAST_ROWS_PER_ITER), 0:src.shape[1]]
                    .astype(BF16))
                return carry

            lax.fori_loop(0, CAST_CHUNK_ROWS // CAST_ROWS_PER_ITER, cast_rows, 0)
        return task

    return prime, [cast(k) for k in range(len(chunks))]


def _layer_kernel(sinks_ref, x_ref, pos_ref, freq_ref, mod_ref, n1_ref, n2_ref, nf_ref,
                  pscale_ref, wpool_ref, win_hbm, wout_hbm, wg_hbm, wu_hbm, wd_hbm,
                  o_ref,
                  win_ref, wout_ref, wg_ref, wu_ref, wd_ref, stage_ref, cast_sems,
                  cos_ref, sin_ref, h_ref, q_ref, kt_ref, v_ref, ext_ref, mix_ref, x1_ref, h2_ref,
                  act_ref, *, tiles_per_seq):
    t = x_ref.shape[1]
    s = pl.program_id(0)
    n_tiles = pl.num_programs(0) - 1
    tile_f = jnp.minimum(s, n_tiles - 1)
    first = (tile_f % tiles_per_seq) == 0
    slot_f = s % 2
    slot_b = 1 - slot_f
    seq_f = tile_f // tiles_per_seq
    seq_b = jnp.maximum(s - 1, 0) // tiles_per_seq

    def mod_row(k, seq):
        return mod_ref[k, pl.ds(seq, 1), :]

    lane =lax.broadcasted_iota(jnp.int32, (1, LANES), 1)
    low_lanes = lane < HEAD_DIM
    first_half = (lane % HEAD_DIM) < HALF

    def f_norm():
        x = x_ref[0]
        scale = n1_ref[...] * (1.0 + mod_row(1, seq_f))
        h_ref[...] = (x * _rms_scale(x) * scale + mod_row(0, seq_f)).astype(BF16)

    def in_proj(cols):
        return jnp.dot(h_ref[...], win_ref[:, cols], preferred_element_type=F32)

    def f_trig():
        ang = pos_ref[...].astype(F32) * freq_ref[...]
        cos, sin = jnp.cos(ang), jnp.sin(ang)
        per_row = LANES // HALF
        for m in range(per_row):
            c32 = cos[:, m * HALF:(m + 1) * HALF]
            s32 = sin[:, m * HALF:(m + 1) * HALF]
            rows = pl.ds(m, t // per_row, stride=per_row)
            cos_ref[rows, :] = jnp.concatenate([c32, c32, c32, c32], axis=-1)
            sin_ref[rows, :] = jnp.concatenate([-s32, s32, -s32, s32], axis=-1)

    def rope(tile):
        partner = jnp.where(first_half,
                            pltpu.roll(tile, LANES - HALF, 1),
                            pltpu.roll(tile, HALF, 1))
        return tile * cos_ref[...] + partner * sin_ref[...]

    def f_q():
        u = in_proj(slice(0, ATTN_WIDTH))
        q_scale = LOG2_E / math.sqrt(HEAD_DIM)
        for j in range(ATTN_WIDTH // LANES):
            cols = slice(j * LANES, (j + 1) * LANES)
            q_ref[:, cols] = (rope(u[:, cols]) * q_scale).astype(BF16)

    def f_kv():
        kt_ref[:, 0:WINDOW] = kt_ref[:, t:]
        v_ref[0:WINDOW, :] = v_ref[t:, :]
        u = in_proj(slice(ATTN_WIDTH, ATTN_WIDTH + 2 * KV_WIDTH))
        kt_ref[:, WINDOW:] = rope(u[:, :KV_WIDTH]).T.astype(BF16)
        vv = u[:, KV_WIDTH:]
        for g in range(N_KV_HEADS):
            vg = vv[:, g * HEAD_DIM:(g + 1) * HEAD_DIM]
            cols = slice(g * LANES, (g + 1) * LANES)
            v_ref[WINDOW:, cols] = jnp.concatenate([vg, vg], axis=-1).astype(BF16)

    def f_up():
        halo = ext_ref[t:, :]
        ext_ref[0:POOL_HALO, :] = jnp.where(first, jnp.zeros_like(halo), halo)
        ext_ref[POOL_HALO:, :] = in_proj(slice(ATTN_WIDTH + 2 * KV_WIDTH, IN_PROJ_WIDTH))

    def f_pool(gi, w):
        def task():
            cols = slice(gi * POOL_GROUP_WIDTH, (gi + 1) * POOL_GROUP_WIDTH)
            pos_in_seq = ((tile_f % tiles_per_seq) * t
                          + lax.broadcasted_iota(jnp.int32, (t, 1), 0))
            acc = ext_ref[:, cols]
            span = 1
            while span < w:
                acc = acc + pltpu.roll(acc, span, 0)
                span *= 2
            tok = ext_ref[POOL_HALO:, cols]
            total = acc[POOL_HALO:]
            count = jnp.minimum(pos_in_seq + 1, w).astype(F32)
            mix_ref[:, ATTN_WIDTH + gi * POOL_GROUP_WIDTH:
                    ATTN_WIDTH + (gi + 1) * POOL_GROUP_WIDTH] = (total / count - tok).astype(BF16)
        return task

    n_blocks = t // WINDOW
    chunks = N_HEADS // 2
    bodies = [(j, chunk) for j in range(n_blocks) for chunk in range(chunks)]
    kv_cache, logits, probs = {}, {}, {}

    def band_mask(j):
        qi = lax.broadcasted_iota(jnp.int32, (WINDOW, 2 * WINDOW), 0)
        kj = lax.broadcasted_iota(jnp.int32, (WINDOW, 2 * WINDOW), 1)
        rel = kj - WINDOW - qi
        band = (rel <= 0) & (rel > -WINDOW)
        if j == 0:
            band = band & (kj >= jnp.where(first, WINDOW, 0))
        return band

    def block_diag_kv(j, g):
        if (j, g) not in kv_cache:
            keys = slice(j * WINDOW, (j + 2) * WINDOW)
            ktg = kt_ref[g * HEAD_DIM:(g + 1) * HEAD_DIM, keys]
            zk = jnp.zeros_like(ktg)
            k_bd = jnp.concatenate([jnp.concatenate([ktg, zk], axis=1),
                                    jnp.concatenate([zk, ktg], axis=1)], axis=0)
            vd = v_ref[keys, g * LANES:(g + 1) * LANES]
            zv = jnp.zeros_like(vd)
            v_bd = jnp.concatenate([jnp.where(low_lanes, vd, zv),
                                    jnp.where(low_lanes, zv, vd)], axis=0)
            kv_cache[(j, g)] = (k_bd, v_bd)
        return kv_cache[(j, g)]

    def scores(j, chunk):
        k_bd, _ = block_diag_kv(j, chunk // (GROUP // 2))
        qc = q_ref[j * WINDOW:(j + 1) * WINDOW, chunk * LANES:(chunk + 1) * LANES]
        return jnp.dot(qc, k_bd, preferred_element_type=F32)

    def softmax(j, chunk, lg):
        mask = band_mask(j)
        ps, inv_den = [], []
        for hh in range(2):
            sink = sinks_ref[2 * chunk + hh] * LOG2_E
            l = jnp.where(mask, lg[:, hh * 2 * WINDOW:(hh + 1) * 2 * WINDOW], -jnp.inf)
            m = jnp.maximum(jnp.max(l, axis=-1, keepdims=True), sink)
            p = jnp.exp2(l - m)
            den = jnp.sum(p, axis=-1, keepdims=True) + jnp.exp2(sink - m)
            ps.append(p.astype(BF16))
            inv_den.append(1.0 / den)
        return jnp.concatenate(ps, axis=-1), jnp.where(low_lanes, inv_den[0], inv_den[1])

    def values(j, chunk, p, inv_den):
        _, v_bd = block_diag_kv(j, chunk // (GROUP // 2))
        pv = jnp.dot(p, v_bd, preferred_element_type=F32)
        mix_ref[j * WINDOW:(j + 1) * WINDOW, chunk * LANES:(chunk + 1) * LANES] = (
            (pv * inv_den).astype(BF16))

    def out_proj(j):
        rows = slice(j * WINDOW, (j + 1) * WINDOW)
        mixed = jnp.dot(mix_ref[rows, :], wout_ref[...], preferred_element_type=F32)
        x1_ref[slot_f, rows, :] = x_ref[0, rows, :] + mod_row(2, seq_f) * mixed

    def f_attn(step):
        def task():
            n = len(bodies)
            if step < n:
                logits[step] = scores(*bodies[step])
            if 0 <= step - 1 < n:
                probs[step - 1] = softmax(*bodies[step - 1], logits.pop(step - 1))
            done = step - ATTN_SKEW
            if 0 <= done < n:
                values(*bodies[done], *probs.pop(done))
            ready = step - ATTN_SKEW - OUT_PROJ_LAG
            if (0 <= ready < n and bodies[ready][1] == chunks - 1
                    and bodies[ready][0] < n_blocks - LATE_OUT_PROJ):
                out_proj(bodies[ready][0])
        return task

    late_out_proj = [functools.partial(out_proj, j)
                     for j in range(n_blocks - LATE_OUT_PROJ, n_blocks)]

    front = [f_norm, f_trig, f_q, f_kv, f_up]
    front += [f_pool(gi, w) for gi, w in enumerate(POOL_WINDOWS)]
    front += [f_attn(step) for step in range(len(bodies) + ATTN_SKEW + OUT_PROJ_LAG)]

    def b_norm():
        x1 = x1_ref[slot_b]
        scale = n2_ref[...] * (1.0 + mod_row(4, seq_b))
        h2_ref[...] = (x1 * _rms_scale(x1) * scale + mod_row(3, seq_b)).astype(BF16)

    def b_gate_up(n):
        def task():
            cols = slice(n * FF_CHUNK, (n + 1) * FF_CHUNK)
            g = jnp.dot(h2_ref[...], wg_ref[:, cols], preferred_element_type=F32)
            u = jnp.dot(h2_ref[...], wu_ref[:, cols], preferred_element_type=F32)
            act_ref[:, cols] = (g * jax.nn.sigmoid(g) * u).astype(BF16)
        return task

    def b_down(n):
        def task():
            cols = slice(n * DOWN_CHUNK, (n + 1) * DOWN_CHUNK)
            ff = jnp.dot(act_ref[...], wd_ref[:, cols], preferred_element_type=F32)
            o_ref[0, :, cols] = x1_ref[slot_b, :, cols] + mod_row(5, seq_b)[:, cols] * ff
        return task

    def b_final():
        x2 = o_ref[0]
        o_ref[0] = x2 * _rms_scale(x2) * nf_ref[...]

    d_ff = wg_ref.shape[1]
    back = [b_norm]
    back += [b_gate_up(n) for n in range(d_ff // FF_CHUNK)]
    back += [b_down(n) for n in range(D_MODEL // DOWN_CHUNK)]
    back += [b_final]

    def run(tasks):
        kv_cache.clear(), logits.clear(), probs.clear()
        for task in tasks:
            task()

    @pl.when(s == 0)
    def _():
        kt_ref[:, t:] = jnp.zeros((KV_WIDTH, WINDOW), BF16)
        v_ref[t:, :] = jnp.zeros((WINDOW, 2 * KV_WIDTH), BF16)
        ext_ref[t:, :] = jnp.zeros((POOL_HALO, POOL_WIDTH), F32)

        def fold_pool(c, rows_ref):
            g = c - ATTN_WIDTH // CAST_CHUNK_ROWS
            if g < 0:
                return None
            cols = slice(g * POOL_GROUP_WIDTH, (g + 1) * POOL_GROUP_WIDTH)
            return _dot_f32(wpool_ref[cols, :] * pscale_ref[:, cols], rows_ref[...])

        prime, casts = _weight_cast_tasks(
            [(win_hbm, win_ref, None), (wout_hbm, wout_ref, fold_pool),
             (wg_hbm, wg_ref, None), (wu_hbm, wu_ref, None), (wd_hbm, wd_ref, None)],
            stage_ref, cast_sems)
        front_casts = (win_hbm.shape[0] + wout_hbm.shape[0]) // CAST_CHUNK_ROWS
        prime()
        run(casts[:front_casts])
        run(_interleave(front, casts[front_casts:]) + late_out_proj)

    @pl.when(s > 0)
    def _():
        order = [task for task in _interleave(front, back) if task not in back[-2:]]
        run(order + [back[-2], late_out_proj[0], back[-1], late_out_proj[1]])


def _layer(x, mod, positions, sinks, norm1, norm2, norm_f, w_in, w_pool, pool_scale, w_out,
           w_gate, w_up, w_down):
    b, s, d = x.shape
    t = TOKEN_TILE
    d_ff = w_gate.shape[1]
    assert d == D_MODEL and s % t == 0 and t % WINDOW == 0 and d_ff % FF_CHUNK == 0
    assert d_ff % CAST_CHUNK_ROWS == 0 and d % CAST_CHUNK_ROWS == 0
    tiles_per_seq = s // t
    n_tiles = b * tiles_per_seq

    def front_tile(step):
        return jnp.minimum(step, n_tiles - 1)

    def back_tile(step):
        return jnp.maximum(step - 1, 0)

    def tok_map(tile_of):
        return lambda step, *_: (tile_of(step) // tiles_per_seq, tile_of(step) % tiles_per_seq, 0)

    per_row = LANES // HALF
    inv_freq = ROPE_THETA ** (-jnp.arange(HALF, dtype=F32) * (2.0 / HEAD_DIM))
    freq = jnp.tile(inv_freq, per_row).reshape(1, LANES)
    pos_rep = jnp.repeat(positions.reshape(b * s // per_row, per_row), HALF, axis=1)
    grid_spec = pltpu.PrefetchScalarGridSpec(
        num_scalar_prefetch=1,
        grid=(n_tiles + 1,),
        in_specs=[
            pl.BlockSpec((1, t, d), tok_map(front_tile)),
            pl.BlockSpec((t // per_row, LANES), lambda step, *_: (front_tile(step), 0)),
            _const_spec((1, LANES)),
            _const_spec((N_MOD, b, d)),
            _const_spec((1, d)), _const_spec((1, d)), _const_spec((1, d)),
            _const_spec((1, POOL_WIDTH)),
            _const_spec((POOL_WIDTH, POOL_GROUP_WIDTH)),
        ] + [pl.BlockSpec(memory_space=pl.ANY)] * 5,
        out_specs=pl.BlockSpec((1, t, d), tok_map(back_tile)),
        scratch_shapes=[
            pltpu.VMEM((d, IN_PROJ_WIDTH), BF16),
            pltpu.VMEM((ATTN_WIDTH + POOL_WIDTH, d), BF16),
            pltpu.VMEM((d, d_ff), BF16),
            pltpu.VMEM((d, d_ff), BF16),
            pltpu.VMEM((d_ff, d), BF16),
            pltpu.VMEM((CAST_SLOTS, CAST_CHUNK_ROWS, max(d_ff, IN_PROJ_WIDTH)), F32),
            pltpu.SemaphoreType.DMA((CAST_SLOTS,)),
            pltpu.VMEM((t, LANES), F32),
            pltpu.VMEM((t, LANES), F32),
            pltpu.VMEM((t, d), BF16),
            pltpu.VMEM((t, ATTN_WIDTH), BF16),
            pltpu.VMEM((KV_WIDTH, t + WINDOW), BF16),
            pltpu.VMEM((t + WINDOW, 2 * KV_WIDTH), BF16),
            pltpu.VMEM((t + POOL_HALO, POOL_WIDTH), F32),
            pltpu.VMEM((t, ATTN_WIDTH + POOL_WIDTH), BF16),
            pltpu.VMEM((2, t, d), F32),
            pltpu.VMEM((t, d), BF16),
            pltpu.VMEM((t, d_ff), BF16),
        ],
    )
    return pl.pallas_call(
        functools.partial(_layer_kernel, tiles_per_seq=tiles_per_seq),
        grid_spec=grid_spec,
        out_shape=jax.ShapeDtypeStruct((b, s, d), F32),
        compiler_params=pltpu.CompilerParams(
            dimension_semantics=("arbitrary",), vmem_limit_bytes=VMEM_LIMIT_BYTES),
        name="layer",
    )(sinks, x, pos_rep, freq, mod, norm1.reshape(1, d), norm2.reshape(1, d), norm_f.reshape(1, d),
      pool_scale.reshape(1, -1), w_pool.reshape(POOL_WIDTH, POOL_GROUP_WIDTH), w_in, w_out,
      w_gate, w_up, w_down)


def kernel(x, c, positions, w_ada, b_ada, norm1, w_in, sinks, w_pool, pool_scale,
           w_out, norm2, w_gate, w_up, w_down, norm_f):
    mod = _adaln_mod(c, w_ada, b_ada)
    return _layer(x, mod, positions, sinks, norm1, norm2, norm_f, w_in, w_pool, pool_scale,
                  w_out, w_gate, w_up, w_down)
```

```python
import functools
import math

import jax
import jax.numpy as jnp
from jax import lax
from jax.experimental import pallas as pl
from jax.experimental.pallas import tpu as pltpu

F32 = jnp.float32
BF16 = jnp.bfloat16

D_MODEL = 1024
HEAD_DIM = 64
N_HEADS = 8
N_KV_HEADS = 2
GROUP = N_HEADS // N_KV_HEADS
ATTN_WIDTH = N_HEADS * HEAD_DIM
KV_WIDTH = N_KV_HEADS * HEAD_DIM
POOL_WINDOWS = (2, 4, 8, 16)
POOL_GROUP_WIDTH = 128
POOL_WIDTH = POOL_GROUP_WIDTH * len(POOL_WINDOWS)
IN_PROJ_WIDTH = ATTN_WIDTH + 2 * KV_WIDTH + POOL_WIDTH
WINDOW = 128
ROPE_THETA = 10000.0
N_MOD = 6
RMS_EPS = 1e-6
LOG2_E = math.log2(math.e)
HALF = HEAD_DIM // 2

LANES = 128
POOL_HALO = 16
VMEM_LIMIT_BYTES = 56 * 1024 * 1024

TOKEN_TILE = 512
FF_CHUNK = 256
DOWN_CHUNK = 256
ATTN_SKEW = 3
CAST_CHUNK_ROWS = 128
CAST_SLOTS = 4
CAST_ROWS_PER_ITER = 16
OUT_PROJ_LAG = 2
LATE_OUT_PROJ = 2
BACK_SPAN = 0.9


def _const_spec(shape):
    zeros = (0,) * len(shape)
    return pl.BlockSpec(shape, lambda *_: zeros, pipeline_mode=pl.Buffered(1))


def _rms_scale(x):
    return lax.rsqrt(jnp.mean(x * x, axis=-1, keepdims=True) + RMS_EPS)


def _interleave(front, back):
    keyed = [((k + 0.5) / len(front), 0, k, f) for k, f in enumerate(front)]
    keyed += [(BACK_SPAN * (k + 0.5) / len(back), 1, k, f) for k, f in enumerate(back)]
    return [f for *_, f in sorted(keyed, key=lambda e: e[:3])]


def _mod_kernel(ct_ref, w_ref, b_ref, o_ref):
    ct = ct_ref[...]
    sc = ct * jax.nn.sigmoid(ct)
    w = w_ref[...]
    for r in range(ct.shape[1]):
        o_ref[0, r:r + 1, :] = jnp.sum(w * sc[:, r:r + 1], axis=0, keepdims=True) + b_ref[...]


def _adaln_mod(c, w_ada, b_ada):
    b, d = c.shape
    return pl.pallas_call(
        _mod_kernel,
        grid=(N_MOD,),
        in_specs=[
            pl.BlockSpec((d, b), lambda j: (0, 0)),
            pl.BlockSpec((d, d), lambda j: (0, j)),
            pl.BlockSpec((1, d), lambda j: (0, j)),
        ],
        out_specs=pl.BlockSpec((1, b, d), lambda j: (j, 0, 0)),
        out_shape=jax.ShapeDtypeStruct((N_MOD, b, d), F32),
        compiler_params=pltpu.CompilerParams(dimension_semantics=("arbitrary",)),
        name="adaln_mod",
    )(c.T, w_ada, b_ada.reshape(1, -1))


def _dot_f32(a, b):
    ah = a.astype(BF16)
    al = (a - ah.astype(F32)).astype(BF16)
    bh = b.astype(BF16)
    bl = (b - bh.astype(F32)).astype(BF16)
    dot = functools.partial(jnp.dot, preferred_element_type=F32)
    return dot(ah, bh) + dot(ah, bl) + dot(al, bh)


def _weight_cast_tasks(triples, stage_ref, sems):
    chunks = [(src, dst, c, fold) for src, dst, fold in triples
              for c in range(src.shape[0] // CAST_CHUNK_ROWS)]
    slots = stage_ref.shape[0]

    def copy(k):
        src, _, c, _ = chunks[k]
        return pltpu.make_async_copy(
            src.at[pl.ds(c * CAST_CHUNK_ROWS, CAST_CHUNK_ROWS), :],
            stage_ref.at[k % slots, :, pl.ds(0, src.shape[1])],
            sems.at[k % slots])

    def start(k):
        copy(k).start(priority=k % 2)

    def prime():
        for k in range(min(slots - 1, len(chunks))):
            start(k)

    def cast(k):
        def task():
            src, dst, c, fold = chunks[k]
            if k + slots - 1 < len(chunks):
                start(k + slots - 1)
            copy(k).wait()
            folded = fold(c, stage_ref.at[k % slots, :, pl.ds(0, src.shape[1])]) if fold else None
            if folded is not None:
                dst[c * CAST_CHUNK_ROWS:(c + 1) * CAST_CHUNK_ROWS, :] = folded.astype(BF16)
                return

            def cast_rows(r, carry):
                off = pl.multiple_of(r * CAST_ROWS_PER_ITER, CAST_ROWS_PER_ITER)
                dst[pl.ds(c * CAST_CHUNK_ROWS + off, CAST_ROWS_PER_ITER), :] = (
                    stage_ref[k % slots, pl.ds(off, CAST_ROWS_PER_ITER), 0:src.shape[1]]
                    .astype(BF16))
                return carry

            lax.fori_loop(0, CAST_CHUNK_ROWS // CAST_ROWS_PER_ITER, cast_rows, 0)
        return task

    return prime, [cast(k) for k in range(len(chunks))]


def _layer_kernel(sinks_ref, x_ref, pos_ref, freq_ref, mod_ref, n1_ref, n2_ref, nf_ref,
                  pscale_ref, wpool_ref, win_hbm, wout_hbm, wg_hbm, wu_hbm, wd_hbm,
                  o_ref,
                  win_ref, wout_ref, wg_ref, wu_ref, wd_ref, stage_ref, cast_sems,
                  cos_ref, sin_ref, h_ref, q_ref, kt_ref, v_ref, ext_ref, mix_ref, x1_ref, h2_ref,
                  act_ref, *, tiles_per_seq):
    t = x_ref.shape[1]
    s = pl.program_id(0)
    n_tiles = pl.num_programs(0) - 1
    tile_f = jnp.minimum(s, n_tiles - 1)
    first = (tile_f % tiles_per_seq) == 0
    slot_f = s % 2
    slot_b = 1 - slot_f
    seq_f = tile_f // tiles_per_seq
    seq_b = jnp.maximum(s - 1, 0) // tiles_per_seq

    def mod_row(k, seq):
        return mod_ref[k, pl.ds(seq, 1), :]

    lane =lax.broadcasted_iota(jnp.int32, (1, LANES), 1)
    low_lanes = lane < HEAD_DIM
    first_half = (lane % HEAD_DIM) < HALF

    def f_norm():
        x = x_ref[0]
        scale = n1_ref[...] * (1.0 + mod_row(1, seq_f))
        h_ref[...] = (x * _rms_scale(x) * scale + mod_row(0, seq_f)).astype(BF16)

    def in_proj(cols):
        return jnp.dot(h_ref[...], win_ref[:, cols], preferred_element_type=F32)

    def f_trig():
        per_row = LANES // HALF
        pos = pos_ref[...].astype(F32)
        pos = jnp.concatenate([jnp.broadcast_to(pos[:, m:m + 1], (t // per_row, HALF))
                               for m in range(per_row)], axis=-1)
        ang = pos * freq_ref[...]
        cos, sin = jnp.cos(ang), jnp.sin(ang)
        for m in range(per_row):
            c32 = cos[:, m * HALF:(m + 1) * HALF]
            s32 = sin[:, m * HALF:(m + 1) * HALF]
            rows = pl.ds(m, t // per_row, stride=per_row)
            cos_ref[rows, :] = jnp.concatenate([c32, c32, c32, c32], axis=-1)
            sin_ref[rows, :] = jnp.concatenate([-s32, s32, -s32, s32], axis=-1)

    def rope(tile):
        partner = jnp.where(first_half,
                            pltpu.roll(tile, LANES - HALF, 1),
                            pltpu.roll(tile, HALF, 1))
        return tile * cos_ref[...] + partner * sin_ref[...]

    def f_q():
        u = in_proj(slice(0, ATTN_WIDTH))
        q_scale = LOG2_E / math.sqrt(HEAD_DIM)
        for j in range(ATTN_WIDTH // LANES):
            cols = slice(j * LANES, (j + 1) * LANES)
            q_ref[:, cols] = (rope(u[:, cols]) * q_scale).astype(BF16)

    def f_kv():
        kt_ref[:, 0:WINDOW] = kt_ref[:, t:]
        v_ref[0:WINDOW, :] = v_ref[t:, :]
        u = in_proj(slice(ATTN_WIDTH, ATTN_WIDTH + 2 * KV_WIDTH))
        kt_ref[:, WINDOW:] = rope(u[:, :KV_WIDTH]).T.astype(BF16)
        vv = u[:, KV_WIDTH:]
        for g in range(N_KV_HEADS):
            vg = vv[:, g * HEAD_DIM:(g + 1) * HEAD_DIM]
            cols = slice(g * LANES, (g + 1) * LANES)
            v_ref[WINDOW:, cols] = jnp.concatenate([vg, vg], axis=-1).astype(BF16)

    def f_up():
        halo = ext_ref[t:, :]
        ext_ref[0:POOL_HALO, :] = jnp.where(first, jnp.zeros_like(halo), halo)
        ext_ref[POOL_HALO:, :] = in_proj(slice(ATTN_WIDTH + 2 * KV_WIDTH, IN_PROJ_WIDTH))

    def f_pool(gi, w):
        def task():
            cols = slice(gi * POOL_GROUP_WIDTH, (gi + 1) * POOL_GROUP_WIDTH)
            pos_in_seq = ((tile_f % tiles_per_seq) * t
                          + lax.broadcasted_iota(jnp.int32, (t, 1), 0))
            acc = ext_ref[:, cols]
            span = 1
            while span < w:
                acc = acc + pltpu.roll(acc, span, 0)
                span *= 2
            tok = ext_ref[POOL_HALO:, cols]
            total = acc[POOL_HALO:]
            count = jnp.minimum(pos_in_seq + 1, w).astype(F32)
            mix_ref[:, ATTN_WIDTH + gi * POOL_GROUP_WIDTH:
                    ATTN_WIDTH + (gi + 1) * POOL_GROUP_WIDTH] = (total / count - tok).astype(BF16)
        return task

    n_blocks = t // WINDOW
    chunks = N_HEADS // 2
    bodies = [(j, chunk) for j in range(n_blocks) for chunk in range(chunks)]
    kv_cache, logits, probs = {}, {}, {}

    def band_mask(j):
        qi = lax.broadcasted_iota(jnp.int32, (WINDOW, 2 * WINDOW), 0)
        kj = lax.broadcasted_iota(jnp.int32, (WINDOW, 2 * WINDOW), 1)
        rel = kj - WINDOW - qi
        band = (rel <= 0) & (rel > -WINDOW)
        if j == 0:
            band = band & (kj >= jnp.where(first, WINDOW, 0))
        return band

    def block_diag_kv(j, g):
        if (j, g) not in kv_cache:
            keys = slice(j * WINDOW, (j + 2) * WINDOW)
            ktg = kt_ref[g * HEAD_DIM:(g + 1) * HEAD_DIM, keys]
            zk = jnp.zeros_like(ktg)
            k_bd = jnp.concatenate([jnp.concatenate([ktg, zk], axis=1),
                                    jnp.concatenate([zk, ktg], axis=1)], axis=0)
            vd = v_ref[keys, g * LANES:(g + 1) * LANES]
            zv = jnp.zeros_like(vd)
            v_bd = jnp.concatenate([jnp.where(low_lanes, vd, zv),
                                    jnp.where(low_lanes, zv, vd)], axis=0)
            kv_cache[(j, g)] = (k_bd, v_bd)
        return kv_cache[(j, g)]

    def scores(j, chunk):
        k_bd, _ = block_diag_kv(j, chunk // (GROUP // 2))
        qc = q_ref[j * WINDOW:(j + 1) * WINDOW, chunk * LANES:(chunk + 1) * LANES]
        return jnp.dot(qc, k_bd, preferred_element_type=F32)

    def softmax(j, chunk, lg):
        mask = band_mask(j)
        ps, inv_den = [], []
        for hh in range(2):
            sink = sinks_ref[2 * chunk + hh] * LOG2_E
            l = jnp.where(mask, lg[:, hh * 2 * WINDOW:(hh + 1) * 2 * WINDOW], -jnp.inf)
            m = jnp.maximum(jnp.max(l, axis=-1, keepdims=True), sink)
            p = jnp.exp2(l - m)
            den = jnp.sum(p, axis=-1, keepdims=True) + jnp.exp2(sink - m)
            ps.append(p.astype(BF16))
            inv_den.append(1.0 / den)
        return jnp.concatenate(ps, axis=-1), jnp.where(low_lanes, inv_den[0], inv_den[1])

    def values(j, chunk, p, inv_den):
        _, v_bd = block_diag_kv(j, chunk // (GROUP // 2))
        pv = jnp.dot(p, v_bd, preferred_element_type=F32)
        mix_ref[j * WINDOW:(j + 1) * WINDOW, chunk * LANES:(chunk + 1) * LANES] = (
            (pv * inv_den).astype(BF16))

    def out_proj(j):
        rows = slice(j * WINDOW, (j + 1) * WINDOW)
        mixed = jnp.dot(mix_ref[rows, :], wout_ref[...], preferred_element_type=F32)
        x1_ref[slot_f, rows, :] = x_ref[0, rows, :] + mod_row(2, seq_f) * mixed

    def f_attn(step):
        def task():
            n = len(bodies)
            if step < n:
                logits[step] = scores(*bodies[step])
            if 0 <= step - 1 < n:
                probs[step - 1] = softmax(*bodies[step - 1], logits.pop(step - 1))
            done = step - ATTN_SKEW
            if 0 <= done < n:
                values(*bodies[done], *probs.pop(done))
            ready = step - ATTN_SKEW - OUT_PROJ_LAG
            if (0 <= ready < n and bodies[ready][1] == chunks - 1
                    and bodies[ready][0] < n_blocks - LATE_OUT_PROJ):
                out_proj(bodies[ready][0])
        return task

    late_out_proj = [functools.partial(out_proj, j)
                     for j in range(n_blocks - LATE_OUT_PROJ, n_blocks)]

    front = [f_norm, f_trig, f_q, f_kv, f_up]
    front += [f_pool(gi, w) for gi, w in enumerate(POOL_WINDOWS)]
    front += [f_attn(step) for step in range(len(bodies) + ATTN_SKEW + OUT_PROJ_LAG)]

    def b_norm():
        x1 = x1_ref[slot_b]
        scale = n2_ref[...] * (1.0 + mod_row(4, seq_b))
        h2_ref[...] = (x1 * _rms_scale(x1) * scale + mod_row(3, seq_b)).astype(BF16)

    def b_gate_up(n):
        def task():
            cols = slice(n * FF_CHUNK, (n + 1) * FF_CHUNK)
            g = jnp.dot(h2_ref[...], wg_ref[:, cols], preferred_element_type=F32)
            u = jnp.dot(h2_ref[...], wu_ref[:, cols], preferred_element_type=F32)
            act_ref[:, cols] = (g * jax.nn.sigmoid(g) * u).astype(BF16)
        return task

    def b_down(n):
        def task():
            cols = slice(n * DOWN_CHUNK, (n + 1) * DOWN_CHUNK)
            ff = jnp.dot(act_ref[...], wd_ref[:, cols], preferred_element_type=F32)
            o_ref[0, :, cols] = x1_ref[slot_b, :, cols] + mod_row(5, seq_b)[:, cols] * ff
        return task

    def b_final():
        x2 = o_ref[0]
        o_ref[0] = x2 * _rms_scale(x2) * nf_ref[...]

    d_ff = wg_ref.shape[1]
    back = [b_norm]
    back += [b_gate_up(n) for n in range(d_ff // FF_CHUNK)]
    back += [b_down(n) for n in range(D_MODEL // DOWN_CHUNK)]
    back += [b_final]

    def run(tasks):
        kv_cache.clear(), logits.clear(), probs.clear()
        for task in tasks:
            task()

    @pl.when(s == 0)
    def _():
        kt_ref[:, t:] = jnp.zeros((KV_WIDTH, WINDOW), BF16)
        v_ref[t:, :] = jnp.zeros((WINDOW, 2 * KV_WIDTH), BF16)
        ext_ref[t:, :] = jnp.zeros((POOL_HALO, POOL_WIDTH), F32)

        def fold_pool(c, rows_ref):
            g = c - ATTN_WIDTH // CAST_CHUNK_ROWS
            if g < 0:
                return None
            cols = slice(g * POOL_GROUP_WIDTH, (g + 1) * POOL_GROUP_WIDTH)
            return _dot_f32(wpool_ref[g] * pscale_ref[:, cols], rows_ref[...])

        prime, casts = _weight_cast_tasks(
            [(win_hbm, win_ref, None), (wout_hbm, wout_ref, fold_pool),
             (wg_hbm, wg_ref, None), (wu_hbm, wu_ref, None), (wd_hbm, wd_ref, None)],
            stage_ref, cast_sems)
        front_casts = (win_hbm.shape[0] + wout_hbm.shape[0]) // CAST_CHUNK_ROWS
        prime()
        run(casts[:front_casts])
        run(_interleave(front, casts[front_casts:]) + late_out_proj)

    @pl.when(s > 0)
    def _():
        order = [task for task in _interleave(front, back) if task not in back[-2:]]
        run(order + [back[-2], late_out_proj[0], back[-1], late_out_proj[1]])


def _layer(x, mod, positions, sinks, norm1, norm2, norm_f, w_in, w_pool, pool_scale, w_out,
           w_gate, w_up, w_down):
    b, s, d = x.shape
    t = TOKEN_TILE
    d_ff = w_gate.shape[1]
    assert d == D_MODEL and s % t == 0 and t % WINDOW == 0 and d_ff % FF_CHUNK == 0
    assert d_ff % CAST_CHUNK_ROWS == 0 and d % CAST_CHUNK_ROWS == 0
    tiles_per_seq = s // t
    n_tiles = b * tiles_per_seq

    def front_tile(step):
        return jnp.minimum(step, n_tiles - 1)

    def back_tile(step):
        return jnp.maximum(step - 1, 0)

    def tok_map(tile_of):
        return lambda step, *_: (tile_of(step) // tiles_per_seq, tile_of(step) % tiles_per_seq, 0)

    per_row = LANES // HALF
    inv_freq = ROPE_THETA ** (-jnp.arange(HALF, dtype=F32) * (2.0 / HEAD_DIM))
    freq = jnp.tile(inv_freq, per_row).reshape(1, LANES)
    pos_rows = positions.reshape(b * s // per_row, per_row)
    grid_spec = pltpu.PrefetchScalarGridSpec(
        num_scalar_prefetch=1,
        grid=(n_tiles + 1,),
        in_specs=[
            pl.BlockSpec((1, t, d), tok_map(front_tile)),
            pl.BlockSpec((t // per_row, per_row), lambda step, *_: (front_tile(step), 0)),
            _const_spec((1, LANES)),
            _const_spec((N_MOD, b, d)),
            _const_spec((1, d)), _const_spec((1, d)), _const_spec((1, d)),
            _const_spec((1, POOL_WIDTH)),
            _const_spec(w_pool.shape),
        ] + [pl.BlockSpec(memory_space=pl.ANY)] * 5,
        out_specs=pl.BlockSpec((1, t, d), tok_map(back_tile)),
        scratch_shapes=[
            pltpu.VMEM((d, IN_PROJ_WIDTH), BF16),
            pltpu.VMEM((ATTN_WIDTH + POOL_WIDTH, d), BF16),
            pltpu.VMEM((d, d_ff), BF16),
            pltpu.VMEM((d, d_ff), BF16),
            pltpu.VMEM((d_ff, d), BF16),
            pltpu.VMEM((CAST_SLOTS, CAST_CHUNK_ROWS, max(d_ff, IN_PROJ_WIDTH)), F32),
            pltpu.SemaphoreType.DMA((CAST_SLOTS,)),
            pltpu.VMEM((t, LANES), F32),
            pltpu.VMEM((t, LANES), F32),
            pltpu.VMEM((t, d), BF16),
            pltpu.VMEM((t, ATTN_WIDTH), BF16),
            pltpu.VMEM((KV_WIDTH, t + WINDOW), BF16),
            pltpu.VMEM((t + WINDOW, 2 * KV_WIDTH), BF16),
            pltpu.VMEM((t + POOL_HALO, POOL_WIDTH), F32),
            pltpu.VMEM((t, ATTN_WIDTH + POOL_WIDTH), BF16),
            pltpu.VMEM((2, t, d), F32),
            pltpu.VMEM((t, d), BF16),
            pltpu.VMEM((t, d_ff), BF16),
        ],
    )
    return pl.pallas_call(
        functools.partial(_layer_kernel, tiles_per_seq=tiles_per_seq),
        grid_spec=grid_spec,
        out_shape=jax.ShapeDtypeStruct((b, s, d), F32),
        compiler_params=pltpu.CompilerParams(
            dimension_semantics=("arbitrary",), vmem_limit_bytes=VMEM_LIMIT_BYTES),
        name="layer",
    )(sinks, x, pos_rows, freq, mod, norm1.reshape(1, d), norm2.reshape(1, d), norm_f.reshape(1, d),
      pool_scale.reshape(1, -1), w_pool, w_in, w_out,
      w_gate, w_up, w_down)


def kernel(x, c, positions, w_ada, b_ada, norm1, w_in, sinks, w_pool, pool_scale,
           w_out, norm2, w_gate, w_up, w_down, norm_f):
    mod = _adaln_mod(c, w_ada, b_ada)
    return _layer(x, mod, positions, sinks, norm1, norm2, norm_f, w_in, w_pool, pool_scale,
                  w_out, w_gate, w_up, w_down)
```

```python
import functools
import math

import jax
import jax.numpy as jnp
from jax import lax
from jax.experimental import pallas as pl
from jax.experimental.pallas import tpu as pltpu

F32 = jnp.float32
BF16 = jnp.bfloat16

D_MODEL = 1024
HEAD_DIM = 64
N_HEADS = 8
N_KV_HEADS = 2
GROUP = N_HEADS // N_KV_HEADS
ATTN_WIDTH = N_HEADS * HEAD_DIM
KV_WIDTH = N_KV_HEADS * HEAD_DIM
POOL_WINDOWS = (2, 4, 8, 16)
POOL_GROUP_WIDTH = 128
POOL_WIDTH = POOL_GROUP_WIDTH * len(POOL_WINDOWS)
IN_PROJ_WIDTH = ATTN_WIDTH + 2 * KV_WIDTH + POOL_WIDTH
WINDOW = 128
ROPE_THETA = 10000.0
N_MOD = 6
RMS_EPS = 1e-6
LOG2_E = math.log2(math.e)
HALF = HEAD_DIM // 2

LANES = 128
POOL_HALO = 16
VMEM_LIMIT_BYTES = 56 * 1024 * 1024

TOKEN_TILE = 512
FF_CHUNK = 256
DOWN_CHUNK = 256
ATTN_SKEW = 3
CAST_CHUNK_ROWS = 128
CAST_SLOTS = 4
CAST_ROWS_PER_ITER = 16
OUT_PROJ_LAG = 2
LATE_OUT_PROJ = 2
BACK_SPAN = 0.9


def _const_spec(shape):
    zeros = (0,) * len(shape)
    return pl.BlockSpec(shape, lambda *_: zeros, pipeline_mode=pl.Buffered(1))


def _rms_scale(x):
    return lax.rsqrt(jnp.mean(x * x, axis=-1, keepdims=True) + RMS_EPS)


def _interleave(front, back):
    keyed = [((k + 0.5) / len(front), 0, k, f) for k, f in enumerate(front)]
    keyed += [(BACK_SPAN * (k + 0.5) / len(back), 1, k, f) for k, f in enumerate(back)]
    return [f for *_, f in sorted(keyed, key=lambda e: e[:3])]


def _mod_kernel(ct_ref, w_ref, b_ref, o_ref):
    ct = ct_ref[...]
    sc = ct * jax.nn.sigmoid(ct)
    w = w_ref[...]
    for r in range(ct.shape[1]):
        o_ref[0, r:r + 1, :] = jnp.sum(w * sc[:, r:r + 1], axis=0, keepdims=True) + b_ref[...]


def _adaln_mod(c, w_ada, b_ada):
    b, d = c.shape
    return pl.pallas_call(
        _mod_kernel,
        grid=(N_MOD,),
        in_specs=[
            pl.BlockSpec((d, b), lambda j: (0, 0)),
            pl.BlockSpec((d, d), lambda j: (0, j)),
            pl.BlockSpec((1, d), lambda j: (0, j)),
        ],
        out_specs=pl.BlockSpec((1, b, d), lambda j: (j, 0, 0)),
        out_shape=jax.ShapeDtypeStruct((N_MOD, b, d), F32),
        compiler_params=pltpu.CompilerParams(dimension_semantics=("arbitrary",)),
        name="adaln_mod",
    )(c.T, w_ada, b_ada.reshape(1, -1))


def _dot_f32(a, b):
    ah = a.astype(BF16)
    al = (a - ah.astype(F32)).astype(BF16)
    bh = b.astype(BF16)
    bl = (b - bh.astype(F32)).astype(BF16)
    dot = functools.partial(jnp.dot, preferred_element_type=F32)
    return dot(ah, bh) + dot(ah, bl) + dot(al, bh)


def _weight_cast_tasks(triples, stage_ref, sems):
    chunks = [(src, dst, c, fold) for src, dst, fold in triples
              for c in range(src.shape[0] // CAST_CHUNK_ROWS)]
    slots = stage_ref.shape[0]

    def copy(k):
        src, _, c, _ = chunks[k]
        return pltpu.make_async_copy(
            src.at[pl.ds(c * CAST_CHUNK_ROWS, CAST_CHUNK_ROWS), :],
            stage_ref.at[k % slots, :, pl.ds(0, src.shape[1])],
            sems.at[k % slots])

    def start(k):
        copy(k).start(priority=k % 2)

    def prime():
        for k in range(min(slots - 1, len(chunks))):
            start(k)

    def cast(k):
        def task():
            src, dst, c, fold = chunks[k]
            if k + slots - 1 < len(chunks):
                start(k + slots - 1)
            copy(k).wait()
            folded = fold(c, stage_ref.at[k % slots, :, pl.ds(0, src.shape[1])]) if fold else None
            if folded is not None:
                dst[c * CAST_CHUNK_ROWS:(c + 1) * CAST_CHUNK_ROWS, :] = folded.astype(BF16)
                return

            def cast_rows(r, carry):
                off = pl.multiple_of(r * CAST_ROWS_PER_ITER, CAST_ROWS_PER_ITER)
                dst[pl.ds(c * CAST_CHUNK_ROWS + off, CAST_ROWS_PER_ITER), :] = (
                    stage_ref[k % slots, pl.ds(off, CAST_ROWS_PER_ITER), 0:src.shape[1]]
                    .astype(BF16))
                return carry

            lax.fori_loop(0, CAST_CHUNK_ROWS // CAST_ROWS_PER_ITER, cast_rows, 0)
        return task

    return prime, [cast(k) for k in range(len(chunks))]


def _layer_kernel(sinks_ref, x_ref, pos_ref, freq_ref, mod_ref, n1_ref, n2_ref, nf_ref,
                  pscale_ref, wpool_ref, win_hbm, wout_hbm, wg_hbm, wu_hbm, wd_hbm,
                  o_ref,
                  win_ref, wout_ref, wg_ref, wu_ref, wd_ref, stage_ref, cast_sems,
                  cos_ref, sin_ref, h_ref, q_ref, kt_ref, v_ref, ext_ref, mix_ref, x1_ref, h2_ref,
                  act_ref, *, tiles_per_seq):
    t = x_ref.shape[1]
    s = pl.program_id(0)
    n_tiles = pl.num_programs(0) - 1
    tile_f = jnp.minimum(s, n_tiles - 1)
    first = (tile_f % tiles_per_seq) == 0
    slot_f = s % 2
    slot_b = 1 - slot_f
    seq_f = tile_f // tiles_per_seq
    seq_b = jnp.maximum(s - 1, 0) // tiles_per_seq

    def mod_row(k, seq):
        return mod_ref[k, pl.ds(seq, 1), :]

    lane =lax.broadcasted_iota(jnp.int32, (1, LANES), 1)
    low_lanes = lane < HEAD_DIM
    first_half = (lane % HEAD_DIM) < HALF

    def f_norm():
        x = x_ref[0]
        scale = n1_ref[...] * (1.0 + mod_row(1, seq_f))
        h_ref[...] = (x * _rms_scale(x) * scale + mod_row(0, seq_f)).astype(BF16)

    def in_proj(cols):
        return jnp.dot(h_ref[...], win_ref[:, cols], preferred_element_type=F32)

    def f_trig():
        groups = t // LANES
        pos = pos_ref[0].astype(F32)
        pos = jnp.concatenate([pos, jnp.zeros((8 - groups, LANES), F32)], axis=0).T
        pos = jnp.concatenate([jnp.broadcast_to(pos[:, m:m + 1], (LANES, HALF))
                               for m in range(groups)], axis=-1)
        ang = pos * freq_ref[...]
        cos, sin = jnp.cos(ang), jnp.sin(ang)
        for m in range(groups):
            c32 = cos[:, m * HALF:(m + 1) * HALF]
            s32 = sin[:, m * HALF:(m + 1) * HALF]
            rows = slice(m * LANES, (m + 1) * LANES)
            cos_ref[rows, :] = jnp.concatenate([c32, c32, c32, c32], axis=-1)
            sin_ref[rows, :] = jnp.concatenate([-s32, s32, -s32, s32], axis=-1)

    def rope(tile):
        partner = jnp.where(first_half,
                            pltpu.roll(tile, LANES - HALF, 1),
                            pltpu.roll(tile, HALF, 1))
        return tile * cos_ref[...] + partner * sin_ref[...]

    def f_q():
        u = in_proj(slice(0, ATTN_WIDTH))
        q_scale = LOG2_E / math.sqrt(HEAD_DIM)
        for j in range(ATTN_WIDTH // LANES):
            cols = slice(j * LANES, (j + 1) * LANES)
            q_ref[:, cols] = (rope(u[:, cols]) * q_scale).astype(BF16)

    def f_kv():
        kt_ref[:, 0:WINDOW] = kt_ref[:, t:]
        v_ref[0:WINDOW, :] = v_ref[t:, :]
        u = in_proj(slice(ATTN_WIDTH, ATTN_WIDTH + 2 * KV_WIDTH))
        kt_ref[:, WINDOW:] = rope(u[:, :KV_WIDTH]).T.astype(BF16)
        vv = u[:, KV_WIDTH:]
        for g in range(N_KV_HEADS):
            vg = vv[:, g * HEAD_DIM:(g + 1) * HEAD_DIM]
            cols = slice(g * LANES, (g + 1) * LANES)
            v_ref[WINDOW:, cols] = jnp.concatenate([vg, vg], axis=-1).astype(BF16)

    def f_up():
        halo = ext_ref[t:, :]
        ext_ref[0:POOL_HALO, :] = jnp.where(first, jnp.zeros_like(halo), halo)
        ext_ref[POOL_HALO:, :] = in_proj(slice(ATTN_WIDTH + 2 * KV_WIDTH, IN_PROJ_WIDTH))

    def f_pool(gi, w):
        def task():
            cols = slice(gi * POOL_GROUP_WIDTH, (gi + 1) * POOL_GROUP_WIDTH)
            pos_in_seq = ((tile_f % tiles_per_seq) * t
                          + lax.broadcasted_iota(jnp.int32, (t, 1), 0))
            acc = ext_ref[:, cols]
            span = 1
            while span < w:
                acc = acc + pltpu.roll(acc, span, 0)
                span *= 2
            tok = ext_ref[POOL_HALO:, cols]
            total = acc[POOL_HALO:]
            count = jnp.minimum(pos_in_seq + 1, w).astype(F32)
            mix_ref[:, ATTN_WIDTH + gi * POOL_GROUP_WIDTH:
                    ATTN_WIDTH + (gi + 1) * POOL_GROUP_WIDTH] = (total / count - tok).astype(BF16)
        return task

    n_blocks = t // WINDOW
    chunks = N_HEADS // 2
    bodies = [(j, chunk) for j in range(n_blocks) for chunk in range(chunks)]
    kv_cache, logits, probs = {}, {}, {}

    def band_mask(j):
        qi = lax.broadcasted_iota(jnp.int32, (WINDOW, 2 * WINDOW), 0)
        kj = lax.broadcasted_iota(jnp.int32, (WINDOW, 2 * WINDOW), 1)
        rel = kj - WINDOW - qi
        band = (rel <= 0) & (rel > -WINDOW)
        if j == 0:
            band = band & (kj >= jnp.where(first, WINDOW, 0))
        return band

    def block_diag_kv(j, g):
        if (j, g) not in kv_cache:
            keys = slice(j * WINDOW, (j + 2) * WINDOW)
            ktg = kt_ref[g * HEAD_DIM:(g + 1) * HEAD_DIM, keys]
            zk = jnp.zeros_like(ktg)
            k_bd = jnp.concatenate([jnp.concatenate([ktg, zk], axis=1),
                                    jnp.concatenate([zk, ktg], axis=1)], axis=0)
            vd = v_ref[keys, g * LANES:(g + 1) * LANES]
            zv = jnp.zeros_like(vd)
            v_bd = jnp.concatenate([jnp.where(low_lanes, vd, zv),
                                    jnp.where(low_lanes, zv, vd)], axis=0)
            kv_cache[(j, g)] = (k_bd, v_bd)
        return kv_cache[(j, g)]

    def scores(j, chunk):
        k_bd, _ = block_diag_kv(j, chunk // (GROUP // 2))
        qc = q_ref[j * WINDOW:(j + 1) * WINDOW, chunk * LANES:(chunk + 1) * LANES]
        return jnp.dot(qc, k_bd, preferred_element_type=F32)

    def softmax(j, chunk, lg):
        mask = band_mask(j)
        ps, inv_den = [], []
        for hh in range(2):
            sink = sinks_ref[2 * chunk + hh] * LOG2_E
            l = jnp.where(mask, lg[:, hh * 2 * WINDOW:(hh + 1) * 2 * WINDOW], -jnp.inf)
            m = jnp.maximum(jnp.max(l, axis=-1, keepdims=True), sink)
            p = jnp.exp2(l - m)
            den = jnp.sum(p, axis=-1, keepdims=True) + jnp.exp2(sink - m)
            ps.append(p.astype(BF16))
            inv_den.append(1.0 / den)
        return jnp.concatenate(ps, axis=-1), jnp.where(low_lanes, inv_den[0], inv_den[1])

    def values(j, chunk, p, inv_den):
        _, v_bd = block_diag_kv(j, chunk // (GROUP // 2))
        pv = jnp.dot(p, v_bd, preferred_element_type=F32)
        mix_ref[j * WINDOW:(j + 1) * WINDOW, chunk * LANES:(chunk + 1) * LANES] = (
            (pv * inv_den).astype(BF16))

    def out_proj(j):
        rows = slice(j * WINDOW, (j + 1) * WINDOW)
        mixed = jnp.dot(mix_ref[rows, :], wout_ref[...], preferred_element_type=F32)
        x1_ref[slot_f, rows, :] = x_ref[0, rows, :] + mod_row(2, seq_f) * mixed

    def f_attn(step):
        def task():
            n = len(bodies)
            if step < n:
                logits[step] = scores(*bodies[step])
            if 0 <= step - 1 < n:
                probs[step - 1] = softmax(*bodies[step - 1], logits.pop(step - 1))
            done = step - ATTN_SKEW
            if 0 <= done < n:
                values(*bodies[done], *probs.pop(done))
            ready = step - ATTN_SKEW - OUT_PROJ_LAG
            if (0 <= ready < n and bodies[ready][1] == chunks - 1
                    and bodies[ready][0] < n_blocks - LATE_OUT_PROJ):
                out_proj(bodies[ready][0])
        return task

    late_out_proj = [functools.partial(out_proj, j)
                     for j in range(n_blocks - LATE_OUT_PROJ, n_blocks)]

    front = [f_norm, f_trig, f_q, f_kv, f_up]
    front += [f_pool(gi, w) for gi, w in enumerate(POOL_WINDOWS)]
    front += [f_attn(step) for step in range(len(bodies) + ATTN_SKEW + OUT_PROJ_LAG)]

    def b_norm():
        x1 = x1_ref[slot_b]
        scale = n2_ref[...] * (1.0 + mod_row(4, seq_b))
        h2_ref[...] = (x1 * _rms_scale(x1) * scale + mod_row(3, seq_b)).astype(BF16)

    def b_gate_up(n):
        def task():
            cols = slice(n * FF_CHUNK, (n + 1) * FF_CHUNK)
            g = jnp.dot(h2_ref[...], wg_ref[:, cols], preferred_element_type=F32)
            u = jnp.dot(h2_ref[...], wu_ref[:, cols], preferred_element_type=F32)
            act_ref[:, cols] = (g * jax.nn.sigmoid(g) * u).astype(BF16)
        return task

    def b_down(n):
        def task():
            cols = slice(n * DOWN_CHUNK, (n + 1) * DOWN_CHUNK)
            ff = jnp.dot(act_ref[...], wd_ref[:, cols], preferred_element_type=F32)
            o_ref[0, :, cols] = x1_ref[slot_b, :, cols] + mod_row(5, seq_b)[:, cols] * ff
        return task

    def b_final():
        x2 = o_ref[0]
        o_ref[0] = x2 * _rms_scale(x2) * nf_ref[...]

    d_ff = wg_ref.shape[1]
    back = [b_norm]
    back += [b_gate_up(n) for n in range(d_ff // FF_CHUNK)]
    back += [b_down(n) for n in range(D_MODEL // DOWN_CHUNK)]
    back += [b_final]

    def run(tasks):
        kv_cache.clear(), logits.clear(), probs.clear()
        for task in tasks:
            task()

    @pl.when(s == 0)
    def _():
        kt_ref[:, t:] = jnp.zeros((KV_WIDTH, WINDOW), BF16)
        v_ref[t:, :] = jnp.zeros((WINDOW, 2 * KV_WIDTH), BF16)
        ext_ref[t:, :] = jnp.zeros((POOL_HALO, POOL_WIDTH), F32)

        def fold_pool(c, rows_ref):
            g = c - ATTN_WIDTH // CAST_CHUNK_ROWS
            if g < 0:
                return None
            cols = slice(g * POOL_GROUP_WIDTH, (g + 1) * POOL_GROUP_WIDTH)
            return _dot_f32(wpool_ref[g] * pscale_ref[:, cols], rows_ref[...])

        prime, casts = _weight_cast_tasks(
            [(win_hbm, win_ref, None), (wout_hbm, wout_ref, fold_pool),
             (wg_hbm, wg_ref, None), (wu_hbm, wu_ref, None), (wd_hbm, wd_ref, None)],
            stage_ref, cast_sems)
        front_casts = (win_hbm.shape[0] + wout_hbm.shape[0]) // CAST_CHUNK_ROWS
        prime()
        run(casts[:front_casts])
        run(_interleave(front, casts[front_casts:]) + late_out_proj)

    @pl.when(s > 0)
    def _():
        order = [task for task in _interleave(front, back) if task not in back[-2:]]
        run(order + [back[-2], late_out_proj[0], back[-1], late_out_proj[1]])


def _layer(x, mod, positions, sinks, norm1, norm2, norm_f, w_in, w_pool, pool_scale, w_out,
           w_gate, w_up, w_down):
    b, s, d = x.shape
    t = TOKEN_TILE
    d_ff = w_gate.shape[1]
    assert d == D_MODEL and s % t == 0 and t % WINDOW == 0 and d_ff % FF_CHUNK == 0
    assert d_ff % CAST_CHUNK_ROWS == 0 and d % CAST_CHUNK_ROWS == 0
    tiles_per_seq = s // t
    n_tiles = b * tiles_per_seq

    def front_tile(step):
        return jnp.minimum(step, n_tiles - 1)

    def back_tile(step):
        return jnp.maximum(step - 1, 0)

    def tok_map(tile_of):
        return lambda step, *_: (tile_of(step) // tiles_per_seq, tile_of(step) % tiles_per_seq, 0)

    per_row = LANES // HALF
    inv_freq = ROPE_THETA ** (-jnp.arange(HALF, dtype=F32) * (2.0 / HEAD_DIM))
    freq = jnp.tile(inv_freq, per_row).reshape(1, LANES)
    assert t // LANES == per_row
    pos_rows = positions.reshape(n_tiles, t // LANES, LANES)
    grid_spec = pltpu.PrefetchScalarGridSpec(
        num_scalar_prefetch=1,
        grid=(n_tiles + 1,),
        in_specs=[
            pl.BlockSpec((1, t, d), tok_map(front_tile)),
            pl.BlockSpec((1, t // LANES, LANES), lambda step, *_: (front_tile(step), 0, 0)),
            _const_spec((1, LANES)),
            _const_spec((N_MOD, b, d)),
            _const_spec((1, d)), _const_spec((1, d)), _const_spec((1, d)),
            _const_spec((1, POOL_WIDTH)),
            _const_spec(w_pool.shape),
        ] + [pl.BlockSpec(memory_space=pl.ANY)] * 5,
        out_specs=pl.BlockSpec((1, t, d), tok_map(back_tile)),
        scratch_shapes=[
            pltpu.VMEM((d, IN_PROJ_WIDTH), BF16),
            pltpu.VMEM((ATTN_WIDTH + POOL_WIDTH, d), BF16),
            pltpu.VMEM((d, d_ff), BF16),
            pltpu.VMEM((d, d_ff), BF16),
            pltpu.VMEM((d_ff, d), BF16),
            pltpu.VMEM((CAST_SLOTS, CAST_CHUNK_ROWS, max(d_ff, IN_PROJ_WIDTH)), F32),
            pltpu.SemaphoreType.DMA((CAST_SLOTS,)),
            pltpu.VMEM((t, LANES), F32),
            pltpu.VMEM((t, LANES), F32),
            pltpu.VMEM((t, d), BF16),
            pltpu.VMEM((t, ATTN_WIDTH), BF16),
            pltpu.VMEM((KV_WIDTH, t + WINDOW), BF16),
            pltpu.VMEM((t + WINDOW, 2 * KV_WIDTH), BF16),
            pltpu.VMEM((t + POOL_HALO, POOL_WIDTH), F32),
            pltpu.VMEM((t, ATTN_WIDTH + POOL_WIDTH), BF16),
            pltpu.VMEM((2, t, d), F32),
            pltpu.VMEM((t, d), BF16),
            pltpu.VMEM((t, d_ff), BF16),
        ],
    )
    return pl.pallas_call(
        functools.partial(_layer_kernel, tiles_per_seq=tiles_per_seq),
        grid_spec=grid_spec,
        out_shape=jax.ShapeDtypeStruct((b, s, d), F32),
        compiler_params=pltpu.CompilerParams(
            dimension_semantics=("arbitrary",), vmem_limit_bytes=VMEM_LIMIT_BYTES),
        name="layer",
    )(sinks, x, pos_rows, freq, mod, norm1.reshape(1, d), norm2.reshape(1, d), norm_f.reshape(1, d),
      pool_scale.reshape(1, -1), w_pool, w_in, w_out,
      w_gate, w_up, w_down)


def kernel(x, c, positions, w_ada, b_ada, norm1, w_in, sinks, w_pool, pool_scale,
           w_out, norm2, w_gate, w_up, w_down, norm_f):
    mod = _adaln_mod(c, w_ada, b_ada)
    return _layer(x, mod, positions, sinks, norm1, norm2, norm_f, w_in, w_pool, pool_scale,
                  w_out, w_gate, w_up, w_down)
```

```python
import functools
import math

import jax
import jax.numpy as jnp
from jax import lax
from jax.experimental import pallas as pl
from jax.experimental.pallas import tpu as pltpu

F32 = jnp.float32
BF16 = jnp.bfloat16

D_MODEL = 1024
HEAD_DIM = 64
N_HEADS = 8
N_KV_HEADS = 2
GROUP = N_HEADS // N_KV_HEADS
ATTN_WIDTH = N_HEADS * HEAD_DIM
KV_WIDTH = N_KV_HEADS * HEAD_DIM
POOL_WINDOWS = (2, 4, 8, 16)
POOL_GROUP_WIDTH = 128
POOL_WIDTH = POOL_GROUP_WIDTH * len(POOL_WINDOWS)
IN_PROJ_WIDTH = ATTN_WIDTH + 2 * KV_WIDTH + POOL_WIDTH
WINDOW = 128
ROPE_THETA = 10000.0
N_MOD = 6
RMS_EPS = 1e-6
LOG2_E = math.log2(math.e)
HALF = HEAD_DIM // 2

LANES = 128
POOL_HALO = 16
VMEM_LIMIT_BYTES = 56 * 1024 * 1024

MOD_BLOCK_COLS = 512
TOKEN_TILE = 512
FF_CHUNK = 256
DOWN_CHUNK = 256
ATTN_SKEW = 3
CAST_CHUNK_ROWS = 128
CAST_SLOTS = 4
CAST_ROWS_PER_ITER = 16
OUT_PROJ_LAG = 2
LATE_OUT_PROJ = 2
BACK_SPAN = 0.9


def _const_spec(shape):
    zeros = (0,) * len(shape)
    return pl.BlockSpec(shape, lambda *_: zeros, pipeline_mode=pl.Buffered(1))


def _rms_scale(x):
    return lax.rsqrt(jnp.mean(x * x, axis=-1, keepdims=True) + RMS_EPS)


def _interleave(front, back):
    keyed = [((k + 0.5) / len(front), 0, k, f) for k, f in enumerate(front)]
    keyed += [(BACK_SPAN * (k + 0.5) / len(back), 1, k, f) for k, f in enumerate(back)]
    return [f for *_, f in sorted(keyed, key=lambda e: e[:3])]


def _mod_kernel(ct_ref, w_ref, b_ref, o_ref):
    ct = ct_ref[...]
    sc = ct * jax.nn.sigmoid(ct)
    w = w_ref[...]
    for r in range(ct.shape[1]):
        o_ref[0, r:r + 1, :] = jnp.sum(w * sc[:, r:r + 1], axis=0, keepdims=True) + b_ref[...]


def _adaln_mod(c, w_ada, b_ada):
    b, d = c.shape
    cols = MOD_BLOCK_COLS
    per_mod = d // cols
    return pl.pallas_call(
        _mod_kernel,
        grid=(N_MOD * per_mod,),
        in_specs=[
            pl.BlockSpec((d, b), lambda j: (0, 0)),
            pl.BlockSpec((d, cols), lambda j: (0, j)),
            pl.BlockSpec((1, cols), lambda j: (0, j)),
        ],
        out_specs=pl.BlockSpec((1, b, cols), lambda j: (j // per_mod, 0, j % per_mod)),
        out_shape=jax.ShapeDtypeStruct((N_MOD, b, d), F32),
        compiler_params=pltpu.CompilerParams(dimension_semantics=("arbitrary",)),
        name="adaln_mod",
    )(c.T, w_ada, b_ada.reshape(1, -1))


def _dot_f32(a, b):
    ah = a.astype(BF16)
    al = (a - ah.astype(F32)).astype(BF16)
    bh = b.astype(BF16)
    bl = (b - bh.astype(F32)).astype(BF16)
    dot = functools.partial(jnp.dot, preferred_element_type=F32)
    return dot(ah, bh) + dot(ah, bl) + dot(al, bh)


def _weight_cast_tasks(triples, stage_ref, sems):
    chunks = [(src, dst, c, fold) for src, dst, fold in triples
              for c in range(src.shape[0] // CAST_CHUNK_ROWS)]
    slots = stage_ref.shape[0]

    def copy(k):
        src, _, c, _ = chunks[k]
        return pltpu.make_async_copy(
            src.at[pl.ds(c * CAST_CHUNK_ROWS, CAST_CHUNK_ROWS), :],
            stage_ref.at[k % slots, :, pl.ds(0, src.shape[1])],
            sems.at[k % slots])

    def start(k):
        copy(k).start(priority=k % 2)

    def prime():
        for k in range(min(slots - 1, len(chunks))):
            start(k)

    def cast(k):
        def task():
            src, dst, c, fold = chunks[k]
            if k + slots - 1 < len(chunks):
                start(k + slots - 1)
            copy(k).wait()
            folded = fold(c, stage_ref.at[k % slots, :, pl.ds(0, src.shape[1])]) if fold else None
            if folded is not None:
                dst[c * CAST_CHUNK_ROWS:(c + 1) * CAST_CHUNK_ROWS, :] = folded.astype(BF16)
                return

            def cast_rows(r, carry):
                off = pl.multiple_of(r * CAST_ROWS_PER_ITER, CAST_ROWS_PER_ITER)
                dst[pl.ds(c * CAST_CHUNK_ROWS + off, CAST_ROWS_PER_ITER), :] = (
                    stage_ref[k % slots, pl.ds(off, CAST_ROWS_PER_ITER), 0:src.shape[1]]
                    .astype(BF16))
                return carry

            lax.fori_loop(0, CAST_CHUNK_ROWS // CAST_ROWS_PER_ITER, cast_rows, 0)
        return task

    return prime, [cast(k) for k in range(len(chunks))]


def _layer_kernel(sinks_ref, x_ref, pos_ref, freq_ref, mod_ref, n1_ref, n2_ref, nf_ref,
                  pscale_ref, wpool_ref, win_hbm, wout_hbm, wg_hbm, wu_hbm, wd_hbm,
                  o_ref,
                  win_ref, wout_ref, wg_ref, wu_ref, wd_ref, stage_ref, cast_sems,
                  cos_ref, sin_ref, h_ref, q_ref, kt_ref, v_ref, ext_ref, mix_ref, x1_ref, h2_ref,
                  act_ref, *, tiles_per_seq):
    t = x_ref.shape[1]
    s = pl.program_id(0)
    n_tiles = pl.num_programs(0) - 1
    tile_f = jnp.minimum(s, n_tiles - 1)
    first = (tile_f % tiles_per_seq) == 0
    slot_f = s % 2
    slot_b = 1 - slot_f
    seq_f = tile_f // tiles_per_seq
    seq_b = jnp.maximum(s - 1, 0) // tiles_per_seq

    def mod_row(k, seq):
        return mod_ref[k, pl.ds(seq, 1), :]

    lane =lax.broadcasted_iota(jnp.int32, (1, LANES), 1)
    low_lanes = lane < HEAD_DIM
    first_half = (lane % HEAD_DIM) < HALF

    def f_norm():
        x = x_ref[0]
        scale = n1_ref[...] * (1.0 + mod_row(1, seq_f))
        h_ref[...] = (x * _rms_scale(x) * scale + mod_row(0, seq_f)).astype(BF16)

    def in_proj(cols):
        return jnp.dot(h_ref[...], win_ref[:, cols], preferred_element_type=F32)

    def f_trig():
        per_row = LANES // HALF
        pos = pos_ref[...].astype(F32)
        pos = jnp.concatenate([jnp.broadcast_to(pos[:, m:m + 1], (t // per_row, HALF))
                               for m in range(per_row)], axis=-1)
        ang = pos * freq_ref[...]
        cos, sin = jnp.cos(ang), jnp.sin(ang)
        for m in range(per_row):
            c32 = cos[:, m * HALF:(m + 1) * HALF]
            s32 = sin[:, m * HALF:(m + 1) * HALF]
            rows = pl.ds(m, t // per_row, stride=per_row)
            cos_ref[rows, :] = jnp.concatenate([c32, c32, c32, c32], axis=-1)
            sin_ref[rows, :] = jnp.concatenate([-s32, s32, -s32, s32], axis=-1)

    def rope(tile):
        partner = jnp.where(first_half,
                            pltpu.roll(tile, LANES - HALF, 1),
                            pltpu.roll(tile, HALF, 1))
        return tile * cos_ref[...] + partner * sin_ref[...]

    def f_q():
        u = in_proj(slice(0, ATTN_WIDTH))
        q_scale = LOG2_E / math.sqrt(HEAD_DIM)
        for j in range(ATTN_WIDTH // LANES):
            cols = slice(j * LANES, (j + 1) * LANES)
            q_ref[:, cols] = (rope(u[:, cols]) * q_scale).astype(BF16)

    def f_kv():
        kt_ref[:, 0:WINDOW] = kt_ref[:, t:]
        v_ref[0:WINDOW, :] = v_ref[t:, :]
        u = in_proj(slice(ATTN_WIDTH, ATTN_WIDTH + 2 * KV_WIDTH))
        kt_ref[:, WINDOW:] = rope(u[:, :KV_WIDTH]).T.astype(BF16)
        vv = u[:, KV_WIDTH:]
        for g in range(N_KV_HEADS):
            vg = vv[:, g * HEAD_DIM:(g + 1) * HEAD_DIM]
            cols = slice(g * LANES, (g + 1) * LANES)
            v_ref[WINDOW:, cols] = jnp.concatenate([vg, vg], axis=-1).astype(BF16)

    def f_up():
        halo = ext_ref[t:, :]
        ext_ref[0:POOL_HALO, :] = jnp.where(first, jnp.zeros_like(halo), halo)
        ext_ref[POOL_HALO:, :] = in_proj(slice(ATTN_WIDTH + 2 * KV_WIDTH, IN_PROJ_WIDTH))

    def f_pool(gi, w):
        def task():
            cols = slice(gi * POOL_GROUP_WIDTH, (gi + 1) * POOL_GROUP_WIDTH)
            pos_in_seq = ((tile_f % tiles_per_seq) * t
                          + lax.broadcasted_iota(jnp.int32, (t, 1), 0))
            acc = ext_ref[:, cols]
            span = 1
            while span < w:
                acc = acc + pltpu.roll(acc, span, 0)
                span *= 2
            tok = ext_ref[POOL_HALO:, cols]
            total = acc[POOL_HALO:]
            count = jnp.minimum(pos_in_seq + 1, w).astype(F32)
            mix_ref[:, ATTN_WIDTH + gi * POOL_GROUP_WIDTH:
                    ATTN_WIDTH + (gi + 1) * POOL_GROUP_WIDTH] = (total / count - tok).astype(BF16)
        return task

    n_blocks = t // WINDOW
    chunks = N_HEADS // 2
    bodies = [(j, chunk) for j in range(n_blocks) for chunk in range(chunks)]
    kv_cache, logits, probs = {}, {}, {}

    def band_mask(j):
        qi = lax.broadcasted_iota(jnp.int32, (WINDOW, 2 * WINDOW), 0)
        kj = lax.broadcasted_iota(jnp.int32, (WINDOW, 2 * WINDOW), 1)
        rel = kj - WINDOW - qi
        band = (rel <= 0) & (rel > -WINDOW)
        if j == 0:
            band = band & (kj >= jnp.where(first, WINDOW, 0))
        return band

    def block_diag_kv(j, g):
        if (j, g) not in kv_cache:
            keys = slice(j * WINDOW, (j + 2) * WINDOW)
            ktg = kt_ref[g * HEAD_DIM:(g + 1) * HEAD_DIM, keys]
            zk = jnp.zeros_like(ktg)
            k_bd = jnp.concatenate([jnp.concatenate([ktg, zk], axis=1),
                                    jnp.concatenate([zk, ktg], axis=1)], axis=0)
            vd = v_ref[keys, g * LANES:(g + 1) * LANES]
            zv = jnp.zeros_like(vd)
            v_bd = jnp.concatenate([jnp.where(low_lanes, vd, zv),
                                    jnp.where(low_lanes, zv, vd)], axis=0)
            kv_cache[(j, g)] = (k_bd, v_bd)
        return kv_cache[(j, g)]

    def scores(j, chunk):
        k_bd, _ = block_diag_kv(j, chunk // (GROUP // 2))
        qc = q_ref[j * WINDOW:(j + 1) * WINDOW, chunk * LANES:(chunk + 1) * LANES]
        return jnp.dot(qc, k_bd, preferred_element_type=F32)

    def softmax(j, chunk, lg):
        mask = band_mask(j)
        ps, inv_den = [], []
        for hh in range(2):
            sink = sinks_ref[2 * chunk + hh] * LOG2_E
            l = jnp.where(mask, lg[:, hh * 2 * WINDOW:(hh + 1) * 2 * WINDOW], -jnp.inf)
            m = jnp.maximum(jnp.max(l, axis=-1, keepdims=True), sink)
            p = jnp.exp2(l - m)
            den = jnp.sum(p, axis=-1, keepdims=True) + jnp.exp2(sink - m)
            ps.append(p.astype(BF16))
            inv_den.append(1.0 / den)
        return jnp.concatenate(ps, axis=-1), jnp.where(low_lanes, inv_den[0], inv_den[1])

    def values(j, chunk, p, inv_den):
        _, v_bd = block_diag_kv(j, chunk // (GROUP // 2))
        pv = jnp.dot(p, v_bd, preferred_element_type=F32)
        mix_ref[j * WINDOW:(j + 1) * WINDOW, chunk * LANES:(chunk + 1) * LANES] = (
            (pv * inv_den).astype(BF16))

    def out_proj(j):
        rows = slice(j * WINDOW, (j + 1) * WINDOW)
        mixed = jnp.dot(mix_ref[rows, :], wout_ref[...], preferred_element_type=F32)
        x1_ref[slot_f, rows, :] = x_ref[0, rows, :] + mod_row(2, seq_f) * mixed

    def f_attn(step):
        def task():
            n = len(bodies)
            if step < n:
                logits[step] = scores(*bodies[step])
            if 0 <= step - 1 < n:
                probs[step - 1] = softmax(*bodies[step - 1], logits.pop(step - 1))
            done = step - ATTN_SKEW
            if 0 <= done < n:
                values(*bodies[done], *probs.pop(done))
            ready = step - ATTN_SKEW - OUT_PROJ_LAG
            if (0 <= ready < n and bodies[ready][1] == chunks - 1
                    and bodies[ready][0] < n_blocks - LATE_OUT_PROJ):
                out_proj(bodies[ready][0])
        return task

    late_out_proj = [functools.partial(out_proj, j)
                     for j in range(n_blocks - LATE_OUT_PROJ, n_blocks)]

    front = [f_norm, f_trig, f_q, f_kv, f_up]
    front += [f_pool(gi, w) for gi, w in enumerate(POOL_WINDOWS)]
    front += [f_attn(step) for step in range(len(bodies) + ATTN_SKEW + OUT_PROJ_LAG)]

    def b_norm():
        x1 = x1_ref[slot_b]
        scale = n2_ref[...] * (1.0 + mod_row(4, seq_b))
        h2_ref[...] = (x1 * _rms_scale(x1) * scale + mod_row(3, seq_b)).astype(BF16)

    def b_gate_up(n):
        def task():
            cols = slice(n * FF_CHUNK, (n + 1) * FF_CHUNK)
            g = jnp.dot(h2_ref[...], wg_ref[:, cols], preferred_element_type=F32)
            u = jnp.dot(h2_ref[...], wu_ref[:, cols], preferred_element_type=F32)
            act_ref[:, cols] = (g * jax.nn.sigmoid(g) * u).astype(BF16)
        return task

    def b_down(n):
        def task():
            cols = slice(n * DOWN_CHUNK, (n + 1) * DOWN_CHUNK)
            ff = jnp.dot(act_ref[...], wd_ref[:, cols], preferred_element_type=F32)
            o_ref[0, :, cols] = x1_ref[slot_b, :, cols] + mod_row(5, seq_b)[:, cols] * ff
        return task

    def b_final():
        x2 = o_ref[0]
        o_ref[0] = x2 * _rms_scale(x2) * nf_ref[...]

    d_ff = wg_ref.shape[1]
    back = [b_norm]
    back += [b_gate_up(n) for n in range(d_ff // FF_CHUNK)]
    back += [b_down(n) for n in range(D_MODEL // DOWN_CHUNK)]
    back += [b_final]

    def run(tasks):
        kv_cache.clear(), logits.clear(), probs.clear()
        for task in tasks:
            task()

    @pl.when(s == 0)
    def _():
        kt_ref[:, t:] = jnp.zeros((KV_WIDTH, WINDOW), BF16)
        v_ref[t:, :] = jnp.zeros((WINDOW, 2 * KV_WIDTH), BF16)
        ext_ref[t:, :] = jnp.zeros((POOL_HALO, POOL_WIDTH), F32)

        def fold_pool(c, rows_ref):
            g = c - ATTN_WIDTH // CAST_CHUNK_ROWS
            if g < 0:
                return None
            cols = slice(g * POOL_GROUP_WIDTH, (g + 1) * POOL_GROUP_WIDTH)
            return _dot_f32(wpool_ref[g] * pscale_ref[:, cols], rows_ref[...])

        prime, casts = _weight_cast_tasks(
            [(win_hbm, win_ref, None), (wout_hbm, wout_ref, fold_pool),
             (wg_hbm, wg_ref, None), (wu_hbm, wu_ref, None), (wd_hbm, wd_ref, None)],
            stage_ref, cast_sems)
        front_casts = (win_hbm.shape[0] + wout_hbm.shape[0]) // CAST_CHUNK_ROWS
        prime()
        run(casts[:front_casts])
        run(_interleave(front, casts[front_casts:]) + late_out_proj)

    @pl.when(s > 0)
    def _():
        order = [task for task in _interleave(front, back) if task not in back[-2:]]
        run(order + [back[-2], late_out_proj[0], back[-1], late_out_proj[1]])


def _layer(x, mod, positions, sinks, norm1, norm2, norm_f, w_in, w_pool, pool_scale, w_out,
           w_gate, w_up, w_down):
    b, s, d = x.shape
    t = TOKEN_TILE
    d_ff = w_gate.shape[1]
    assert d == D_MODEL and s % t == 0 and t % WINDOW == 0 and d_ff % FF_CHUNK == 0
    assert d_ff % CAST_CHUNK_ROWS == 0 and d % CAST_CHUNK_ROWS == 0
    tiles_per_seq = s // t
    n_tiles = b * tiles_per_seq

    def front_tile(step):
        return jnp.minimum(step, n_tiles - 1)

    def back_tile(step):
        return jnp.maximum(step - 1, 0)

    def tok_map(tile_of):
        return lambda step, *_: (tile_of(step) // tiles_per_seq, tile_of(step) % tiles_per_seq, 0)

    per_row = LANES // HALF
    inv_freq = ROPE_THETA ** (-jnp.arange(HALF, dtype=F32) * (2.0 / HEAD_DIM))
    freq = jnp.tile(inv_freq, per_row).reshape(1, LANES)
    pos_rows = positions.reshape(b * s // per_row, per_row)
    grid_spec = pltpu.PrefetchScalarGridSpec(
        num_scalar_prefetch=1,
        grid=(n_tiles + 1,),
        in_specs=[
            pl.BlockSpec((1, t, d), tok_map(front_tile)),
            pl.BlockSpec((t // per_row, per_row), lambda step, *_: (front_tile(step), 0)),
            _const_spec((1, LANES)),
            _const_spec((N_MOD, b, d)),
            _const_spec((1, d)), _const_spec((1, d)), _const_spec((1, d)),
            _const_spec((1, POOL_WIDTH)),
            _const_spec(w_pool.shape),
        ] + [pl.BlockSpec(memory_space=pl.ANY)] * 5,
        out_specs=pl.BlockSpec((1, t, d), tok_map(back_tile)),
        scratch_shapes=[
            pltpu.VMEM((d, IN_PROJ_WIDTH), BF16),
            pltpu.VMEM((ATTN_WIDTH + POOL_WIDTH, d), BF16),
            pltpu.VMEM((d, d_ff), BF16),
            pltpu.VMEM((d, d_ff), BF16),
            pltpu.VMEM((d_ff, d), BF16),
            pltpu.VMEM((CAST_SLOTS, CAST_CHUNK_ROWS, max(d_ff, IN_PROJ_WIDTH)), F32),
            pltpu.SemaphoreType.DMA((CAST_SLOTS,)),
            pltpu.VMEM((t, LANES), F32),
            pltpu.VMEM((t, LANES), F32),
            pltpu.VMEM((t, d), BF16),
            pltpu.VMEM((t, ATTN_WIDTH), BF16),
            pltpu.VMEM((KV_WIDTH, t + WINDOW), BF16),
            pltpu.VMEM((t + WINDOW, 2 * KV_WIDTH), BF16),
            pltpu.VMEM((t + POOL_HALO, POOL_WIDTH), F32),
            pltpu.VMEM((t, ATTN_WIDTH + POOL_WIDTH), BF16),
            pltpu.VMEM((2, t, d), F32),
            pltpu.VMEM((t, d), BF16),
            pltpu.VMEM((t, d_ff), BF16),
        ],
    )
    return pl.pallas_call(
        functools.partial(_layer_kernel, tiles_per_seq=tiles_per_seq),
        grid_spec=grid_spec,
        out_shape=jax.ShapeDtypeStruct((b, s, d), F32),
        compiler_params=pltpu.CompilerParams(
            dimension_semantics=("arbitrary",), vmem_limit_bytes=VMEM_LIMIT_BYTES),
        name="layer",
    )(sinks, x, pos_rows, freq, mod, norm1.reshape(1, d), norm2.reshape(1, d), norm_f.reshape(1, d),
      pool_scale.reshape(1, -1), w_pool, w_in, w_out,
      w_gate, w_up, w_down)


def kernel(x, c, positions, w_ada, b_ada, norm1, w_in, sinks, w_pool, pool_scale,
           w_out, norm2, w_gate, w_up, w_down, norm_f):
    mod = _adaln_mod(c, w_ada, b_ada)
    return _layer(x, mod, positions, sinks, norm1, norm2, norm_f, w_in, w_pool, pool_scale,
                  w_out, w_gate, w_up, w_down)
```

```python
import functools
import math

import jax
import jax.numpy as jnp
from jax import lax
from jax.experimental import pallas as pl
from jax.experimental.pallas import tpu as pltpu

F32 = jnp.float32
BF16 = jnp.bfloat16

D_MODEL = 1024
HEAD_DIM = 64
N_HEADS = 8
N_KV_HEADS = 2
GROUP = N_HEADS // N_KV_HEADS
ATTN_WIDTH = N_HEADS * HEAD_DIM
KV_WIDTH = N_KV_HEADS * HEAD_DIM
POOL_WINDOWS = (2, 4, 8, 16)
POOL_GROUP_WIDTH = 128
POOL_WIDTH = POOL_GROUP_WIDTH * len(POOL_WINDOWS)
IN_PROJ_WIDTH = ATTN_WIDTH + 2 * KV_WIDTH + POOL_WIDTH
WINDOW = 128
ROPE_THETA = 10000.0
N_MOD = 6
RMS_EPS = 1e-6
LOG2_E = math.log2(math.e)
HALF = HEAD_DIM // 2

LANES = 128
POOL_HALO = 16
VMEM_LIMIT_BYTES = 56 * 1024 * 1024

MOD_ROWS_PER_STEP = 2
TOKEN_TILE = 512
FF_CHUNK = 256
DOWN_CHUNK = 256
ATTN_SKEW = 3
CAST_CHUNK_ROWS = 128
CAST_SLOTS = 4
CAST_ROWS_PER_ITER = 16
OUT_PROJ_LAG = 2
LATE_OUT_PROJ = 2
BACK_SPAN = 0.9


def _const_spec(shape):
    zeros = (0,) * len(shape)
    return pl.BlockSpec(shape, lambda *_: zeros, pipeline_mode=pl.Buffered(1))


def _rms_scale(x):
    return lax.rsqrt(jnp.mean(x * x, axis=-1, keepdims=True) + RMS_EPS)


def _interleave(front, back):
    keyed = [((k + 0.5) / len(front), 0, k, f) for k, f in enumerate(front)]
    keyed += [(BACK_SPAN * (k + 0.5) / len(back), 1, k, f) for k, f in enumerate(back)]
    return [f for *_, f in sorted(keyed, key=lambda e: e[:3])]


def _mod_kernel(ct_ref, w_ref, b_ref, o_ref):
    ct = ct_ref[...]
    sc = ct * jax.nn.sigmoid(ct)
    d = o_ref.shape[2]
    for k in range(o_ref.shape[0]):
        w = w_ref[:, k * d:(k + 1) * d]
        for r in range(ct.shape[1]):
            o_ref[k, r:r + 1, :] = (jnp.sum(w * sc[:, r:r + 1], axis=0, keepdims=True)
                                    + b_ref[:, k * d:(k + 1) * d])


def _adaln_mod(c, w_ada, b_ada):
    b, d = c.shape
    rows = MOD_ROWS_PER_STEP
    return pl.pallas_call(
        _mod_kernel,
        grid=(N_MOD // rows,),
        in_specs=[
            pl.BlockSpec((d, b), lambda j: (0, 0)),
            pl.BlockSpec((d, rows * d), lambda j: (0, j)),
            pl.BlockSpec((1, rows * d), lambda j: (0, j)),
        ],
        out_specs=pl.BlockSpec((rows, b, d), lambda j: (j, 0, 0)),
        out_shape=jax.ShapeDtypeStruct((N_MOD, b, d), F32),
        compiler_params=pltpu.CompilerParams(dimension_semantics=("arbitrary",)),
        name="adaln_mod",
    )(c.T, w_ada, b_ada.reshape(1, -1))


def _dot_f32(a, b):
    ah = a.astype(BF16)
    al = (a - ah.astype(F32)).astype(BF16)
    bh = b.astype(BF16)
    bl = (b - bh.astype(F32)).astype(BF16)
    dot = functools.partial(jnp.dot, preferred_element_type=F32)
    return dot(ah, bh) + dot(ah, bl) + dot(al, bh)


def _weight_cast_tasks(triples, stage_ref, sems):
    chunks = [(src, dst, c, fold) for src, dst, fold in triples
              for c in range(src.shape[0] // CAST_CHUNK_ROWS)]
    slots = stage_ref.shape[0]

    def copy(k):
        src, _, c, _ = chunks[k]
        return pltpu.make_async_copy(
            src.at[pl.ds(c * CAST_CHUNK_ROWS, CAST_CHUNK_ROWS), :],
            stage_ref.at[k % slots, :, pl.ds(0, src.shape[1])],
            sems.at[k % slots])

    def start(k):
        copy(k).start(priority=k % 2)

    def prime():
        for k in range(min(slots - 1, len(chunks))):
            start(k)

    def cast(k):
        def task():
            src, dst, c, fold = chunks[k]
            if k + slots - 1 < len(chunks):
                start(k + slots - 1)
            copy(k).wait()
            folded = fold(c, stage_ref.at[k % slots, :, pl.ds(0, src.shape[1])]) if fold else None
            if folded is not None:
                dst[c * CAST_CHUNK_ROWS:(c + 1) * CAST_CHUNK_ROWS, :] = folded.astype(BF16)
                return

            def cast_rows(r, carry):
                off = pl.multiple_of(r * CAST_ROWS_PER_ITER, CAST_ROWS_PER_ITER)
                dst[pl.ds(c * CAST_CHUNK_ROWS + off, CAST_ROWS_PER_ITER), :] = (
                    stage_ref[k % slots, pl.ds(off, CAST_ROWS_PER_ITER), 0:src.shape[1]]
                    .astype(BF16))
                return carry

            lax.fori_loop(0, CAST_CHUNK_ROWS // CAST_ROWS_PER_ITER, cast_rows, 0)
        return task

    return prime, [cast(k) for k in range(len(chunks))]


def _layer_kernel(sinks_ref, x_ref, pos_ref, freq_ref, mod_ref, n1_ref, n2_ref, nf_ref,
                  pscale_ref, wpool_ref, win_hbm, wout_hbm, wg_hbm, wu_hbm, wd_hbm,
                  o_ref,
                  win_ref, wout_ref, wg_ref, wu_ref, wd_ref, stage_ref, cast_sems,
                  cos_ref, sin_ref, h_ref, q_ref, kt_ref, v_ref, ext_ref, mix_ref, x1_ref, h2_ref,
                  act_ref, *, tiles_per_seq):
    t = x_ref.shape[1]
    s = pl.program_id(0)
    n_tiles = pl.num_programs(0) - 1
    tile_f = jnp.minimum(s, n_tiles - 1)
    first = (tile_f % tiles_per_seq) == 0
    slot_f = s % 2
    slot_b = 1 - slot_f
    seq_f = tile_f // tiles_per_seq
    seq_b = jnp.maximum(s - 1, 0) // tiles_per_seq

    def mod_row(k, seq):
        return mod_ref[k, pl.ds(seq, 1), :]

    lane =lax.broadcasted_iota(jnp.int32, (1, LANES), 1)
    low_lanes = lane < HEAD_DIM
    first_half = (lane % HEAD_DIM) < HALF

    def f_norm():
        x = x_ref[0]
        scale = n1_ref[...] * (1.0 + mod_row(1, seq_f))
        h_ref[...] = (x * _rms_scale(x) * scale + mod_row(0, seq_f)).astype(BF16)

    def in_proj(cols):
        return jnp.dot(h_ref[...], win_ref[:, cols], preferred_element_type=F32)

    def f_trig():
        per_row = LANES // HALF
        pos = pos_ref[...].astype(F32)
        pos = jnp.concatenate([jnp.broadcast_to(pos[:, m:m + 1], (t // per_row, HALF))
                               for m in range(per_row)], axis=-1)
        ang = pos * freq_ref[...]
        cos, sin = jnp.cos(ang), jnp.sin(ang)
        for m in range(per_row):
            c32 = cos[:, m * HALF:(m + 1) * HALF]
            s32 = sin[:, m * HALF:(m + 1) * HALF]
            rows = pl.ds(m, t // per_row, stride=per_row)
            cos_ref[rows, :] = jnp.concatenate([c32, c32, c32, c32], axis=-1)
            sin_ref[rows, :] = jnp.concatenate([-s32, s32, -s32, s32], axis=-1)

    def rope(tile):
        partner = jnp.where(first_half,
                            pltpu.roll(tile, LANES - HALF, 1),
                            pltpu.roll(tile, HALF, 1))
        return tile * cos_ref[...] + partner * sin_ref[...]

    def f_q():
        u = in_proj(slice(0, ATTN_WIDTH))
        q_scale = LOG2_E / math.sqrt(HEAD_DIM)
        for j in range(ATTN_WIDTH // LANES):
            cols = slice(j * LANES, (j + 1) * LANES)
            q_ref[:, cols] = (rope(u[:, cols]) * q_scale).astype(BF16)

    def f_kv():
        kt_ref[:, 0:WINDOW] = kt_ref[:, t:]
        v_ref[0:WINDOW, :] = v_ref[t:, :]
        u = in_proj(slice(ATTN_WIDTH, ATTN_WIDTH + 2 * KV_WIDTH))
        kt_ref[:, WINDOW:] = rope(u[:, :KV_WIDTH]).T.astype(BF16)
        vv = u[:, KV_WIDTH:]
        for g in range(N_KV_HEADS):
            vg = vv[:, g * HEAD_DIM:(g + 1) * HEAD_DIM]
            cols = slice(g * LANES, (g + 1) * LANES)
            v_ref[WINDOW:, cols] = jnp.concatenate([vg, vg], axis=-1).astype(BF16)

    def f_up():
        halo = ext_ref[t:, :]
        ext_ref[0:POOL_HALO, :] = jnp.where(first, jnp.zeros_like(halo), halo)
        ext_ref[POOL_HALO:, :] = in_proj(slice(ATTN_WIDTH + 2 * KV_WIDTH, IN_PROJ_WIDTH))

    def f_pool(gi, w):
        def task():
            cols = slice(gi * POOL_GROUP_WIDTH, (gi + 1) * POOL_GROUP_WIDTH)
            pos_in_seq = ((tile_f % tiles_per_seq) * t
                          + lax.broadcasted_iota(jnp.int32, (t, 1), 0))
            acc = ext_ref[:, cols]
            span = 1
            while span < w:
                acc = acc + pltpu.roll(acc, span, 0)
                span *= 2
            tok = ext_ref[POOL_HALO:, cols]
            total = acc[POOL_HALO:]
            count = jnp.minimum(pos_in_seq + 1, w).astype(F32)
            mix_ref[:, ATTN_WIDTH + gi * POOL_GROUP_WIDTH:
                    ATTN_WIDTH + (gi + 1) * POOL_GROUP_WIDTH] = (total / count - tok).astype(BF16)
        return task

    n_blocks = t // WINDOW
    chunks = N_HEADS // 2
    bodies = [(j, chunk) for j in range(n_blocks) for chunk in range(chunks)]
    kv_cache, logits, probs = {}, {}, {}

    def band_mask(j):
        qi = lax.broadcasted_iota(jnp.int32, (WINDOW, 2 * WINDOW), 0)
        kj = lax.broadcasted_iota(jnp.int32, (WINDOW, 2 * WINDOW), 1)
        rel = kj - WINDOW - qi
        band = (rel <= 0) & (rel > -WINDOW)
        if j == 0:
            band = band & (kj >= jnp.where(first, WINDOW, 0))
        return band

    def block_diag_kv(j, g):
        if (j, g) not in kv_cache:
            keys = slice(j * WINDOW, (j + 2) * WINDOW)
            ktg = kt_ref[g * HEAD_DIM:(g + 1) * HEAD_DIM, keys]
            zk = jnp.zeros_like(ktg)
            k_bd = jnp.concatenate([jnp.concatenate([ktg, zk], axis=1),
                                    jnp.concatenate([zk, ktg], axis=1)], axis=0)
            vd = v_ref[keys, g * LANES:(g + 1) * LANES]
            zv = jnp.zeros_like(vd)
            v_bd = jnp.concatenate([jnp.where(low_lanes, vd, zv),
                                    jnp.where(low_lanes, zv, vd)], axis=0)
            kv_cache[(j, g)] = (k_bd, v_bd)
        return kv_cache[(j, g)]

    def scores(j, chunk):
        k_bd, _ = block_diag_kv(j, chunk // (GROUP // 2))
        qc = q_ref[j * WINDOW:(j + 1) * WINDOW, chunk * LANES:(chunk + 1) * LANES]
        return jnp.dot(qc, k_bd, preferred_element_type=F32)

    def softmax(j, chunk, lg):
        mask = band_mask(j)
        ps, inv_den = [], []
        for hh in range(2):
            sink = sinks_ref[2 * chunk + hh] * LOG2_E
            l = jnp.where(mask, lg[:, hh * 2 * WINDOW:(hh + 1) * 2 * WINDOW], -jnp.inf)
            m = jnp.maximum(jnp.max(l, axis=-1, keepdims=True), sink)
            p = jnp.exp2(l - m)
            den = jnp.sum(p, axis=-1, keepdims=True) + jnp.exp2(sink - m)
            ps.append(p.astype(BF16))
            inv_den.append(1.0 / den)
        return jnp.concatenate(ps, axis=-1), jnp.where(low_lanes, inv_den[0], inv_den[1])

    def values(j, chunk, p, inv_den):
        _, v_bd = block_diag_kv(j, chunk // (GROUP // 2))
        pv = jnp.dot(p, v_bd, preferred_element_type=F32)
        mix_ref[j * WINDOW:(j + 1) * WINDOW, chunk * LANES:(chunk + 1) * LANES] = (
            (pv * inv_den).astype(BF16))

    def out_proj(j):
        rows = slice(j * WINDOW, (j + 1) * WINDOW)
        mixed = jnp.dot(mix_ref[rows, :], wout_ref[...], preferred_element_type=F32)
        x1_ref[slot_f, rows, :] = x_ref[0, rows, :] + mod_row(2, seq_f) * mixed

    def f_attn(step):
        def task():
            n = len(bodies)
            if step < n:
                logits[step] = scores(*bodies[step])
            if 0 <= step - 1 < n:
                probs[step - 1] = softmax(*bodies[step - 1], logits.pop(step - 1))
            done = step - ATTN_SKEW
            if 0 <= done < n:
                values(*bodies[done], *probs.pop(done))
            ready = step - ATTN_SKEW - OUT_PROJ_LAG
            if (0 <= ready < n and bodies[ready][1] == chunks - 1
                    and bodies[ready][0] < n_blocks - LATE_OUT_PROJ):
                out_proj(bodies[ready][0])
        return task

    late_out_proj = [functools.partial(out_proj, j)
                     for j in range(n_blocks - LATE_OUT_PROJ, n_blocks)]

    front = [f_norm, f_trig, f_q, f_kv, f_up]
    front += [f_pool(gi, w) for gi, w in enumerate(POOL_WINDOWS)]
    front += [f_attn(step) for step in range(len(bodies) + ATTN_SKEW + OUT_PROJ_LAG)]

    def b_norm():
        x1 = x1_ref[slot_b]
        scale = n2_ref[...] * (1.0 + mod_row(4, seq_b))
        h2_ref[...] = (x1 * _rms_scale(x1) * scale + mod_row(3, seq_b)).astype(BF16)

    def b_gate_up(n):
        def task():
            cols = slice(n * FF_CHUNK, (n + 1) * FF_CHUNK)
            g = jnp.dot(h2_ref[...], wg_ref[:, cols], preferred_element_type=F32)
            u = jnp.dot(h2_ref[...], wu_ref[:, cols], preferred_element_type=F32)
            act_ref[:, cols] = (g * jax.nn.sigmoid(g) * u).astype(BF16)
        return task

    def b_down(n):
        def task():
            cols = slice(n * DOWN_CHUNK, (n + 1) * DOWN_CHUNK)
            ff = jnp.dot(act_ref[...], wd_ref[:, cols], preferred_element_type=F32)
            o_ref[0, :, cols] = x1_ref[slot_b, :, cols] + mod_row(5, seq_b)[:, cols] * ff
        return task

    def b_final():
        x2 = o_ref[0]
        o_ref[0] = x2 * _rms_scale(x2) * nf_ref[...]

    d_ff = wg_ref.shape[1]
    back = [b_norm]
    back += [b_gate_up(n) for n in range(d_ff // FF_CHUNK)]
    back += [b_down(n) for n in range(D_MODEL // DOWN_CHUNK)]
    back += [b_final]

    def run(tasks):
        kv_cache.clear(), logits.clear(), probs.clear()
        for task in tasks:
            task()

    @pl.when(s == 0)
    def _():
        kt_ref[:, t:] = jnp.zeros((KV_WIDTH, WINDOW), BF16)
        v_ref[t:, :] = jnp.zeros((WINDOW, 2 * KV_WIDTH), BF16)
        ext_ref[t:, :] = jnp.zeros((POOL_HALO, POOL_WIDTH), F32)

        def fold_pool(c, rows_ref):
            g = c - ATTN_WIDTH // CAST_CHUNK_ROWS
            if g < 0:
                return None
            cols = slice(g * POOL_GROUP_WIDTH, (g + 1) * POOL_GROUP_WIDTH)
            return _dot_f32(wpool_ref[g] * pscale_ref[:, cols], rows_ref[...])

        prime, casts = _weight_cast_tasks(
            [(win_hbm, win_ref, None), (wout_hbm, wout_ref, fold_pool),
             (wg_hbm, wg_ref, None), (wu_hbm, wu_ref, None), (wd_hbm, wd_ref, None)],
            stage_ref, cast_sems)
        front_casts = (win_hbm.shape[0] + wout_hbm.shape[0]) // CAST_CHUNK_ROWS
        prime()
        run(casts[:front_casts])
        run(_interleave(front, casts[front_casts:]) + late_out_proj)

    @pl.when(s > 0)
    def _():
        order = [task for task in _interleave(front, back) if task not in back[-2:]]
        run(order + [back[-2], late_out_proj[0], back[-1], late_out_proj[1]])


def _layer(x, mod, positions, sinks, norm1, norm2, norm_f, w_in, w_pool, pool_scale, w_out,
           w_gate, w_up, w_down):
    b, s, d = x.shape
    t = TOKEN_TILE
    d_ff = w_gate.shape[1]
    assert d == D_MODEL and s % t == 0 and t % WINDOW == 0 and d_ff % FF_CHUNK == 0
    assert d_ff % CAST_CHUNK_ROWS == 0 and d % CAST_CHUNK_ROWS == 0
    tiles_per_seq = s // t
    n_tiles = b * tiles_per_seq

    def front_tile(step):
        return jnp.minimum(step, n_tiles - 1)

    def back_tile(step):
        return jnp.maximum(step - 1, 0)

    def tok_map(tile_of):
        return lambda step, *_: (tile_of(step) // tiles_per_seq, tile_of(step) % tiles_per_seq, 0)

    per_row = LANES // HALF
    inv_freq = ROPE_THETA ** (-jnp.arange(HALF, dtype=F32) * (2.0 / HEAD_DIM))
    freq = jnp.tile(inv_freq, per_row).reshape(1, LANES)
    pos_rows = positions.reshape(b * s // per_row, per_row)
    grid_spec = pltpu.PrefetchScalarGridSpec(
        num_scalar_prefetch=1,
        grid=(n_tiles + 1,),
        in_specs=[
            pl.BlockSpec((1, t, d), tok_map(front_tile)),
            pl.BlockSpec((t // per_row, per_row), lambda step, *_: (front_tile(step), 0)),
            _const_spec((1, LANES)),
            _const_spec((N_MOD, b, d)),
            _const_spec((1, d)), _const_spec((1, d)), _const_spec((1, d)),
            _const_spec((1, POOL_WIDTH)),
            _const_spec(w_pool.shape),
        ] + [pl.BlockSpec(memory_space=pl.ANY)] * 5,
        out_specs=pl.BlockSpec((1, t, d), tok_map(back_tile)),
        scratch_shapes=[
            pltpu.VMEM((d, IN_PROJ_WIDTH), BF16),
            pltpu.VMEM((ATTN_WIDTH + POOL_WIDTH, d), BF16),
            pltpu.VMEM((d, d_ff), BF16),
            pltpu.VMEM((d, d_ff), BF16),
            pltpu.VMEM((d_ff, d), BF16),
            pltpu.VMEM((CAST_SLOTS, CAST_CHUNK_ROWS, max(d_ff, IN_PROJ_WIDTH)), F32),
            pltpu.SemaphoreType.DMA((CAST_SLOTS,)),
            pltpu.VMEM((t, LANES), F32),
            pltpu.VMEM((t, LANES), F32),
            pltpu.VMEM((t, d), BF16),
            pltpu.VMEM((t, ATTN_WIDTH), BF16),
            pltpu.VMEM((KV_WIDTH, t + WINDOW), BF16),
            pltpu.VMEM((t + WINDOW, 2 * KV_WIDTH), BF16),
            pltpu.VMEM((t + POOL_HALO, POOL_WIDTH), F32),
            pltpu.VMEM((t, ATTN_WIDTH + POOL_WIDTH), BF16),
            pltpu.VMEM((2, t, d), F32),
            pltpu.VMEM((t, d), BF16),
            pltpu.VMEM((t, d_ff), BF16),
        ],
    )
    return pl.pallas_call(
        functools.partial(_layer_kernel, tiles_per_seq=tiles_per_seq),
        grid_spec=grid_spec,
        out_shape=jax.ShapeDtypeStruct((b, s, d), F32),
        compiler_params=pltpu.CompilerParams(
            dimension_semantics=("arbitrary",), vmem_limit_bytes=VMEM_LIMIT_BYTES),
        name="layer",
    )(sinks, x, pos_rows, freq, mod, norm1.reshape(1, d), norm2.reshape(1, d), norm_f.reshape(1, d),
      pool_scale.reshape(1, -1), w_pool, w_in, w_out,
      w_gate, w_up, w_down)


def kernel(x, c, positions, w_ada, b_ada, norm1, w_in, sinks, w_pool, pool_scale,
           w_out, norm2, w_gate, w_up, w_down, norm_f):
    mod = _adaln_mod(c, w_ada, b_ada)
    return _layer(x, mod, positions, sinks, norm1, norm2, norm_f, w_in, w_pool, pool_scale,
                  w_out, w_gate, w_up, w_down)
```

```python
import functools
import math

import jax
import jax.numpy as jnp
from jax import lax
from jax.experimental import pallas as pl
from jax.experimental.pallas import tpu as pltpu

F32 = jnp.float32
BF16 = jnp.bfloat16

D_MODEL = 1024
HEAD_DIM = 64
N_HEADS = 8
N_KV_HEADS = 2
GROUP = N_HEADS // N_KV_HEADS
ATTN_WIDTH = N_HEADS * HEAD_DIM
KV_WIDTH = N_KV_HEADS * HEAD_DIM
POOL_WINDOWS = (2, 4, 8, 16)
POOL_GROUP_WIDTH = 128
POOL_WIDTH = POOL_GROUP_WIDTH * len(POOL_WINDOWS)
IN_PROJ_WIDTH = ATTN_WIDTH + 2 * KV_WIDTH + POOL_WIDTH
WINDOW = 128
ROPE_THETA = 10000.0
N_MOD = 6
RMS_EPS = 1e-6
LOG2_E = math.log2(math.e)
HALF = HEAD_DIM // 2

LANES = 128
POOL_HALO = 16
VMEM_LIMIT_BYTES = 56 * 1024 * 1024

TOKEN_TILE = 512
FF_CHUNK = 256
DOWN_CHUNK = 256
ATTN_SKEW = 3
CAST_CHUNK_ROWS = 128
CAST_SLOTS = 4
CAST_ROWS_PER_ITER = 16
FIRST_CHUNK_ROW_BLOCKS = 2
OUT_PROJ_LAG = 2
LATE_OUT_PROJ = 2
BACK_SPAN = 0.9


def _const_spec(shape):
    zeros = (0,) * len(shape)
    return pl.BlockSpec(shape, lambda *_: zeros, pipeline_mode=pl.Buffered(1))


def _rms_scale(x):
    return lax.rsqrt(jnp.mean(x * x, axis=-1, keepdims=True) + RMS_EPS)


def _interleave(front, back):
    keyed = [((k + 0.5) / len(front), 0, k, f) for k, f in enumerate(front)]
    keyed += [(BACK_SPAN * (k + 0.5) / len(back), 1, k, f) for k, f in enumerate(back)]
    return [f for *_, f in sorted(keyed, key=lambda e: e[:3])]


def _mod_kernel(ct_ref, w_ref, b_ref, o_ref):
    ct = ct_ref[...]
    sc = ct * jax.nn.sigmoid(ct)
    w = w_ref[...]
    for r in range(ct.shape[1]):
        o_ref[0, r:r + 1, :] = jnp.sum(w * sc[:, r:r + 1], axis=0, keepdims=True) + b_ref[...]


def _adaln_mod(c, w_ada, b_ada):
    b, d = c.shape
    return pl.pallas_call(
        _mod_kernel,
        grid=(N_MOD,),
        in_specs=[
            pl.BlockSpec((d, b), lambda j: (0, 0)),
            pl.BlockSpec((d, d), lambda j: (0, j)),
            pl.BlockSpec((1, d), lambda j: (0, j)),
        ],
        out_specs=pl.BlockSpec((1, b, d), lambda j: (j, 0, 0)),
        out_shape=jax.ShapeDtypeStruct((N_MOD, b, d), F32),
        compiler_params=pltpu.CompilerParams(dimension_semantics=("arbitrary",)),
        name="adaln_mod",
    )(c.T, w_ada, b_ada.reshape(1, -1))


def _dot_f32(a, b):
    ah = a.astype(BF16)
    al = (a - ah.astype(F32)).astype(BF16)
    bh = b.astype(BF16)
    bl = (b - bh.astype(F32)).astype(BF16)
    dot = functools.partial(jnp.dot, preferred_element_type=F32)
    return dot(ah, bh) + dot(ah, bl) + dot(al, bh)


def _weight_cast_tasks(triples, stage_ref, sems):
    chunks = [(src, dst, c, fold) for src, dst, fold in triples
              for c in range(src.shape[0] // CAST_CHUNK_ROWS)]
    slots = stage_ref.shape[0]

    def copy(k):
        src, _, c, _ = chunks[k]
        return pltpu.make_async_copy(
            src.at[pl.ds(c * CAST_CHUNK_ROWS, CAST_CHUNK_ROWS), :],
            stage_ref.at[k % slots, :, pl.ds(0, src.shape[1])],
            sems.at[k % slots])

    def start(k):
        copy(k).start(priority=k % 2)

    def prime():
        for k in range(min(slots - 1, len(chunks))):
            start(k)

    def cast(k):
        def task():
            src, dst, c, fold = chunks[k]
            if k + slots - 1 < len(chunks):
                start(k + slots - 1)
            copy(k).wait()
            folded = fold(c, stage_ref.at[k % slots, :, pl.ds(0, src.shape[1])]) if fold else None
            if folded is not None:
                dst[c * CAST_CHUNK_ROWS:(c + 1) * CAST_CHUNK_ROWS, :] = folded.astype(BF16)
                return

            def cast_rows(r, carry):
                off = pl.multiple_of(r * CAST_ROWS_PER_ITER, CAST_ROWS_PER_ITER)
                dst[pl.ds(c * CAST_CHUNK_ROWS + off, CAST_ROWS_PER_ITER), :] = (
                    stage_ref[k % slots, pl.ds(off, CAST_ROWS_PER_ITER), 0:src.shape[1]]
                    .astype(BF16))
                return carry

            lax.fori_loop(0, CAST_CHUNK_ROWS // CAST_ROWS_PER_ITER, cast_rows, 0)
        return task

    return prime, [cast(k) for k in range(len(chunks))]


def _layer_kernel(sinks_ref, x_ref, pos_ref, freq_ref, mod_ref, n1_ref, n2_ref, nf_ref,
                  pscale_ref, wpool_ref, win_hbm, wout_hbm, wg_hbm, wu_hbm, wd_hbm,
                  o_ref,
                  win_ref, wout_ref, wg_ref, wu_ref, wd_ref, stage_ref, cast_sems,
                  cos_ref, sin_ref, h_ref, q_ref, kt_ref, v_ref, ext_ref, mix_ref, x1_ref, h2_ref,
                  act_ref, *, tiles_per_seq):
    t = x_ref.shape[1]
    s = pl.program_id(0)
    n_tiles = pl.num_programs(0) - 1
    tile_f = jnp.minimum(s, n_tiles - 1)
    first = (tile_f % tiles_per_seq) == 0
    slot_f = s % 2
    slot_b = 1 - slot_f
    seq_f = tile_f // tiles_per_seq
    seq_b = jnp.maximum(s - 1, 0) // tiles_per_seq

    def mod_row(k, seq):
        return mod_ref[k, pl.ds(seq, 1), :]

    lane =lax.broadcasted_iota(jnp.int32, (1, LANES), 1)
    low_lanes = lane < HEAD_DIM
    first_half = (lane % HEAD_DIM) < HALF

    def f_norm():
        x = x_ref[0]
        scale = n1_ref[...] * (1.0 + mod_row(1, seq_f))
        h_ref[...] = (x * _rms_scale(x) * scale + mod_row(0, seq_f)).astype(BF16)

    def in_proj(cols):
        return jnp.dot(h_ref[...], win_ref[:, cols], preferred_element_type=F32)

    def f_trig():
        per_row = LANES // HALF
        pos = pos_ref[...].astype(F32)
        pos = jnp.concatenate([jnp.broadcast_to(pos[:, m:m + 1], (t // per_row, HALF))
                               for m in range(per_row)], axis=-1)
        ang = pos * freq_ref[...]
        cos, sin = jnp.cos(ang), jnp.sin(ang)
        for m in range(per_row):
            c32 = cos[:, m * HALF:(m + 1) * HALF]
            s32 = sin[:, m * HALF:(m + 1) * HALF]
            rows = pl.ds(m, t // per_row, stride=per_row)
            cos_ref[rows, :] = jnp.concatenate([c32, c32, c32, c32], axis=-1)
            sin_ref[rows, :] = jnp.concatenate([-s32, s32, -s32, s32], axis=-1)

    def rope(tile):
        partner = jnp.where(first_half,
                            pltpu.roll(tile, LANES - HALF, 1),
                            pltpu.roll(tile, HALF, 1))
        return tile * cos_ref[...] + partner * sin_ref[...]

    def f_q():
        u = in_proj(slice(0, ATTN_WIDTH))
        q_scale = LOG2_E / math.sqrt(HEAD_DIM)
        for j in range(ATTN_WIDTH // LANES):
            cols = slice(j * LANES, (j + 1) * LANES)
            q_ref[:, cols] = (rope(u[:, cols]) * q_scale).astype(BF16)

    def f_kv():
        kt_ref[:, 0:WINDOW] = kt_ref[:, t:]
        v_ref[0:WINDOW, :] = v_ref[t:, :]
        u = in_proj(slice(ATTN_WIDTH, ATTN_WIDTH + 2 * KV_WIDTH))
        kt_ref[:, WINDOW:] = rope(u[:, :KV_WIDTH]).T.astype(BF16)
        vv = u[:, KV_WIDTH:]
        for g in range(N_KV_HEADS):
            vg = vv[:, g * HEAD_DIM:(g + 1) * HEAD_DIM]
            cols = slice(g * LANES, (g + 1) * LANES)
            v_ref[WINDOW:, cols] = jnp.concatenate([vg, vg], axis=-1).astype(BF16)

    def f_up():
        halo = ext_ref[t:, :]
        ext_ref[0:POOL_HALO, :] = jnp.where(first, jnp.zeros_like(halo), halo)
        ext_ref[POOL_HALO:, :] = in_proj(slice(ATTN_WIDTH + 2 * KV_WIDTH, IN_PROJ_WIDTH))

    def f_pool(gi, w):
        def task():
            cols = slice(gi * POOL_GROUP_WIDTH, (gi + 1) * POOL_GROUP_WIDTH)
            pos_in_seq = ((tile_f % tiles_per_seq) * t
                          + lax.broadcasted_iota(jnp.int32, (t, 1), 0))
            acc = ext_ref[:, cols]
            span = 1
            while span < w:
                acc = acc + pltpu.roll(acc, span, 0)
                span *= 2
            tok = ext_ref[POOL_HALO:, cols]
            total = acc[POOL_HALO:]
            count = jnp.minimum(pos_in_seq + 1, w).astype(F32)
            mix_ref[:, ATTN_WIDTH + gi * POOL_GROUP_WIDTH:
                    ATTN_WIDTH + (gi + 1) * POOL_GROUP_WIDTH] = (total / count - tok).astype(BF16)
        return task

    n_blocks = t // WINDOW
    chunks = N_HEADS // 2
    bodies = [(j, chunk) for j in range(n_blocks) for chunk in range(chunks)]
    kv_cache, logits, probs = {}, {}, {}

    def band_mask(j):
        qi = lax.broadcasted_iota(jnp.int32, (WINDOW, 2 * WINDOW), 0)
        kj = lax.broadcasted_iota(jnp.int32, (WINDOW, 2 * WINDOW), 1)
        rel = kj - WINDOW - qi
        band = (rel <= 0) & (rel > -WINDOW)
        if j == 0:
            band = band & (kj >= jnp.where(first, WINDOW, 0))
        return band

    def block_diag_kv(j, g):
        if (j, g) not in kv_cache:
            keys = slice(j * WINDOW, (j + 2) * WINDOW)
            ktg = kt_ref[g * HEAD_DIM:(g + 1) * HEAD_DIM, keys]
            zk = jnp.zeros_like(ktg)
            k_bd = jnp.concatenate([jnp.concatenate([ktg, zk], axis=1),
                                    jnp.concatenate([zk, ktg], axis=1)], axis=0)
            vd = v_ref[keys, g * LANES:(g + 1) * LANES]
            zv = jnp.zeros_like(vd)
            v_bd = jnp.concatenate([jnp.where(low_lanes, vd, zv),
                                    jnp.where(low_lanes, zv, vd)], axis=0)
            kv_cache[(j, g)] = (k_bd, v_bd)
        return kv_cache[(j, g)]

    def scores(j, chunk):
        k_bd, _ = block_diag_kv(j, chunk // (GROUP // 2))
        qc = q_ref[j * WINDOW:(j + 1) * WINDOW, chunk * LANES:(chunk + 1) * LANES]
        return jnp.dot(qc, k_bd, preferred_element_type=F32)

    def softmax(j, chunk, lg):
        mask = band_mask(j)
        ps, inv_den = [], []
        for hh in range(2):
            sink = sinks_ref[2 * chunk + hh] * LOG2_E
            l = jnp.where(mask, lg[:, hh * 2 * WINDOW:(hh + 1) * 2 * WINDOW], -jnp.inf)
            m = jnp.maximum(jnp.max(l, axis=-1, keepdims=True), sink)
            p = jnp.exp2(l - m)
            den = jnp.sum(p, axis=-1, keepdims=True) + jnp.exp2(sink - m)
            ps.append(p.astype(BF16))
            inv_den.append(1.0 / den)
        return jnp.concatenate(ps, axis=-1), jnp.where(low_lanes, inv_den[0], inv_den[1])

    def values(j, chunk, p, inv_den):
        _, v_bd = block_diag_kv(j, chunk // (GROUP // 2))
        pv = jnp.dot(p, v_bd, preferred_element_type=F32)
        mix_ref[j * WINDOW:(j + 1) * WINDOW, chunk * LANES:(chunk + 1) * LANES] = (
            (pv * inv_den).astype(BF16))

    def out_proj(j):
        rows = slice(j * WINDOW, (j + 1) * WINDOW)
        mixed = jnp.dot(mix_ref[rows, :], wout_ref[...], preferred_element_type=F32)
        x1_ref[slot_f, rows, :] = x_ref[0, rows, :] + mod_row(2, seq_f) * mixed

    def f_attn(step):
        def task():
            n = len(bodies)
            if step < n:
                logits[step] = scores(*bodies[step])
            if 0 <= step - 1 < n:
                probs[step - 1] = softmax(*bodies[step - 1], logits.pop(step - 1))
            done = step - ATTN_SKEW
            if 0 <= done < n:
                values(*bodies[done], *probs.pop(done))
            ready = step - ATTN_SKEW - OUT_PROJ_LAG
            if (0 <= ready < n and bodies[ready][1] == chunks - 1
                    and bodies[ready][0] < n_blocks - LATE_OUT_PROJ):
                out_proj(bodies[ready][0])
        return task

    late_out_proj = [functools.partial(out_proj, j)
                     for j in range(n_blocks - LATE_OUT_PROJ, n_blocks)]

    front = [f_norm, f_trig, f_q, f_kv, f_up]
    front += [f_pool(gi, w) for gi, w in enumerate(POOL_WINDOWS)]
    front += [f_attn(step) for step in range(len(bodies) + ATTN_SKEW + OUT_PROJ_LAG)]

    def b_norm():
        x1 = x1_ref[slot_b]
        scale = n2_ref[...] * (1.0 + mod_row(4, seq_b))
        h2_ref[...] = (x1 * _rms_scale(x1) * scale + mod_row(3, seq_b)).astype(BF16)

    def b_gate_up(n):
        def task():
            cols = slice(n * FF_CHUNK, (n + 1) * FF_CHUNK)
            row_blocks = FIRST_CHUNK_ROW_BLOCKS if n == 0 else 1
            for rb in range(row_blocks):
                rows = slice(rb * t // row_blocks, (rb + 1) * t // row_blocks)
                g = jnp.dot(h2_ref[rows, :], wg_ref[:, cols], preferred_element_type=F32)
                u = jnp.dot(h2_ref[rows, :], wu_ref[:, cols], preferred_element_type=F32)
                act_ref[rows, cols] = (g * jax.nn.sigmoid(g) * u).astype(BF16)
        return task

    def b_down(n):
        def task():
            cols = slice(n * DOWN_CHUNK, (n + 1) * DOWN_CHUNK)
            ff = jnp.dot(act_ref[...], wd_ref[:, cols], preferred_element_type=F32)
            o_ref[0, :, cols] = x1_ref[slot_b, :, cols] + mod_row(5, seq_b)[:, cols] * ff
        return task

    def b_final():
        x2 = o_ref[0]
        o_ref[0] = x2 * _rms_scale(x2) * nf_ref[...]

    d_ff = wg_ref.shape[1]
    back = [b_norm]
    back += [b_gate_up(n) for n in range(d_ff // FF_CHUNK)]
    back += [b_down(n) for n in range(D_MODEL // DOWN_CHUNK)]
    back += [b_final]

    def run(tasks):
        kv_cache.clear(), logits.clear(), probs.clear()
        for task in tasks:
            task()

    @pl.when(s == 0)
    def _():
        kt_ref[:, t:] = jnp.zeros((KV_WIDTH, WINDOW), BF16)
        v_ref[t:, :] = jnp.zeros((WINDOW, 2 * KV_WIDTH), BF16)
        ext_ref[t:, :] = jnp.zeros((POOL_HALO, POOL_WIDTH), F32)

        def fold_pool(c, rows_ref):
            g = c - ATTN_WIDTH // CAST_CHUNK_ROWS
            if g < 0:
                return None
            cols = slice(g * POOL_GROUP_WIDTH, (g + 1) * POOL_GROUP_WIDTH)
            return _dot_f32(wpool_ref[g] * pscale_ref[:, cols], rows_ref[...])

        prime, casts = _weight_cast_tasks(
            [(win_hbm, win_ref, None), (wout_hbm, wout_ref, fold_pool),
             (wg_hbm, wg_ref, None), (wu_hbm, wu_ref, None), (wd_hbm, wd_ref, None)],
            stage_ref, cast_sems)
        front_casts = (win_hbm.shape[0] + wout_hbm.shape[0]) // CAST_CHUNK_ROWS
        prime()
        run(casts[:front_casts])
        run(_interleave(front, casts[front_casts:]) + late_out_proj)

    @pl.when(s > 0)
    def _():
        order = [task for task in _interleave(front, back) if task not in back[-2:]]
        run(order + [back[-2], late_out_proj[0], back[-1], late_out_proj[1]])


def _layer(x, mod, positions, sinks, norm1, norm2, norm_f, w_in, w_pool, pool_scale, w_out,
           w_gate, w_up, w_down):
    b, s, d = x.shape
    t = TOKEN_TILE
    d_ff = w_gate.shape[1]
    assert d == D_MODEL and s % t == 0 and t % WINDOW == 0 and d_ff % FF_CHUNK == 0
    assert d_ff % CAST_CHUNK_ROWS == 0 and d % CAST_CHUNK_ROWS == 0
    tiles_per_seq = s // t
    n_tiles = b * tiles_per_seq

    def front_tile(step):
        return jnp.minimum(step, n_tiles - 1)

    def back_tile(step):
        return jnp.maximum(step - 1, 0)

    def tok_map(tile_of):
        return lambda step, *_: (tile_of(step) // tiles_per_seq, tile_of(step) % tiles_per_seq, 0)

    per_row = LANES // HALF
    inv_freq = ROPE_THETA ** (-jnp.arange(HALF, dtype=F32) * (2.0 / HEAD_DIM))
    freq = jnp.tile(inv_freq, per_row).reshape(1, LANES)
    pos_rows = positions.reshape(b * s // per_row, per_row)
    grid_spec = pltpu.PrefetchScalarGridSpec(
        num_scalar_prefetch=1,
        grid=(n_tiles + 1,),
        in_specs=[
            pl.BlockSpec((1, t, d), tok_map(front_tile)),
            pl.BlockSpec((t // per_row, per_row), lambda step, *_: (front_tile(step), 0)),
            _const_spec((1, LANES)),
            _const_spec((N_MOD, b, d)),
            _const_spec((1, d)), _const_spec((1, d)), _const_spec((1, d)),
            _const_spec((1, POOL_WIDTH)),
            _const_spec(w_pool.shape),
        ] + [pl.BlockSpec(memory_space=pl.ANY)] * 5,
        out_specs=pl.BlockSpec((1, t, d), tok_map(back_tile)),
        scratch_shapes=[
            pltpu.VMEM((d, IN_PROJ_WIDTH), BF16),
            pltpu.VMEM((ATTN_WIDTH + POOL_WIDTH, d), BF16),
            pltpu.VMEM((d, d_ff), BF16),
            pltpu.VMEM((d, d_ff), BF16),
            pltpu.VMEM((d_ff, d), BF16),
            pltpu.VMEM((CAST_SLOTS, CAST_CHUNK_ROWS, max(d_ff, IN_PROJ_WIDTH)), F32),
            pltpu.SemaphoreType.DMA((CAST_SLOTS,)),
            pltpu.VMEM((t, LANES), F32),
            pltpu.VMEM((t, LANES), F32),
            pltpu.VMEM((t, d), BF16),
            pltpu.VMEM((t, ATTN_WIDTH), BF16),
            pltpu.VMEM((KV_WIDTH, t + WINDOW), BF16),
            pltpu.VMEM((t + WINDOW, 2 * KV_WIDTH), BF16),
            pltpu.VMEM((t + POOL_HALO, POOL_WIDTH), F32),
            pltpu.VMEM((t, ATTN_WIDTH + POOL_WIDTH), BF16),
            pltpu.VMEM((2, t, d), F32),
            pltpu.VMEM((t, d), BF16),
            pltpu.VMEM((t, d_ff), BF16),
        ],
    )
    return pl.pallas_call(
        functools.partial(_layer_kernel, tiles_per_seq=tiles_per_seq),
        grid_spec=grid_spec,
        out_shape=jax.ShapeDtypeStruct((b, s, d), F32),
        compiler_params=pltpu.CompilerParams(
            dimension_semantics=("arbitrary",), vmem_limit_bytes=VMEM_LIMIT_BYTES),
        name="layer",
    )(sinks, x, pos_rows, freq, mod, norm1.reshape(1, d), norm2.reshape(1, d), norm_f.reshape(1, d),
      pool_scale.reshape(1, -1), w_pool, w_in, w_out,
      w_gate, w_up, w_down)


def kernel(x, c, positions, w_ada, b_ada, norm1, w_in, sinks, w_pool, pool_scale,
           w_out, norm2, w_gate, w_up, w_down, norm_f):
    mod = _adaln_mod(c, w_ada, b_ada)
    return _layer(x, mod, positions, sinks, norm1, norm2, norm_f, w_in, w_pool, pool_scale,
                  w_out, w_gate, w_up, w_down)
```

```python
import functools
import math

import jax
import jax.numpy as jnp
from jax import lax
from jax.experimental import pallas as pl
from jax.experimental.pallas import tpu as pltpu

F32 = jnp.float32
BF16 = jnp.bfloat16

D_MODEL = 1024
HEAD_DIM = 64
N_HEADS = 8
N_KV_HEADS = 2
GROUP = N_HEADS // N_KV_HEADS
ATTN_WIDTH = N_HEADS * HEAD_DIM
KV_WIDTH = N_KV_HEADS * HEAD_DIM
POOL_WINDOWS = (2, 4, 8, 16)
POOL_GROUP_WIDTH = 128
POOL_WIDTH = POOL_GROUP_WIDTH * len(POOL_WINDOWS)
IN_PROJ_WIDTH = ATTN_WIDTH + 2 * KV_WIDTH + POOL_WIDTH
WINDOW = 128
ROPE_THETA = 10000.0
N_MOD = 6
RMS_EPS = 1e-6
LOG2_E = math.log2(math.e)
HALF = HEAD_DIM // 2

LANES = 128
POOL_HALO = 16
VMEM_LIMIT_BYTES = 56 * 1024 * 1024

TOKEN_TILE = 512
FF_CHUNK = 256
DOWN_CHUNK = 256
CAST_CHUNK_ROWS = 128
CAST_SLOTS = 4
CAST_ROWS_PER_ITER = 16
ATTN_SKEW = 3
OUT_PROJ_LAG = 2
LATE_OUT_PROJ = 2
FIRST_CHUNK_ROW_BLOCKS = 2
BACK_SPAN = 0.9


def _const_spec(shape):
    zeros = (0,) * len(shape)
    return pl.BlockSpec(shape, lambda *_: zeros, pipeline_mode=pl.Buffered(1))


def _rms_scale(x):
    return lax.rsqrt(jnp.mean(x * x, axis=-1, keepdims=True) + RMS_EPS)


def _interleave(front, back):
    keyed = [((k + 0.5) / len(front), 0, k, f) for k, f in enumerate(front)]
    keyed += [(BACK_SPAN * (k + 0.5) / len(back), 1, k, f) for k, f in enumerate(back)]
    return [f for *_, f in sorted(keyed, key=lambda e: e[:3])]


def _mod_kernel(ct_ref, w_ref, b_ref, o_ref):
    ct = ct_ref[...]
    sc = ct * jax.nn.sigmoid(ct)
    w = w_ref[...]
    for r in range(ct.shape[1]):
        o_ref[0, r:r + 1, :] = jnp.sum(w * sc[:, r:r + 1], axis=0, keepdims=True) + b_ref[...]


def _adaln_mod(c, w_ada, b_ada):
    b, d = c.shape
    return pl.pallas_call(
        _mod_kernel,
        grid=(N_MOD,),
        in_specs=[
            pl.BlockSpec((d, b), lambda j: (0, 0)),
            pl.BlockSpec((d, d), lambda j: (0, j)),
            pl.BlockSpec((1, d), lambda j: (0, j)),
        ],
        out_specs=pl.BlockSpec((1, b, d), lambda j: (j, 0, 0)),
        out_shape=jax.ShapeDtypeStruct((N_MOD, b, d), F32),
        compiler_params=pltpu.CompilerParams(dimension_semantics=("arbitrary",)),
        name="adaln_mod",
    )(c.T, w_ada, b_ada.reshape(1, -1))


def _dot_f32(a, b):
    ah = a.astype(BF16)
    al = (a - ah.astype(F32)).astype(BF16)
    bh = b.astype(BF16)
    bl = (b - bh.astype(F32)).astype(BF16)
    dot = functools.partial(jnp.dot, preferred_element_type=F32)
    return dot(ah, bh) + dot(ah, bl) + dot(al, bh)


def _weight_cast_tasks(triples, stage_ref, sems):
    chunks = [(src, dst, c, fold) for src, dst, fold in triples
              for c in range(src.shape[0] // CAST_CHUNK_ROWS)]
    slots = stage_ref.shape[0]

    def copy(k):
        src, _, c, _ = chunks[k]
        return pltpu.make_async_copy(
            src.at[pl.ds(c * CAST_CHUNK_ROWS, CAST_CHUNK_ROWS), :],
            stage_ref.at[k % slots, :, pl.ds(0, src.shape[1])],
            sems.at[k % slots])

    def prime():
        for k in range(min(slots - 1, len(chunks))):
            copy(k).start()

    def cast(k):
        def task():
            src, dst, c, fold = chunks[k]
            if k + slots - 1 < len(chunks):
                copy(k + slots - 1).start()
            copy(k).wait()
            folded = fold(c, stage_ref.at[k % slots, :, pl.ds(0, src.shape[1])]) if fold else None
            if folded is not None:
                dst[c * CAST_CHUNK_ROWS:(c + 1) * CAST_CHUNK_ROWS, :] = folded.astype(BF16)
                return

            def cast_rows(r, carry):
                off = pl.multiple_of(r * CAST_ROWS_PER_ITER, CAST_ROWS_PER_ITER)
                dst[pl.ds(c * CAST_CHUNK_ROWS + off, CAST_ROWS_PER_ITER), :] = (
                    stage_ref[k % slots, pl.ds(off, CAST_ROWS_PER_ITER), 0:src.shape[1]]
                    .astype(BF16))
                return carry

            lax.fori_loop(0, CAST_CHUNK_ROWS // CAST_ROWS_PER_ITER, cast_rows, 0)
        return task

    return prime, [cast(k) for k in range(len(chunks))]


def _layer_kernel(sinks_ref, x_ref, pos_ref, freq_ref, mod_ref, n1_ref, n2_ref, nf_ref,
                  pscale_ref, wpool_ref, win_hbm, wout_hbm, wg_hbm, wu_hbm, wd_hbm,
                  o_ref,
                  win_ref, wout_ref, wg_ref, wu_ref, wd_ref, stage_ref, cast_sems,
                  cos_ref, sin_ref, h_ref, q_ref, kt_ref, v_ref, ext_ref, mix_ref, x1_ref, h2_ref,
                  act_ref, *, tiles_per_seq):
    t = x_ref.shape[1]
    s = pl.program_id(0)
    n_tiles = pl.num_programs(0) - 1
    tile_f = jnp.minimum(s, n_tiles - 1)
    first = (tile_f % tiles_per_seq) == 0
    slot_f = s % 2
    slot_b = 1 - slot_f
    seq_f = tile_f // tiles_per_seq
    seq_b = jnp.maximum(s - 1, 0) // tiles_per_seq

    def mod_row(k, seq):
        return mod_ref[k, pl.ds(seq, 1), :]

    lane = lax.broadcasted_iota(jnp.int32, (1, LANES), 1)
    low_lanes = lane < HEAD_DIM
    first_half = (lane % HEAD_DIM) < HALF

    def f_norm():
        x = x_ref[0]
        scale = n1_ref[...] * (1.0 + mod_row(1, seq_f))
        h_ref[...] = (x * _rms_scale(x) * scale + mod_row(0, seq_f)).astype(BF16)

    def in_proj(cols):
        return jnp.dot(h_ref[...], win_ref[:, cols], preferred_element_type=F32)

    def f_trig():
        per_row = LANES // HALF
        pos = pos_ref[...].astype(F32)
        pos = jnp.concatenate([jnp.broadcast_to(pos[:, m:m + 1], (t // per_row, HALF))
                               for m in range(per_row)], axis=-1)
        ang = pos * freq_ref[...]
        cos, sin = jnp.cos(ang), jnp.sin(ang)
        for m in range(per_row):
            c32 = cos[:, m * HALF:(m + 1) * HALF]
            s32 = sin[:, m * HALF:(m + 1) * HALF]
            rows = pl.ds(m, t // per_row, stride=per_row)
            cos_ref[rows, :] = jnp.concatenate([c32, c32, c32, c32], axis=-1)
            sin_ref[rows, :] = jnp.concatenate([-s32, s32, -s32, s32], axis=-1)

    def rope(tile):
        partner = jnp.where(first_half,
                            pltpu.roll(tile, LANES - HALF, 1),
                            pltpu.roll(tile, HALF, 1))
        return tile * cos_ref[...] + partner * sin_ref[...]

    def f_q():
        u = in_proj(slice(0, ATTN_WIDTH))
        q_scale = LOG2_E / math.sqrt(HEAD_DIM)
        for j in range(ATTN_WIDTH // LANES):
            cols = slice(j * LANES, (j + 1) * LANES)
            q_ref[:, cols] = (rope(u[:, cols]) * q_scale).astype(BF16)

    def f_kv():
        kt_ref[:, 0:WINDOW] = kt_ref[:, t:]
        v_ref[0:WINDOW, :] = v_ref[t:, :]
        u = in_proj(slice(ATTN_WIDTH, ATTN_WIDTH + 2 * KV_WIDTH))
        kt_ref[:, WINDOW:] = rope(u[:, :KV_WIDTH]).T.astype(BF16)
        vv = u[:, KV_WIDTH:]
        for g in range(N_KV_HEADS):
            vg = vv[:, g * HEAD_DIM:(g + 1) * HEAD_DIM]
            cols = slice(g * LANES, (g + 1) * LANES)
            v_ref[WINDOW:, cols] = jnp.concatenate([vg, vg], axis=-1).astype(BF16)

    def f_up():
        halo = ext_ref[t:, :]
        ext_ref[0:POOL_HALO, :] = jnp.where(first, jnp.zeros_like(halo), halo)
        ext_ref[POOL_HALO:, :] = in_proj(slice(ATTN_WIDTH + 2 * KV_WIDTH, IN_PROJ_WIDTH))

    def f_pool(gi, w):
        def task():
            cols = slice(gi * POOL_GROUP_WIDTH, (gi + 1) * POOL_GROUP_WIDTH)
            pos_in_seq = ((tile_f % tiles_per_seq) * t
                          + lax.broadcasted_iota(jnp.int32, (t, 1), 0))
            acc = ext_ref[:, cols]
            span = 1
            while span < w:
                acc = acc + pltpu.roll(acc, span, 0)
                span *= 2
            tok = ext_ref[POOL_HALO:, cols]
            total = acc[POOL_HALO:]
            count = jnp.minimum(pos_in_seq + 1, w).astype(F32)
            mix_ref[:, ATTN_WIDTH + gi * POOL_GROUP_WIDTH:
                    ATTN_WIDTH + (gi + 1) * POOL_GROUP_WIDTH] = (total / count - tok).astype(BF16)
        return task

    n_blocks = t // WINDOW
    chunks = N_HEADS // 2
    bodies = [(j, chunk) for j in range(n_blocks) for chunk in range(chunks)]
    kv_cache, logits, probs = {}, {}, {}

    def band_mask(j):
        qi = lax.broadcasted_iota(jnp.int32, (WINDOW, 2 * WINDOW), 0)
        kj = lax.broadcasted_iota(jnp.int32, (WINDOW, 2 * WINDOW), 1)
        rel = kj - WINDOW - qi
        band = (rel <= 0) & (rel > -WINDOW)
        if j == 0:
            band = band & (kj >= jnp.where(first, WINDOW, 0))
        return band

    def block_diag_kv(j, g):
        if (j, g) not in kv_cache:
            keys = slice(j * WINDOW, (j + 2) * WINDOW)
            ktg = kt_ref[g * HEAD_DIM:(g + 1) * HEAD_DIM, keys]
            zk = jnp.zeros_like(ktg)
            k_bd = jnp.concatenate([jnp.concatenate([ktg, zk], axis=1),
                                    jnp.concatenate([zk, ktg], axis=1)], axis=0)
            vd = v_ref[keys, g * LANES:(g + 1) * LANES]
            zv = jnp.zeros_like(vd)
            v_bd = jnp.concatenate([jnp.where(low_lanes, vd, zv),
                                    jnp.where(low_lanes, zv, vd)], axis=0)
            kv_cache[(j, g)] = (k_bd, v_bd)
        return kv_cache[(j, g)]

    def scores(j, chunk):
        k_bd, _ = block_diag_kv(j, chunk // (GROUP // 2))
        qc = q_ref[j * WINDOW:(j + 1) * WINDOW, chunk * LANES:(chunk + 1) * LANES]
        return jnp.dot(qc, k_bd, preferred_element_type=F32)

    def softmax(j, chunk, lg):
        mask = band_mask(j)
        ps, inv_den = [], []
        for hh in range(2):
            sink = sinks_ref[2 * chunk + hh] * LOG2_E
            l = jnp.where(mask, lg[:, hh * 2 * WINDOW:(hh + 1) * 2 * WINDOW], -jnp.inf)
            m = jnp.maximum(jnp.max(l, axis=-1, keepdims=True), sink)
            p = jnp.exp2(l - m)
            den = jnp.sum(p, axis=-1, keepdims=True) + jnp.exp2(sink - m)
            ps.append(p.astype(BF16))
            inv_den.append(1.0 / den)
        return jnp.concatenate(ps, axis=-1), jnp.where(low_lanes, inv_den[0], inv_den[1])

    def values(j, chunk, p, inv_den):
        _, v_bd = block_diag_kv(j, chunk // (GROUP // 2))
        pv = jnp.dot(p, v_bd, preferred_element_type=F32)
        mix_ref[j * WINDOW:(j + 1) * WINDOW, chunk * LANES:(chunk + 1) * LANES] = (
            (pv * inv_den).astype(BF16))

    def out_proj(j):
        rows = slice(j * WINDOW, (j + 1) * WINDOW)
        mixed = jnp.dot(mix_ref[rows, :], wout_ref[...], preferred_element_type=F32)
        x1_ref[slot_f, rows, :] = x_ref[0, rows, :] + mod_row(2, seq_f) * mixed

    def f_attn(step):
        def task():
            n = len(bodies)
            if step < n:
                logits[step] = scores(*bodies[step])
            if 0 <= step - 1 < n:
                probs[step - 1] = softmax(*bodies[step - 1], logits.pop(step - 1))
            done = step - ATTN_SKEW
            if 0 <= done < n:
                values(*bodies[done], *probs.pop(done))
            ready = step - ATTN_SKEW - OUT_PROJ_LAG
            if (0 <= ready < n and bodies[ready][1] == chunks - 1
                    and bodies[ready][0] < n_blocks - LATE_OUT_PROJ):
                out_proj(bodies[ready][0])
        return task

    late_out_proj = [functools.partial(out_proj, j)
                     for j in range(n_blocks - LATE_OUT_PROJ, n_blocks)]

    front = [f_norm, f_trig, f_q, f_kv, f_up]
    front += [f_pool(gi, w) for gi, w in enumerate(POOL_WINDOWS)]
    front += [f_attn(step) for step in range(len(bodies) + ATTN_SKEW + OUT_PROJ_LAG)]

    def b_norm():
        x1 = x1_ref[slot_b]
        scale = n2_ref[...] * (1.0 + mod_row(4, seq_b))
        h2_ref[...] = (x1 * _rms_scale(x1) * scale + mod_row(3, seq_b)).astype(BF16)

    def b_gate_up(n):
        def task():
            cols = slice(n * FF_CHUNK, (n + 1) * FF_CHUNK)
            row_blocks = FIRST_CHUNK_ROW_BLOCKS if n == 0 else 1
            for rb in range(row_blocks):
                rows = slice(rb * t // row_blocks, (rb + 1) * t // row_blocks)
                g = jnp.dot(h2_ref[rows, :], wg_ref[:, cols], preferred_element_type=F32)
                u = jnp.dot(h2_ref[rows, :], wu_ref[:, cols], preferred_element_type=F32)
                act_ref[rows, cols] = (g * jax.nn.sigmoid(g) * u).astype(BF16)
        return task

    def b_down(n):
        def task():
            cols = slice(n * DOWN_CHUNK, (n + 1) * DOWN_CHUNK)
            ff = jnp.dot(act_ref[...], wd_ref[:, cols], preferred_element_type=F32)
            o_ref[0, :, cols] = x1_ref[slot_b, :, cols] + mod_row(5, seq_b)[:, cols] * ff
        return task

    def b_final():
        x2 = o_ref[0]
        o_ref[0] = x2 * _rms_scale(x2) * nf_ref[...]

    d_ff = wg_ref.shape[1]
    back = [b_norm]
    back += [b_gate_up(n) for n in range(d_ff // FF_CHUNK)]
    back += [b_down(n) for n in range(D_MODEL // DOWN_CHUNK)]
    back += [b_final]

    def run(tasks):
        kv_cache.clear(), logits.clear(), probs.clear()
        for task in tasks:
            task()

    @pl.when(s == 0)
    def _():
        kt_ref[:, t:] = jnp.zeros((KV_WIDTH, WINDOW), BF16)
        v_ref[t:, :] = jnp.zeros((WINDOW, 2 * KV_WIDTH), BF16)
        ext_ref[t:, :] = jnp.zeros((POOL_HALO, POOL_WIDTH), F32)

        def fold_pool(c, rows_ref):
            g = c - ATTN_WIDTH // CAST_CHUNK_ROWS
            if g < 0:
                return None
            cols = slice(g * POOL_GROUP_WIDTH, (g + 1) * POOL_GROUP_WIDTH)
            return _dot_f32(wpool_ref[g] * pscale_ref[:, cols], rows_ref[...])

        prime, casts = _weight_cast_tasks(
            [(win_hbm, win_ref, None), (wout_hbm, wout_ref, fold_pool),
             (wg_hbm, wg_ref, None), (wu_hbm, wu_ref, None), (wd_hbm, wd_ref, None)],
            stage_ref, cast_sems)
        front_casts = (win_hbm.shape[0] + wout_hbm.shape[0]) // CAST_CHUNK_ROWS
        prime()
        run(casts[:front_casts])
        run(_interleave(front, casts[front_casts:]) + late_out_proj)

    @pl.when(s > 0)
    def _():
        order = [task for task in _interleave(front, back) if task not in back[-2:]]
        run(order + [back[-2], late_out_proj[0], back[-1], late_out_proj[1]])


def _layer(x, mod, positions, sinks, norm1, norm2, norm_f, w_in, w_pool, pool_scale, w_out,
           w_gate, w_up, w_down):
    b, s, d = x.shape
    t = TOKEN_TILE
    d_ff = w_gate.shape[1]
    assert d == D_MODEL and s % t == 0 and t % WINDOW == 0 and d_ff % FF_CHUNK == 0
    assert d_ff % CAST_CHUNK_ROWS == 0 and d % CAST_CHUNK_ROWS == 0
    tiles_per_seq = s // t
    n_tiles = b * tiles_per_seq

    def front_tile(step):
        return jnp.minimum(step, n_tiles - 1)

    def back_tile(step):
        return jnp.maximum(step - 1, 0)

    def tok_map(tile_of):
        return lambda step, *_: (tile_of(step) // tiles_per_seq, tile_of(step) % tiles_per_seq, 0)

    per_row = LANES // HALF
    inv_freq = ROPE_THETA ** (-jnp.arange(HALF, dtype=F32) * (2.0 / HEAD_DIM))
    freq = jnp.tile(inv_freq, per_row).reshape(1, LANES)
    pos_rows = positions.reshape(b * s // per_row, per_row)
    grid_spec = pltpu.PrefetchScalarGridSpec(
        num_scalar_prefetch=1,
        grid=(n_tiles + 1,),
        in_specs=[
            pl.BlockSpec((1, t, d), tok_map(front_tile)),
            pl.BlockSpec((t // per_row, per_row), lambda step, *_: (front_tile(step), 0)),
            _const_spec((1, LANES)),
            _const_spec((N_MOD, b, d)),
            _const_spec((1, d)), _const_spec((1, d)), _const_spec((1, d)),
            _const_spec((1, POOL_WIDTH)),
            _const_spec(w_pool.shape),
        ] + [pl.BlockSpec(memory_space=pl.ANY)] * 5,
        out_specs=pl.BlockSpec((1, t, d), tok_map(back_tile)),
        scratch_shapes=[
            pltpu.VMEM((d, IN_PROJ_WIDTH), BF16),
            pltpu.VMEM((ATTN_WIDTH + POOL_WIDTH, d), BF16),
            pltpu.VMEM((d, d_ff), BF16),
            pltpu.VMEM((d, d_ff), BF16),
            pltpu.VMEM((d_ff, d), BF16),
            pltpu.VMEM((CAST_SLOTS, CAST_CHUNK_ROWS, max(d_ff, IN_PROJ_WIDTH)), F32),
            pltpu.SemaphoreType.DMA((CAST_SLOTS,)),
            pltpu.VMEM((t, LANES), F32),
            pltpu.VMEM((t, LANES), F32),
            pltpu.VMEM((t, d), BF16),
            pltpu.VMEM((t, ATTN_WIDTH), BF16),
            pltpu.VMEM((KV_WIDTH, t + WINDOW), BF16),
            pltpu.VMEM((t + WINDOW, 2 * KV_WIDTH), BF16),
            pltpu.VMEM((t + POOL_HALO, POOL_WIDTH), F32),
            pltpu.VMEM((t, ATTN_WIDTH + POOL_WIDTH), BF16),
            pltpu.VMEM((2, t, d), F32),
            pltpu.VMEM((t, d), BF16),
            pltpu.VMEM((t, d_ff), BF16),
        ],
    )
    return pl.pallas_call(
        functools.partial(_layer_kernel, tiles_per_seq=tiles_per_seq),
        grid_spec=grid_spec,
        out_shape=jax.ShapeDtypeStruct((b, s, d), F32),
        compiler_params=pltpu.CompilerParams(
            dimension_semantics=("arbitrary",), vmem_limit_bytes=VMEM_LIMIT_BYTES),
        name="layer",
    )(sinks, x, pos_rows, freq, mod, norm1.reshape(1, d), norm2.reshape(1, d), norm_f.reshape(1, d),
      pool_scale.reshape(1, -1), w_pool, w_in, w_out,
      w_gate, w_up, w_down)


def kernel(x, c, positions, w_ada, b_ada, norm1, w_in, sinks, w_pool, pool_scale,
           w_out, norm2, w_gate, w_up, w_down, norm_f):
    mod = _adaln_mod(c, w_ada, b_ada)
    return _layer(x, mod, positions, sinks, norm1, norm2, norm_f, w_in, w_pool, pool_scale,
                  w_out, w_gate, w_up, w_down)
```

```python
import functools
import math

import jax
import jax.numpy as jnp
from jax import lax
from jax.experimental import pallas as pl
from jax.experimental.pallas import tpu as pltpu

F32 = jnp.float32
BF16 = jnp.bfloat16

D_MODEL = 1024
HEAD_DIM = 64
N_HEADS = 8
N_KV_HEADS = 2
GROUP = N_HEADS // N_KV_HEADS
ATTN_WIDTH = N_HEADS * HEAD_DIM
KV_WIDTH = N_KV_HEADS * HEAD_DIM
POOL_WINDOWS = (2, 4, 8, 16)
POOL_GROUP_WIDTH = 128
POOL_WIDTH = POOL_GROUP_WIDTH * len(POOL_WINDOWS)
IN_PROJ_WIDTH = ATTN_WIDTH + 2 * KV_WIDTH + POOL_WIDTH
WINDOW = 128
ROPE_THETA = 10000.0
N_MOD = 6
RMS_EPS = 1e-6
LOG2_E = math.log2(math.e)
HALF = HEAD_DIM // 2

LANES = 128
POOL_HALO = 16
VMEM_LIMIT_BYTES = 56 * 1024 * 1024

TOKEN_TILE = 512
FF_CHUNK = 256
DOWN_CHUNK = 256
CAST_CHUNK_ROWS = 128
CAST_SLOTS = 4
CAST_ROWS_PER_ITER = 16
ATTN_SKEW = 3
OUT_PROJ_LAG = 2
LATE_OUT_PROJ = 2
FIRST_CHUNK_ROW_BLOCKS = 2
BACK_SPAN = 0.9


def _const_spec(shape):
    zeros = (0,) * len(shape)
    return pl.BlockSpec(shape, lambda *_: zeros, pipeline_mode=pl.Buffered(1))


def _rms_scale(x):
    return lax.rsqrt(jnp.mean(x * x, axis=-1, keepdims=True) + RMS_EPS)


def _interleave(front, back):
    keyed = [((k + 0.5) / len(front), 0, k, f) for k, f in enumerate(front)]
    keyed += [(BACK_SPAN * (k + 0.5) / len(back), 1, k, f) for k, f in enumerate(back)]
    return [f for *_, f in sorted(keyed, key=lambda e: e[:3])]


def _mod_kernel(c_ref, w_ref, b_ref, o_ref):
    c = c_ref[...]
    n_seq = c.shape[0]
    sc = c * jax.nn.sigmoid(c)
    sc = jnp.concatenate([sc, jnp.zeros((8 - n_seq, c.shape[1]), F32)], axis=0).T
    w = w_ref[...]
    for r in range(n_seq):
        o_ref[0, r:r + 1, :] = jnp.sum(w * sc[:, r:r + 1], axis=0, keepdims=True) + b_ref[...]


def _adaln_mod(c, w_ada, b_ada):
    b, d = c.shape
    return pl.pallas_call(
        _mod_kernel,
        grid=(N_MOD,),
        in_specs=[
            pl.BlockSpec((b, d), lambda j: (0, 0)),
            pl.BlockSpec((d, d), lambda j: (0, j)),
            pl.BlockSpec((1, d), lambda j: (0, j)),
        ],
        out_specs=pl.BlockSpec((1, b, d), lambda j: (j, 0, 0)),
        out_shape=jax.ShapeDtypeStruct((N_MOD, b, d), F32),
        compiler_params=pltpu.CompilerParams(dimension_semantics=("arbitrary",)),
        name="adaln_mod",
    )(c, w_ada, b_ada.reshape(1, -1))


def _dot_f32(a, b):
    ah = a.astype(BF16)
    al = (a - ah.astype(F32)).astype(BF16)
    bh = b.astype(BF16)
    bl = (b - bh.astype(F32)).astype(BF16)
    dot = functools.partial(jnp.dot, preferred_element_type=F32)
    return dot(ah, bh) + dot(ah, bl) + dot(al, bh)


def _weight_cast_tasks(triples, stage_ref, sems):
    chunks = [(src, dst, c, fold) for src, dst, fold in triples
              for c in range(src.shape[0] // CAST_CHUNK_ROWS)]
    slots = stage_ref.shape[0]

    def copy(k):
        src, _, c, _ = chunks[k]
        return pltpu.make_async_copy(
            src.at[pl.ds(c * CAST_CHUNK_ROWS, CAST_CHUNK_ROWS), :],
            stage_ref.at[k % slots, :, pl.ds(0, src.shape[1])],
            sems.at[k % slots])

    def prime():
        for k in range(min(slots - 1, len(chunks))):
            copy(k).start()

    def cast(k):
        def task():
            src, dst, c, fold = chunks[k]
            if k + slots - 1 < len(chunks):
                copy(k + slots - 1).start()
            copy(k).wait()
            folded = fold(c, stage_ref.at[k % slots, :, pl.ds(0, src.shape[1])]) if fold else None
            if folded is not None:
                dst[c * CAST_CHUNK_ROWS:(c + 1) * CAST_CHUNK_ROWS, :] = folded.astype(BF16)
                return

            def cast_rows(r, carry):
                off = pl.multiple_of(r * CAST_ROWS_PER_ITER, CAST_ROWS_PER_ITER)
                dst[pl.ds(c * CAST_CHUNK_ROWS + off, CAST_ROWS_PER_ITER), :] = (
                    stage_ref[k % slots, pl.ds(off, CAST_ROWS_PER_ITER), 0:src.shape[1]]
                    .astype(BF16))
                return carry

            lax.fori_loop(0, CAST_CHUNK_ROWS // CAST_ROWS_PER_ITER, cast_rows, 0)
        return task

    return prime, [cast(k) for k in range(len(chunks))]


def _layer_kernel(sinks_ref, x_ref, pos_ref, freq_ref, mod_ref, n1_ref, n2_ref, nf_ref,
                  pscale_ref, wpool_ref, win_hbm, wout_hbm, wg_hbm, wu_hbm, wd_hbm,
                  o_ref,
                  win_ref, wout_ref, wg_ref, wu_ref, wd_ref, stage_ref, cast_sems,
                  cos_ref, sin_ref, h_ref, q_ref, kt_ref, v_ref, ext_ref, mix_ref, x1_ref, h2_ref,
                  act_ref, *, tiles_per_seq):
    t = x_ref.shape[1]
    s = pl.program_id(0)
    n_tiles = pl.num_programs(0) - 1
    tile_f = jnp.minimum(s, n_tiles - 1)
    first = (tile_f % tiles_per_seq) == 0
    slot_f = s % 2
    slot_b = 1 - slot_f
    seq_f = tile_f // tiles_per_seq
    seq_b = jnp.maximum(s - 1, 0) // tiles_per_seq

    def mod_row(k, seq):
        return mod_ref[k, pl.ds(seq, 1), :]

    lane = lax.broadcasted_iota(jnp.int32, (1, LANES), 1)
    low_lanes = lane < HEAD_DIM
    first_half = (lane % HEAD_DIM) < HALF

    def f_norm():
        x = x_ref[0]
        scale = n1_ref[...] * (1.0 + mod_row(1, seq_f))
        h_ref[...] = (x * _rms_scale(x) * scale + mod_row(0, seq_f)).astype(BF16)

    def in_proj(cols):
        return jnp.dot(h_ref[...], win_ref[:, cols], preferred_element_type=F32)

    def f_trig():
        per_row = LANES // HALF
        pos = pos_ref[...].astype(F32)
        pos = jnp.concatenate([jnp.broadcast_to(pos[:, m:m + 1], (t // per_row, HALF))
                               for m in range(per_row)], axis=-1)
        ang = pos * freq_ref[...]
        cos, sin = jnp.cos(ang), jnp.sin(ang)
        for m in range(per_row):
            c32 = cos[:, m * HALF:(m + 1) * HALF]
            s32 = sin[:, m * HALF:(m + 1) * HALF]
            rows = pl.ds(m, t // per_row, stride=per_row)
            cos_ref[rows, :] = jnp.concatenate([c32, c32, c32, c32], axis=-1)
            sin_ref[rows, :] = jnp.concatenate([-s32, s32, -s32, s32], axis=-1)

    def rope(tile):
        partner = jnp.where(first_half,
                            pltpu.roll(tile, LANES - HALF, 1),
                            pltpu.roll(tile, HALF, 1))
        return tile * cos_ref[...] + partner * sin_ref[...]

    def f_q():
        u = in_proj(slice(0, ATTN_WIDTH))
        q_scale = LOG2_E / math.sqrt(HEAD_DIM)
        for j in range(ATTN_WIDTH // LANES):
            cols = slice(j * LANES, (j + 1) * LANES)
            q_ref[:, cols] = (rope(u[:, cols]) * q_scale).astype(BF16)

    def f_kv():
        kt_ref[:, 0:WINDOW] = kt_ref[:, t:]
        v_ref[0:WINDOW, :] = v_ref[t:, :]
        u = in_proj(slice(ATTN_WIDTH, ATTN_WIDTH + 2 * KV_WIDTH))
        kt_ref[:, WINDOW:] = rope(u[:, :KV_WIDTH]).T.astype(BF16)
        vv = u[:, KV_WIDTH:]
        for g in range(N_KV_HEADS):
            vg = vv[:, g * HEAD_DIM:(g + 1) * HEAD_DIM]
            cols = slice(g * LANES, (g + 1) * LANES)
            v_ref[WINDOW:, cols] = jnp.concatenate([vg, vg], axis=-1).astype(BF16)

    def f_up():
        halo = ext_ref[t:, :]
        ext_ref[0:POOL_HALO, :] = jnp.where(first, jnp.zeros_like(halo), halo)
        ext_ref[POOL_HALO:, :] = in_proj(slice(ATTN_WIDTH + 2 * KV_WIDTH, IN_PROJ_WIDTH))

    def f_pool(gi, w):
        def task():
            cols = slice(gi * POOL_GROUP_WIDTH, (gi + 1) * POOL_GROUP_WIDTH)
            pos_in_seq = ((tile_f % tiles_per_seq) * t
                          + lax.broadcasted_iota(jnp.int32, (t, 1), 0))
            acc = ext_ref[:, cols]
            span = 1
            while span < w:
                acc = acc + pltpu.roll(acc, span, 0)
                span *= 2
            tok = ext_ref[POOL_HALO:, cols]
            total = acc[POOL_HALO:]
            count = jnp.minimum(pos_in_seq + 1, w).astype(F32)
            mix_ref[:, ATTN_WIDTH + gi * POOL_GROUP_WIDTH:
                    ATTN_WIDTH + (gi + 1) * POOL_GROUP_WIDTH] = (total / count - tok).astype(BF16)
        return task

    n_blocks = t // WINDOW
    chunks = N_HEADS // 2
    bodies = [(j, chunk) for j in range(n_blocks) for chunk in range(chunks)]
    kv_cache, logits, probs = {}, {}, {}

    def band_mask(j):
        qi = lax.broadcasted_iota(jnp.int32, (WINDOW, 2 * WINDOW), 0)
        kj = lax.broadcasted_iota(jnp.int32, (WINDOW, 2 * WINDOW), 1)
        rel = kj - WINDOW - qi
        band = (rel <= 0) & (rel > -WINDOW)
        if j == 0:
            band = band & (kj >= jnp.where(first, WINDOW, 0))
        return band

    def block_diag_kv(j, g):
        if (j, g) not in kv_cache:
            keys = slice(j * WINDOW, (j + 2) * WINDOW)
            ktg = kt_ref[g * HEAD_DIM:(g + 1) * HEAD_DIM, keys]
            zk = jnp.zeros_like(ktg)
            k_bd = jnp.concatenate([jnp.concatenate([ktg, zk], axis=1),
                                    jnp.concatenate([zk, ktg], axis=1)], axis=0)
            vd = v_ref[keys, g * LANES:(g + 1) * LANES]
            zv = jnp.zeros_like(vd)
            v_bd = jnp.concatenate([jnp.where(low_lanes, vd, zv),
                                    jnp.where(low_lanes, zv, vd)], axis=0)
            kv_cache[(j, g)] = (k_bd, v_bd)
        return kv_cache[(j, g)]

    def scores(j, chunk):
        k_bd, _ = block_diag_kv(j, chunk // (GROUP // 2))
        qc = q_ref[j * WINDOW:(j + 1) * WINDOW, chunk * LANES:(chunk + 1) * LANES]
        return jnp.dot(qc, k_bd, preferred_element_type=F32)

    def softmax(j, chunk, lg):
        mask = band_mask(j)
        ps, inv_den = [], []
        for hh in range(2):
            sink = sinks_ref[2 * chunk + hh] * LOG2_E
            l = jnp.where(mask, lg[:, hh * 2 * WINDOW:(hh + 1) * 2 * WINDOW], -jnp.inf)
            m = jnp.maximum(jnp.max(l, axis=-1, keepdims=True), sink)
            p = jnp.exp2(l - m)
            den = jnp.sum(p, axis=-1, keepdims=True) + jnp.exp2(sink - m)
            ps.append(p.astype(BF16))
            inv_den.append(1.0 / den)
        return jnp.concatenate(ps, axis=-1), jnp.where(low_lanes, inv_den[0], inv_den[1])

    def values(j, chunk, p, inv_den):
        _, v_bd = block_diag_kv(j, chunk // (GROUP // 2))
        pv = jnp.dot(p, v_bd, preferred_element_type=F32)
        mix_ref[j * WINDOW:(j + 1) * WINDOW, chunk * LANES:(chunk + 1) * LANES] = (
            (pv * inv_den).astype(BF16))

    def out_proj(j):
        rows = slice(j * WINDOW, (j + 1) * WINDOW)
        mixed = jnp.dot(mix_ref[rows, :], wout_ref[...], preferred_element_type=F32)
        x1_ref[slot_f, rows, :] = x_ref[0, rows, :] + mod_row(2, seq_f) * mixed

    def f_attn(step):
        def task():
            n = len(bodies)
            if step < n:
                logits[step] = scores(*bodies[step])
            if 0 <= step - 1 < n:
                probs[step - 1] = softmax(*bodies[step - 1], logits.pop(step - 1))
            done = step - ATTN_SKEW
            if 0 <= done < n:
                values(*bodies[done], *probs.pop(done))
            ready = step - ATTN_SKEW - OUT_PROJ_LAG
            if (0 <= ready < n and bodies[ready][1] == chunks - 1
                    and bodies[ready][0] < n_blocks - LATE_OUT_PROJ):
                out_proj(bodies[ready][0])
        return task

    late_out_proj = [functools.partial(out_proj, j)
                     for j in range(n_blocks - LATE_OUT_PROJ, n_blocks)]

    front = [f_norm, f_trig, f_q, f_kv, f_up]
    front += [f_pool(gi, w) for gi, w in enumerate(POOL_WINDOWS)]
    front += [f_attn(step) for step in range(len(bodies) + ATTN_SKEW + OUT_PROJ_LAG)]

    def b_norm():
        x1 = x1_ref[slot_b]
        scale = n2_ref[...] * (1.0 + mod_row(4, seq_b))
        h2_ref[...] = (x1 * _rms_scale(x1) * scale + mod_row(3, seq_b)).astype(BF16)

    def b_gate_up(n):
        def task():
            cols = slice(n * FF_CHUNK, (n + 1) * FF_CHUNK)
            row_blocks = FIRST_CHUNK_ROW_BLOCKS if n == 0 else 1
            for rb in range(row_blocks):
                rows = slice(rb * t // row_blocks, (rb + 1) * t // row_blocks)
                g = jnp.dot(h2_ref[rows, :], wg_ref[:, cols], preferred_element_type=F32)
                u = jnp.dot(h2_ref[rows, :], wu_ref[:, cols], preferred_element_type=F32)
                act_ref[rows, cols] = (g * jax.nn.sigmoid(g) * u).astype(BF16)
        return task

    def b_down(n):
        def task():
            cols = slice(n * DOWN_CHUNK, (n + 1) * DOWN_CHUNK)
            ff = jnp.dot(act_ref[...], wd_ref[:, cols], preferred_element_type=F32)
            o_ref[0, :, cols] = x1_ref[slot_b, :, cols] + mod_row(5, seq_b)[:, cols] * ff
        return task

    def b_final():
        x2 = o_ref[0]
        o_ref[0] = x2 * _rms_scale(x2) * nf_ref[...]

    d_ff = wg_ref.shape[1]
    back = [b_norm]
    back += [b_gate_up(n) for n in range(d_ff // FF_CHUNK)]
    back += [b_down(n) for n in range(D_MODEL // DOWN_CHUNK)]
    back += [b_final]

    def run(tasks):
        kv_cache.clear(), logits.clear(), probs.clear()
        for task in tasks:
            task()

    @pl.when(s == 0)
    def _():
        kt_ref[:, t:] = jnp.zeros((KV_WIDTH, WINDOW), BF16)
        v_ref[t:, :] = jnp.zeros((WINDOW, 2 * KV_WIDTH), BF16)
        ext_ref[t:, :] = jnp.zeros((POOL_HALO, POOL_WIDTH), F32)

        def fold_pool(c, rows_ref):
            g = c - ATTN_WIDTH // CAST_CHUNK_ROWS
            if g < 0:
                return None
            cols = slice(g * POOL_GROUP_WIDTH, (g + 1) * POOL_GROUP_WIDTH)
            return _dot_f32(wpool_ref[g] * pscale_ref[:, cols], rows_ref[...])

        prime, casts = _weight_cast_tasks(
            [(win_hbm, win_ref, None), (wout_hbm, wout_ref, fold_pool),
             (wg_hbm, wg_ref, None), (wu_hbm, wu_ref, None), (wd_hbm, wd_ref, None)],
            stage_ref, cast_sems)
        front_casts = (win_hbm.shape[0] + wout_hbm.shape[0]) // CAST_CHUNK_ROWS
        prime()
        run(casts[:front_casts])
        run(_interleave(front, casts[front_casts:]) + late_out_proj)

    @pl.when(s > 0)
    def _():
        order = [task for task in _interleave(front, back) if task not in back[-2:]]
        run(order + [back[-2], late_out_proj[0], back[-1], late_out_proj[1]])


def _layer(x, mod, positions, sinks, norm1, norm2, norm_f, w_in, w_pool, pool_scale, w_out,
           w_gate, w_up, w_down):
    b, s, d = x.shape
    t = TOKEN_TILE
    d_ff = w_gate.shape[1]
    assert d == D_MODEL and s % t == 0 and t % WINDOW == 0 and d_ff % FF_CHUNK == 0
    assert d_ff % CAST_CHUNK_ROWS == 0 and d % CAST_CHUNK_ROWS == 0
    tiles_per_seq = s // t
    n_tiles = b * tiles_per_seq

    def front_tile(step):
        return jnp.minimum(step, n_tiles - 1)

    def back_tile(step):
        return jnp.maximum(step - 1, 0)

    def tok_map(tile_of):
        return lambda step, *_: (tile_of(step) // tiles_per_seq, tile_of(step) % tiles_per_seq, 0)

    per_row = LANES // HALF
    inv_freq = ROPE_THETA ** (-jnp.arange(HALF, dtype=F32) * (2.0 / HEAD_DIM))
    freq = jnp.tile(inv_freq, per_row).reshape(1, LANES)
    pos_rows = positions.reshape(b * s // per_row, per_row)
    grid_spec = pltpu.PrefetchScalarGridSpec(
        num_scalar_prefetch=1,
        grid=(n_tiles + 1,),
        in_specs=[
            pl.BlockSpec((1, t, d), tok_map(front_tile)),
            pl.BlockSpec((t // per_row, per_row), lambda step, *_: (front_tile(step), 0)),
            _const_spec((1, LANES)),
            _const_spec((N_MOD, b, d)),
            _const_spec((1, d)), _const_spec((1, d)), _const_spec((1, d)),
            _const_spec((1, POOL_WIDTH)),
            _const_spec(w_pool.shape),
        ] + [pl.BlockSpec(memory_space=pl.ANY)] * 5,
        out_specs=pl.BlockSpec((1, t, d), tok_map(back_tile)),
        scratch_shapes=[
            pltpu.VMEM((d, IN_PROJ_WIDTH), BF16),
            pltpu.VMEM((ATTN_WIDTH + POOL_WIDTH, d), BF16),
            pltpu.VMEM((d, d_ff), BF16),
            pltpu.VMEM((d, d_ff), BF16),
            pltpu.VMEM((d_ff, d), BF16),
            pltpu.VMEM((CAST_SLOTS, CAST_CHUNK_ROWS, max(d_ff, IN_PROJ_WIDTH)), F32),
            pltpu.SemaphoreType.DMA((CAST_SLOTS,)),
            pltpu.VMEM((t, LANES), F32),
            pltpu.VMEM((t, LANES), F32),
            pltpu.VMEM((t, d), BF16),
            pltpu.VMEM((t, ATTN_WIDTH), BF16),
            pltpu.VMEM((KV_WIDTH, t + WINDOW), BF16),
            pltpu.VMEM((t + WINDOW, 2 * KV_WIDTH), BF16),
            pltpu.VMEM((t + POOL_HALO, POOL_WIDTH), F32),
            pltpu.VMEM((t, ATTN_WIDTH + POOL_WIDTH), BF16),
            pltpu.VMEM((2, t, d), F32),
            pltpu.VMEM((t, d), BF16),
            pltpu.VMEM((t, d_ff), BF16),
        ],
    )
    return pl.pallas_call(
        functools.partial(_layer_kernel, tiles_per_seq=tiles_per_seq),
        grid_spec=grid_spec,
        out_shape=jax.ShapeDtypeStruct((b, s, d), F32),
        compiler_params=pltpu.CompilerParams(
            dimension_semantics=("arbitrary",), vmem_limit_bytes=VMEM_LIMIT_BYTES),
        name="layer",
    )(sinks, x, pos_rows, freq, mod, norm1.reshape(1, d), norm2.reshape(1, d), norm_f.reshape(1, d),
      pool_scale.reshape(1, -1), w_pool, w_in, w_out,
      w_gate, w_up, w_down)


def kernel(x, c, positions, w_ada, b_ada, norm1, w_in, sinks, w_pool, pool_scale,
           w_out, norm2, w_gate, w_up, w_down, norm_f):
    mod = _adaln_mod(c, w_ada, b_ada)
    return _layer(x, mod, positions, sinks, norm1, norm2, norm_f, w_in, w_pool, pool_scale,
                  w_out, w_gate, w_up, w_down)
```

```python
import functools
import math

import jax
import jax.numpy as jnp
from jax import lax
from jax.experimental import pallas as pl
from jax.experimental.pallas import tpu as pltpu

F32 = jnp.float32
BF16 = jnp.bfloat16

D_MODEL = 1024
HEAD_DIM = 64
N_HEADS = 8
N_KV_HEADS = 2
GROUP = N_HEADS // N_KV_HEADS
ATTN_WIDTH = N_HEADS * HEAD_DIM
KV_WIDTH = N_KV_HEADS * HEAD_DIM
POOL_WINDOWS = (2, 4, 8, 16)
POOL_GROUP_WIDTH = 128
POOL_WIDTH = POOL_GROUP_WIDTH * len(POOL_WINDOWS)
IN_PROJ_WIDTH = ATTN_WIDTH + 2 * KV_WIDTH + POOL_WIDTH
WINDOW = 128
ROPE_THETA = 10000.0
N_MOD = 6
RMS_EPS = 1e-6
LOG2_E = math.log2(math.e)
HALF = HEAD_DIM // 2

LANES = 128
POOL_HALO = 16
VMEM_LIMIT_BYTES = 56 * 1024 * 1024

TOKEN_TILE = 512
FF_CHUNK = 256
DOWN_CHUNK = 256
CAST_CHUNK_ROWS = 128
CAST_SLOTS = 4
CAST_ROWS_PER_ITER = 16
ATTN_SKEW = 3
OUT_PROJ_LAG = 2
OUT_PROJ_BLOCKS = 2
FIRST_CHUNK_ROW_BLOCKS = 2
BACK_SPAN = 0.9


def _const_spec(shape):
    zeros = (0,) * len(shape)
    return pl.BlockSpec(shape, lambda *_: zeros, pipeline_mode=pl.Buffered(1))


def _rms_scale(x):
    return lax.rsqrt(jnp.mean(x * x, axis=-1, keepdims=True) + RMS_EPS)


def _interleave(front, back):
    keyed = [((k + 0.5) / len(front), 0, k, f) for k, f in enumerate(front)]
    keyed += [(BACK_SPAN * (k + 0.5) / len(back), 1, k, f) for k, f in enumerate(back)]
    return [f for *_, f in sorted(keyed, key=lambda e: e[:3])]


def _mod_kernel(c_ref, w_ref, b_ref, o_ref):
    c = c_ref[...]
    n_seq = c.shape[0]
    sc = c * jax.nn.sigmoid(c)
    sc = jnp.concatenate([sc, jnp.zeros((8 - n_seq, c.shape[1]), F32)], axis=0).T
    w = w_ref[...]
    for r in range(n_seq):
        o_ref[0, r:r + 1, :] = jnp.sum(w * sc[:, r:r + 1], axis=0, keepdims=True) + b_ref[...]


def _adaln_mod(c, w_ada, b_ada):
    b, d = c.shape
    return pl.pallas_call(
        _mod_kernel,
        grid=(N_MOD,),
        in_specs=[
            pl.BlockSpec((b, d), lambda j: (0, 0)),
            pl.BlockSpec((d, d), lambda j: (0, j)),
            pl.BlockSpec((1, d), lambda j: (0, j)),
        ],
        out_specs=pl.BlockSpec((1, b, d), lambda j: (j, 0, 0)),
        out_shape=jax.ShapeDtypeStruct((N_MOD, b, d), F32),
        compiler_params=pltpu.CompilerParams(dimension_semantics=("arbitrary",)),
        name="adaln_mod",
    )(c, w_ada, b_ada.reshape(1, -1))


def _dot_f32(a, b):
    ah = a.astype(BF16)
    al = (a - ah.astype(F32)).astype(BF16)
    bh = b.astype(BF16)
    bl = (b - bh.astype(F32)).astype(BF16)
    dot = functools.partial(jnp.dot, preferred_element_type=F32)
    return dot(ah, bh) + dot(ah, bl) + dot(al, bh)


def _weight_cast_tasks(triples, stage_ref, sems):
    chunks = [(src, dst, c, fold) for src, dst, fold in triples
              for c in range(src.shape[0] // CAST_CHUNK_ROWS)]
    slots = stage_ref.shape[0]

    def copy(k):
        src, _, c, _ = chunks[k]
        return pltpu.make_async_copy(
            src.at[pl.ds(c * CAST_CHUNK_ROWS, CAST_CHUNK_ROWS), :],
            stage_ref.at[k % slots, :, pl.ds(0, src.shape[1])],
            sems.at[k % slots])

    def prime():
        for k in range(min(slots - 1, len(chunks))):
            copy(k).start()

    def cast(k):
        def task():
            src, dst, c, fold = chunks[k]
            if k + slots - 1 < len(chunks):
                copy(k + slots - 1).start()
            copy(k).wait()
            folded = fold(c, stage_ref.at[k % slots, :, pl.ds(0, src.shape[1])]) if fold else None
            if folded is not None:
                dst[c * CAST_CHUNK_ROWS:(c + 1) * CAST_CHUNK_ROWS, :] = folded.astype(BF16)
                return

            def cast_rows(r, carry):
                off = pl.multiple_of(r * CAST_ROWS_PER_ITER, CAST_ROWS_PER_ITER)
                dst[pl.ds(c * CAST_CHUNK_ROWS + off, CAST_ROWS_PER_ITER), :] = (
                    stage_ref[k % slots, pl.ds(off, CAST_ROWS_PER_ITER), 0:src.shape[1]]
                    .astype(BF16))
                return carry

            lax.fori_loop(0, CAST_CHUNK_ROWS // CAST_ROWS_PER_ITER, cast_rows, 0)
        return task

    return prime, [cast(k) for k in range(len(chunks))]


def _layer_kernel(sinks_ref, x_ref, pos_ref, freq_ref, mod_ref, n1_ref, n2_ref, nf_ref,
                  pscale_ref, wpool_ref, win_hbm, wout_hbm, wg_hbm, wu_hbm, wd_hbm,
                  o_ref,
                  win_ref, wout_ref, wg_ref, wu_ref, wd_ref, stage_ref, cast_sems,
                  cos_ref, sin_ref, h_ref, q_ref, kt_ref, v_ref, ext_ref, mix_ref, x1_ref, h2_ref,
                  act_ref, *, tiles_per_seq):
    t = x_ref.shape[1]
    s = pl.program_id(0)
    n_tiles = pl.num_programs(0) - 1
    tile_f = jnp.minimum(s, n_tiles - 1)
    first = (tile_f % tiles_per_seq) == 0
    slot_f = s % 2
    slot_b = 1 - slot_f
    seq_f = tile_f // tiles_per_seq
    seq_b = jnp.maximum(s - 1, 0) // tiles_per_seq

    def mod_row(k, seq):
        return mod_ref[k, pl.ds(seq, 1), :]

    lane = lax.broadcasted_iota(jnp.int32, (1, LANES), 1)
    low_lanes = lane < HEAD_DIM
    first_half = (lane % HEAD_DIM) < HALF

    def f_norm():
        x = x_ref[0]
        scale = n1_ref[...] * (1.0 + mod_row(1, seq_f))
        h_ref[...] = (x * _rms_scale(x) * scale + mod_row(0, seq_f)).astype(BF16)

    def in_proj(cols):
        return jnp.dot(h_ref[...], win_ref[:, cols], preferred_element_type=F32)

    def f_trig():
        per_row = LANES // HALF
        pos = pos_ref[...].astype(F32)
        pos = jnp.concatenate([jnp.broadcast_to(pos[:, m:m + 1], (t // per_row, HALF))
                               for m in range(per_row)], axis=-1)
        ang = pos * freq_ref[...]
        cos, sin = jnp.cos(ang), jnp.sin(ang)
        for m in range(per_row):
            c32 = cos[:, m * HALF:(m + 1) * HALF]
            s32 = sin[:, m * HALF:(m + 1) * HALF]
            rows = pl.ds(m, t // per_row, stride=per_row)
            cos_ref[rows, :] = jnp.concatenate([c32, c32, c32, c32], axis=-1)
            sin_ref[rows, :] = jnp.concatenate([-s32, s32, -s32, s32], axis=-1)

    def rope(tile):
        partner = jnp.where(first_half,
                            pltpu.roll(tile, LANES - HALF, 1),
                            pltpu.roll(tile, HALF, 1))
        return tile * cos_ref[...] + partner * sin_ref[...]

    def f_q():
        u = in_proj(slice(0, ATTN_WIDTH))
        q_scale = LOG2_E / math.sqrt(HEAD_DIM)
        for j in range(ATTN_WIDTH // LANES):
            cols = slice(j * LANES, (j + 1) * LANES)
            q_ref[:, cols] = (rope(u[:, cols]) * q_scale).astype(BF16)

    def f_kv():
        kt_ref[:, 0:WINDOW] = kt_ref[:, t:]
        v_ref[0:WINDOW, :] = v_ref[t:, :]
        u = in_proj(slice(ATTN_WIDTH, ATTN_WIDTH + 2 * KV_WIDTH))
        kt_ref[:, WINDOW:] = rope(u[:, :KV_WIDTH]).T.astype(BF16)
        vv = u[:, KV_WIDTH:]
        for g in range(N_KV_HEADS):
            vg = vv[:, g * HEAD_DIM:(g + 1) * HEAD_DIM]
            cols = slice(g * LANES, (g + 1) * LANES)
            v_ref[WINDOW:, cols] = jnp.concatenate([vg, vg], axis=-1).astype(BF16)

    def f_up():
        halo = ext_ref[t:, :]
        ext_ref[0:POOL_HALO, :] = jnp.where(first, jnp.zeros_like(halo), halo)
        ext_ref[POOL_HALO:, :] = in_proj(slice(ATTN_WIDTH + 2 * KV_WIDTH, IN_PROJ_WIDTH))

    def f_pool(gi, w):
        def task():
            cols = slice(gi * POOL_GROUP_WIDTH, (gi + 1) * POOL_GROUP_WIDTH)
            pos_in_seq = ((tile_f % tiles_per_seq) * t
                          + lax.broadcasted_iota(jnp.int32, (t, 1), 0))
            acc = ext_ref[:, cols]
            span = 1
            while span < w:
                acc = acc + pltpu.roll(acc, span, 0)
                span *= 2
            tok = ext_ref[POOL_HALO:, cols]
            total = acc[POOL_HALO:]
            count = jnp.minimum(pos_in_seq + 1, w).astype(F32)
            mix_ref[:, ATTN_WIDTH + gi * POOL_GROUP_WIDTH:
                    ATTN_WIDTH + (gi + 1) * POOL_GROUP_WIDTH] = (total / count - tok).astype(BF16)
        return task

    n_blocks = t // WINDOW
    chunks = N_HEADS // 2
    bodies = [(j, chunk) for j in range(n_blocks) for chunk in range(chunks)]
    kv_cache, logits, probs = {}, {}, {}

    def band_mask(j):
        qi = lax.broadcasted_iota(jnp.int32, (WINDOW, 2 * WINDOW), 0)
        kj = lax.broadcasted_iota(jnp.int32, (WINDOW, 2 * WINDOW), 1)
        rel = kj - WINDOW - qi
        band = (rel <= 0) & (rel > -WINDOW)
        if j == 0:
            band = band & (kj >= jnp.where(first, WINDOW, 0))
        return band

    def block_diag_kv(j, g):
        if (j, g) not in kv_cache:
            keys = slice(j * WINDOW, (j + 2) * WINDOW)
            ktg = kt_ref[g * HEAD_DIM:(g + 1) * HEAD_DIM, keys]
            zk = jnp.zeros_like(ktg)
            k_bd = jnp.concatenate([jnp.concatenate([ktg, zk], axis=1),
                                    jnp.concatenate([zk, ktg], axis=1)], axis=0)
            vd = v_ref[keys, g * LANES:(g + 1) * LANES]
            zv = jnp.zeros_like(vd)
            v_bd = jnp.concatenate([jnp.where(low_lanes, vd, zv),
                                    jnp.where(low_lanes, zv, vd)], axis=0)
            kv_cache[(j, g)] = (k_bd, v_bd)
        return kv_cache[(j, g)]

    def scores(j, chunk):
        k_bd, _ = block_diag_kv(j, chunk // (GROUP // 2))
        qc = q_ref[j * WINDOW:(j + 1) * WINDOW, chunk * LANES:(chunk + 1) * LANES]
        return jnp.dot(qc, k_bd, preferred_element_type=F32)

    def softmax(j, chunk, lg):
        mask = band_mask(j)
        ps, inv_den = [], []
        for hh in range(2):
            sink = sinks_ref[2 * chunk + hh] * LOG2_E
            l = jnp.where(mask, lg[:, hh * 2 * WINDOW:(hh + 1) * 2 * WINDOW], -jnp.inf)
            m = jnp.maximum(jnp.max(l, axis=-1, keepdims=True), sink)
            p = jnp.exp2(l - m)
            den = jnp.sum(p, axis=-1, keepdims=True) + jnp.exp2(sink - m)
            ps.append(p.astype(BF16))
            inv_den.append(1.0 / den)
        return jnp.concatenate(ps, axis=-1), jnp.where(low_lanes, inv_den[0], inv_den[1])

    def values(j, chunk, p, inv_den):
        _, v_bd = block_diag_kv(j, chunk // (GROUP // 2))
        pv = jnp.dot(p, v_bd, preferred_element_type=F32)
        mix_ref[j * WINDOW:(j + 1) * WINDOW, chunk * LANES:(chunk + 1) * LANES] = (
            (pv * inv_den).astype(BF16))

    def out_proj(group):
        rows = slice(group * OUT_PROJ_BLOCKS * WINDOW, (group + 1) * OUT_PROJ_BLOCKS * WINDOW)
        mixed = jnp.dot(mix_ref[rows, :], wout_ref[...], preferred_element_type=F32)
        x1_ref[slot_f, rows, :] = x_ref[0, rows, :] + mod_row(2, seq_f) * mixed

    def f_attn(step):
        def task():
            n = len(bodies)
            if step < n:
                logits[step] = scores(*bodies[step])
            if 0 <= step - 1 < n:
                probs[step - 1] = softmax(*bodies[step - 1], logits.pop(step - 1))
            done = step - ATTN_SKEW
            if 0 <= done < n:
                values(*bodies[done], *probs.pop(done))
            ready = step - ATTN_SKEW - OUT_PROJ_LAG
            if (0 <= ready < n and bodies[ready][1] == chunks - 1
                    and (bodies[ready][0] + 1) % OUT_PROJ_BLOCKS == 0
                    and bodies[ready][0] < n_blocks - OUT_PROJ_BLOCKS):
                out_proj(bodies[ready][0] // OUT_PROJ_BLOCKS)
        return task

    late_out_proj = [functools.partial(out_proj, n_blocks // OUT_PROJ_BLOCKS - 1)]

    front = [f_norm, f_trig, f_q, f_kv, f_up]
    front += [f_pool(gi, w) for gi, w in enumerate(POOL_WINDOWS)]
    front += [f_attn(step) for step in range(len(bodies) + ATTN_SKEW + OUT_PROJ_LAG)]

    def b_norm():
        x1 = x1_ref[slot_b]
        scale = n2_ref[...] * (1.0 + mod_row(4, seq_b))
        h2_ref[...] = (x1 * _rms_scale(x1) * scale + mod_row(3, seq_b)).astype(BF16)

    def b_gate_up(n):
        def task():
            cols = slice(n * FF_CHUNK, (n + 1) * FF_CHUNK)
            row_blocks = FIRST_CHUNK_ROW_BLOCKS if n == 0 else 1
            for rb in range(row_blocks):
                rows = slice(rb * t // row_blocks, (rb + 1) * t // row_blocks)
                g = jnp.dot(h2_ref[rows, :], wg_ref[:, cols], preferred_element_type=F32)
                u = jnp.dot(h2_ref[rows, :], wu_ref[:, cols], preferred_element_type=F32)
                act_ref[rows, cols] = (g * jax.nn.sigmoid(g) * u).astype(BF16)
        return task

    def b_down(n):
        def task():
            cols = slice(n * DOWN_CHUNK, (n + 1) * DOWN_CHUNK)
            ff = jnp.dot(act_ref[...], wd_ref[:, cols], preferred_element_type=F32)
            o_ref[0, :, cols] = x1_ref[slot_b, :, cols] + mod_row(5, seq_b)[:, cols] * ff
        return task

    def b_final():
        x2 = o_ref[0]
        o_ref[0] = x2 * _rms_scale(x2) * nf_ref[...]

    d_ff = wg_ref.shape[1]
    back = [b_norm]
    back += [b_gate_up(n) for n in range(d_ff // FF_CHUNK)]
    back += [b_down(n) for n in range(D_MODEL // DOWN_CHUNK)]
    back += [b_final]

    def run(tasks):
        kv_cache.clear(), logits.clear(), probs.clear()
        for task in tasks:
            task()

    @pl.when(s == 0)
    def _():
        kt_ref[:, t:] = jnp.zeros((KV_WIDTH, WINDOW), BF16)
        v_ref[t:, :] = jnp.zeros((WINDOW, 2 * KV_WIDTH), BF16)
        ext_ref[t:, :] = jnp.zeros((POOL_HALO, POOL_WIDTH), F32)

        def fold_pool(c, rows_ref):
            g = c - ATTN_WIDTH // CAST_CHUNK_ROWS
            if g < 0:
                return None
            cols = slice(g * POOL_GROUP_WIDTH, (g + 1) * POOL_GROUP_WIDTH)
            return _dot_f32(wpool_ref[g] * pscale_ref[:, cols], rows_ref[...])

        prime, casts = _weight_cast_tasks(
            [(win_hbm, win_ref, None), (wout_hbm, wout_ref, fold_pool),
             (wg_hbm, wg_ref, None), (wu_hbm, wu_ref, None), (wd_hbm, wd_ref, None)],
            stage_ref, cast_sems)
        front_casts = (win_hbm.shape[0] + wout_hbm.shape[0]) // CAST_CHUNK_ROWS
        prime()
        run(casts[:front_casts])
        run(_interleave(front, casts[front_casts:]) + late_out_proj)

    @pl.when(s > 0)
    def _():
        order = [task for task in _interleave(front, back) if task not in back[-2:]]
        run(order + [back[-2], back[-1], late_out_proj[0]])


def _layer(x, mod, positions, sinks, norm1, norm2, norm_f, w_in, w_pool, pool_scale, w_out,
           w_gate, w_up, w_down):
    b, s, d = x.shape
    t = TOKEN_TILE
    d_ff = w_gate.shape[1]
    assert d == D_MODEL and s % t == 0 and t % WINDOW == 0 and d_ff % FF_CHUNK == 0
    assert d_ff % CAST_CHUNK_ROWS == 0 and d % CAST_CHUNK_ROWS == 0
    tiles_per_seq = s // t
    n_tiles = b * tiles_per_seq

    def front_tile(step):
        return jnp.minimum(step, n_tiles - 1)

    def back_tile(step):
        return jnp.maximum(step - 1, 0)

    def tok_map(tile_of):
        return lambda step, *_: (tile_of(step) // tiles_per_seq, tile_of(step) % tiles_per_seq, 0)

    per_row = LANES // HALF
    inv_freq = ROPE_THETA ** (-jnp.arange(HALF, dtype=F32) * (2.0 / HEAD_DIM))
    freq = jnp.tile(inv_freq, per_row).reshape(1, LANES)
    pos_rows = positions.reshape(b * s // per_row, per_row)
    grid_spec = pltpu.PrefetchScalarGridSpec(
        num_scalar_prefetch=1,
        grid=(n_tiles + 1,),
        in_specs=[
            pl.BlockSpec((1, t, d), tok_map(front_tile)),
            pl.BlockSpec((t // per_row, per_row), lambda step, *_: (front_tile(step), 0)),
            _const_spec((1, LANES)),
            _const_spec((N_MOD, b, d)),
            _const_spec((1, d)), _const_spec((1, d)), _const_spec((1, d)),
            _const_spec((1, POOL_WIDTH)),
            _const_spec(w_pool.shape),
        ] + [pl.BlockSpec(memory_space=pl.ANY)] * 5,
        out_specs=pl.BlockSpec((1, t, d), tok_map(back_tile)),
        scratch_shapes=[
            pltpu.VMEM((d, IN_PROJ_WIDTH), BF16),
            pltpu.VMEM((ATTN_WIDTH + POOL_WIDTH, d), BF16),
            pltpu.VMEM((d, d_ff), BF16),
            pltpu.VMEM((d, d_ff), BF16),
            pltpu.VMEM((d_ff, d), BF16),
            pltpu.VMEM((CAST_SLOTS, CAST_CHUNK_ROWS, max(d_ff, IN_PROJ_WIDTH)), F32),
            pltpu.SemaphoreType.DMA((CAST_SLOTS,)),
            pltpu.VMEM((t, LANES), F32),
            pltpu.VMEM((t, LANES), F32),
            pltpu.VMEM((t, d), BF16),
            pltpu.VMEM((t, ATTN_WIDTH), BF16),
            pltpu.VMEM((KV_WIDTH, t + WINDOW), BF16),
            pltpu.VMEM((t + WINDOW, 2 * KV_WIDTH), BF16),
            pltpu.VMEM((t + POOL_HALO, POOL_WIDTH), F32),
            pltpu.VMEM((t, ATTN_WIDTH + POOL_WIDTH), BF16),
            pltpu.VMEM((2, t, d), F32),
            pltpu.VMEM((t, d), BF16),
            pltpu.VMEM((t, d_ff), BF16),
        ],
    )
    return pl.pallas_call(
        functools.partial(_layer_kernel, tiles_per_seq=tiles_per_seq),
        grid_spec=grid_spec,
        out_shape=jax.ShapeDtypeStruct((b, s, d), F32),
        compiler_params=pltpu.CompilerParams(
            dimension_semantics=("arbitrary",), vmem_limit_bytes=VMEM_LIMIT_BYTES),
        name="layer",
    )(sinks, x, pos_rows, freq, mod, norm1.reshape(1, d), norm2.reshape(1, d), norm_f.reshape(1, d),
      pool_scale.reshape(1, -1), w_pool, w_in, w_out,
      w_gate, w_up, w_down)


def kernel(x, c, positions, w_ada, b_ada, norm1, w_in, sinks, w_pool, pool_scale,
           w_out, norm2, w_gate, w_up, w_down, norm_f):
    mod = _adaln_mod(c, w_ada, b_ada)
    return _layer(x, mod, positions, sinks, norm1, norm2, norm_f, w_in, w_pool, pool_scale,
                  w_out, w_gate, w_up, w_down)
```

```python
import functools
import math

import jax
import jax.numpy as jnp
from jax import lax
from jax.experimental import pallas as pl
from jax.experimental.pallas import tpu as pltpu

F32 = jnp.float32
BF16 = jnp.bfloat16

D_MODEL = 1024
HEAD_DIM = 64
N_HEADS = 8
N_KV_HEADS = 2
GROUP = N_HEADS // N_KV_HEADS
ATTN_WIDTH = N_HEADS * HEAD_DIM
KV_WIDTH = N_KV_HEADS * HEAD_DIM
POOL_WINDOWS = (2, 4, 8, 16)
POOL_GROUP_WIDTH = 128
POOL_WIDTH = POOL_GROUP_WIDTH * len(POOL_WINDOWS)
IN_PROJ_WIDTH = ATTN_WIDTH + 2 * KV_WIDTH + POOL_WIDTH
WINDOW = 128
ROPE_THETA = 10000.0
N_MOD = 6
RMS_EPS = 1e-6
LOG2_E = math.log2(math.e)
HALF = HEAD_DIM // 2

LANES = 128
POOL_HALO = 16
VMEM_LIMIT_BYTES = 56 * 1024 * 1024

TOKEN_TILE = 512
FF_CHUNK = 256
DOWN_CHUNK = 256
CAST_CHUNK_ROWS = 128
CAST_SLOTS = 6
CAST_ROWS_PER_ITER = 16
ATTN_SKEW = 3
OUT_PROJ_LAG = 2
OUT_PROJ_BLOCKS = 2
FIRST_CHUNK_ROW_BLOCKS = 2
BACK_SPAN = 0.9


def _const_spec(shape):
    zeros = (0,) * len(shape)
    return pl.BlockSpec(shape, lambda *_: zeros, pipeline_mode=pl.Buffered(1))


def _rms_scale(x):
    return lax.rsqrt(jnp.mean(x * x, axis=-1, keepdims=True) + RMS_EPS)


def _interleave(front, back):
    keyed = [((k + 0.5) / len(front), 0, k, f) for k, f in enumerate(front)]
    keyed += [(BACK_SPAN * (k + 0.5) / len(back), 1, k, f) for k, f in enumerate(back)]
    return [f for *_, f in sorted(keyed, key=lambda e: e[:3])]


def _mod_kernel(c_ref, w_ref, b_ref, o_ref):
    c = c_ref[...]
    n_seq = c.shape[0]
    sc = c * jax.nn.sigmoid(c)
    sc = jnp.concatenate([sc, jnp.zeros((8 - n_seq, c.shape[1]), F32)], axis=0).T
    w = w_ref[...]
    for r in range(n_seq):
        o_ref[0, r:r + 1, :] = jnp.sum(w * sc[:, r:r + 1], axis=0, keepdims=True) + b_ref[...]


def _adaln_mod(c, w_ada, b_ada):
    b, d = c.shape
    return pl.pallas_call(
        _mod_kernel,
        grid=(N_MOD,),
        in_specs=[
            pl.BlockSpec((b, d), lambda j: (0, 0)),
            pl.BlockSpec((d, d), lambda j: (0, j)),
            pl.BlockSpec((1, d), lambda j: (0, j)),
        ],
        out_specs=pl.BlockSpec((1, b, d), lambda j: (j, 0, 0)),
        out_shape=jax.ShapeDtypeStruct((N_MOD, b, d), F32),
        compiler_params=pltpu.CompilerParams(dimension_semantics=("arbitrary",)),
        name="adaln_mod",
    )(c, w_ada, b_ada.reshape(1, -1))


def _dot_f32(a, b):
    ah = a.astype(BF16)
    al = (a - ah.astype(F32)).astype(BF16)
    bh = b.astype(BF16)
    bl = (b - bh.astype(F32)).astype(BF16)
    dot = functools.partial(jnp.dot, preferred_element_type=F32)
    return dot(ah, bh) + dot(ah, bl) + dot(al, bh)


def _weight_cast_tasks(triples, stage_ref, sems):
    chunks = [(src, dst, c, fold) for src, dst, fold in triples
              for c in range(src.shape[0] // CAST_CHUNK_ROWS)]
    slots = stage_ref.shape[0]

    def copy(k):
        src, _, c, _ = chunks[k]
        return pltpu.make_async_copy(
            src.at[pl.ds(c * CAST_CHUNK_ROWS, CAST_CHUNK_ROWS), :],
            stage_ref.at[k % slots, :, pl.ds(0, src.shape[1])],
            sems.at[k % slots])

    def prime():
        for k in range(min(slots - 1, len(chunks))):
            copy(k).start()

    def cast(k):
        def task():
            src, dst, c, fold = chunks[k]
            if k + slots - 1 < len(chunks):
                copy(k + slots - 1).start()
            copy(k).wait()
            folded = fold(c, stage_ref.at[k % slots, :, pl.ds(0, src.shape[1])]) if fold else None
            if folded is not None:
                dst[c * CAST_CHUNK_ROWS:(c + 1) * CAST_CHUNK_ROWS, :] = folded.astype(BF16)
                return

            def cast_rows(r, carry):
                off = pl.multiple_of(r * CAST_ROWS_PER_ITER, CAST_ROWS_PER_ITER)
                dst[pl.ds(c * CAST_CHUNK_ROWS + off, CAST_ROWS_PER_ITER), :] = (
                    stage_ref[k % slots, pl.ds(off, CAST_ROWS_PER_ITER), 0:src.shape[1]]
                    .astype(BF16))
                return carry

            lax.fori_loop(0, CAST_CHUNK_ROWS // CAST_ROWS_PER_ITER, cast_rows, 0)
        return task

    return prime, [cast(k) for k in range(len(chunks))]


def _layer_kernel(sinks_ref, x_ref, pos_ref, freq_ref, mod_ref, n1_ref, n2_ref, nf_ref,
                  pscale_ref, wpool_ref, win_hbm, wout_hbm, wg_hbm, wu_hbm, wd_hbm,
                  o_ref,
                  win_ref, wout_ref, wg_ref, wu_ref, wd_ref, stage_ref, cast_sems,
                  cos_ref, sin_ref, h_ref, q_ref, kt_ref, v_ref, ext_ref, mix_ref, x1_ref, h2_ref,
                  act_ref, *, tiles_per_seq):
    t = x_ref.shape[1]
    s = pl.program_id(0)
    n_tiles = pl.num_programs(0) - 1
    tile_f = jnp.minimum(s, n_tiles - 1)
    first = (tile_f % tiles_per_seq) == 0
    slot_f = s % 2
    slot_b = 1 - slot_f
    seq_f = tile_f // tiles_per_seq
    seq_b = jnp.maximum(s - 1, 0) // tiles_per_seq

    def mod_row(k, seq):
        return mod_ref[k, pl.ds(seq, 1), :]

    lane = lax.broadcasted_iota(jnp.int32, (1, LANES), 1)
    low_lanes = lane < HEAD_DIM
    first_half = (lane % HEAD_DIM) < HALF

    def f_norm():
        x = x_ref[0]
        scale = n1_ref[...] * (1.0 + mod_row(1, seq_f))
        h_ref[...] = (x * _rms_scale(x) * scale + mod_row(0, seq_f)).astype(BF16)

    def in_proj(cols):
        return jnp.dot(h_ref[...], win_ref[:, cols], preferred_element_type=F32)

    def f_trig():
        per_row = LANES // HALF
        pos = pos_ref[...].astype(F32)
        pos = jnp.concatenate([jnp.broadcast_to(pos[:, m:m + 1], (t // per_row, HALF))
                               for m in range(per_row)], axis=-1)
        ang = pos * freq_ref[...]
        cos, sin = jnp.cos(ang), jnp.sin(ang)
        for m in range(per_row):
            c32 = cos[:, m * HALF:(m + 1) * HALF]
            s32 = sin[:, m * HALF:(m + 1) * HALF]
            rows = pl.ds(m, t // per_row, stride=per_row)
            cos_ref[rows, :] = jnp.concatenate([c32, c32, c32, c32], axis=-1)
            sin_ref[rows, :] = jnp.concatenate([-s32, s32, -s32, s32], axis=-1)

    def rope(tile):
        partner = jnp.where(first_half,
                            pltpu.roll(tile, LANES - HALF, 1),
                            pltpu.roll(tile, HALF, 1))
        return tile * cos_ref[...] + partner * sin_ref[...]

    def f_q():
        u = in_proj(slice(0, ATTN_WIDTH))
        q_scale = LOG2_E / math.sqrt(HEAD_DIM)
        for j in range(ATTN_WIDTH // LANES):
            cols = slice(j * LANES, (j + 1) * LANES)
            q_ref[:, cols] = (rope(u[:, cols]) * q_scale).astype(BF16)

    def f_kv():
        kt_ref[:, 0:WINDOW] = kt_ref[:, t:]
        v_ref[0:WINDOW, :] = v_ref[t:, :]
        u = in_proj(slice(ATTN_WIDTH, ATTN_WIDTH + 2 * KV_WIDTH))
        kt_ref[:, WINDOW:] = rope(u[:, :KV_WIDTH]).T.astype(BF16)
        vv = u[:, KV_WIDTH:]
        for g in range(N_KV_HEADS):
            vg = vv[:, g * HEAD_DIM:(g + 1) * HEAD_DIM]
            cols = slice(g * LANES, (g + 1) * LANES)
            v_ref[WINDOW:, cols] = jnp.concatenate([vg, vg], axis=-1).astype(BF16)

    def f_up():
        halo = ext_ref[t:, :]
        ext_ref[0:POOL_HALO, :] = jnp.where(first, jnp.zeros_like(halo), halo)
        ext_ref[POOL_HALO:, :] = in_proj(slice(ATTN_WIDTH + 2 * KV_WIDTH, IN_PROJ_WIDTH))

    def f_pool(gi, w):
        def task():
            cols = slice(gi * POOL_GROUP_WIDTH, (gi + 1) * POOL_GROUP_WIDTH)
            pos_in_seq = ((tile_f % tiles_per_seq) * t
                          + lax.broadcasted_iota(jnp.int32, (t, 1), 0))
            acc = ext_ref[:, cols]
            span = 1
            while span < w:
                acc = acc + pltpu.roll(acc, span, 0)
                span *= 2
            tok = ext_ref[POOL_HALO:, cols]
            total = acc[POOL_HALO:]
            count = jnp.minimum(pos_in_seq + 1, w).astype(F32)
            mix_ref[:, ATTN_WIDTH + gi * POOL_GROUP_WIDTH:
                    ATTN_WIDTH + (gi + 1) * POOL_GROUP_WIDTH] = (total / count - tok).astype(BF16)
        return task

    n_blocks = t // WINDOW
    chunks = N_HEADS // 2
    bodies = [(j, chunk) for j in range(n_blocks) for chunk in range(chunks)]
    kv_cache, logits, probs = {}, {}, {}

    def band_mask(j):
        qi = lax.broadcasted_iota(jnp.int32, (WINDOW, 2 * WINDOW), 0)
        kj = lax.broadcasted_iota(jnp.int32, (WINDOW, 2 * WINDOW), 1)
        rel = kj - WINDOW - qi
        band = (rel <= 0) & (rel > -WINDOW)
        if j == 0:
            band = band & (kj >= jnp.where(first, WINDOW, 0))
        return band

    def block_diag_kv(j, g):
        if (j, g) not in kv_cache:
            keys = slice(j * WINDOW, (j + 2) * WINDOW)
            ktg = kt_ref[g * HEAD_DIM:(g + 1) * HEAD_DIM, keys]
            zk = jnp.zeros_like(ktg)
            k_bd = jnp.concatenate([jnp.concatenate([ktg, zk], axis=1),
                                    jnp.concatenate([zk, ktg], axis=1)], axis=0)
            vd = v_ref[keys, g * LANES:(g + 1) * LANES]
            zv = jnp.zeros_like(vd)
            v_bd = jnp.concatenate([jnp.where(low_lanes, vd, zv),
                                    jnp.where(low_lanes, zv, vd)], axis=0)
            kv_cache[(j, g)] = (k_bd, v_bd)
        return kv_cache[(j, g)]

    def scores(j, chunk):
        k_bd, _ = block_diag_kv(j, chunk // (GROUP // 2))
        qc = q_ref[j * WINDOW:(j + 1) * WINDOW, chunk * LANES:(chunk + 1) * LANES]
        return jnp.dot(qc, k_bd, preferred_element_type=F32)

    def softmax(j, chunk, lg):
        mask = band_mask(j)
        ps, inv_den = [], []
        for hh in range(2):
            sink = sinks_ref[2 * chunk + hh] * LOG2_E
            l = jnp.where(mask, lg[:, hh * 2 * WINDOW:(hh + 1) * 2 * WINDOW], -jnp.inf)
            m = jnp.maximum(jnp.max(l, axis=-1, keepdims=True), sink)
            p = jnp.exp2(l - m)
            den = jnp.sum(p, axis=-1, keepdims=True) + jnp.exp2(sink - m)
            ps.append(p.astype(BF16))
            inv_den.append(1.0 / den)
        return jnp.concatenate(ps, axis=-1), jnp.where(low_lanes, inv_den[0], inv_den[1])

    def values(j, chunk, p, inv_den):
        _, v_bd = block_diag_kv(j, chunk // (GROUP // 2))
        pv = jnp.dot(p, v_bd, preferred_element_type=F32)
        mix_ref[j * WINDOW:(j + 1) * WINDOW, chunk * LANES:(chunk + 1) * LANES] = (
            (pv * inv_den).astype(BF16))

    def out_proj(group):
        rows = slice(group * OUT_PROJ_BLOCKS * WINDOW, (group + 1) * OUT_PROJ_BLOCKS * WINDOW)
        mixed = jnp.dot(mix_ref[rows, :], wout_ref[...], preferred_element_type=F32)
        x1_ref[slot_f, rows, :] = x_ref[0, rows, :] + mod_row(2, seq_f) * mixed

    def f_attn(step):
        def task():
            n = len(bodies)
            if step < n:
                logits[step] = scores(*bodies[step])
            if 0 <= step - 1 < n:
                probs[step - 1] = softmax(*bodies[step - 1], logits.pop(step - 1))
            done = step - ATTN_SKEW
            if 0 <= done < n:
                values(*bodies[done], *probs.pop(done))
            ready = step - ATTN_SKEW - OUT_PROJ_LAG
            if (0 <= ready < n and bodies[ready][1] == chunks - 1
                    and (bodies[ready][0] + 1) % OUT_PROJ_BLOCKS == 0
                    and bodies[ready][0] < n_blocks - OUT_PROJ_BLOCKS):
                out_proj(bodies[ready][0] // OUT_PROJ_BLOCKS)
        return task

    late_out_proj = [functools.partial(out_proj, n_blocks // OUT_PROJ_BLOCKS - 1)]

    front = [f_norm, f_trig, f_q, f_kv, f_up]
    front += [f_pool(gi, w) for gi, w in enumerate(POOL_WINDOWS)]
    front += [f_attn(step) for step in range(len(bodies) + ATTN_SKEW + OUT_PROJ_LAG)]

    def b_norm():
        x1 = x1_ref[slot_b]
        scale = n2_ref[...] * (1.0 + mod_row(4, seq_b))
        h2_ref[...] = (x1 * _rms_scale(x1) * scale + mod_row(3, seq_b)).astype(BF16)

    def b_gate_up(n):
        def task():
            cols = slice(n * FF_CHUNK, (n + 1) * FF_CHUNK)
            row_blocks = FIRST_CHUNK_ROW_BLOCKS if n == 0 else 1
            for rb in range(row_blocks):
                rows = slice(rb * t // row_blocks, (rb + 1) * t // row_blocks)
                g = jnp.dot(h2_ref[rows, :], wg_ref[:, cols], preferred_element_type=F32)
                u = jnp.dot(h2_ref[rows, :], wu_ref[:, cols], preferred_element_type=F32)
                act_ref[rows, cols] = (g * jax.nn.sigmoid(g) * u).astype(BF16)
        return task

    def b_down(n):
        def task():
            cols = slice(n * DOWN_CHUNK, (n + 1) * DOWN_CHUNK)
            ff = jnp.dot(act_ref[...], wd_ref[:, cols], preferred_element_type=F32)
            o_ref[0, :, cols] = x1_ref[slot_b, :, cols] + mod_row(5, seq_b)[:, cols] * ff
        return task

    def b_final():
        x2 = o_ref[0]
        o_ref[0] = x2 * _rms_scale(x2) * nf_ref[...]

    d_ff = wg_ref.shape[1]
    back = [b_norm]
    back += [b_gate_up(n) for n in range(d_ff // FF_CHUNK)]
    back += [b_down(n) for n in range(D_MODEL // DOWN_CHUNK)]
    back += [b_final]

    def run(tasks):
        kv_cache.clear(), logits.clear(), probs.clear()
        for task in tasks:
            task()

    @pl.when(s == 0)
    def _():
        kt_ref[:, t:] = jnp.zeros((KV_WIDTH, WINDOW), BF16)
        v_ref[t:, :] = jnp.zeros((WINDOW, 2 * KV_WIDTH), BF16)
        ext_ref[t:, :] = jnp.zeros((POOL_HALO, POOL_WIDTH), F32)

        def fold_pool(c, rows_ref):
            g = c - ATTN_WIDTH // CAST_CHUNK_ROWS
            if g < 0:
                return None
            cols = slice(g * POOL_GROUP_WIDTH, (g + 1) * POOL_GROUP_WIDTH)
            return _dot_f32(wpool_ref[g] * pscale_ref[:, cols], rows_ref[...])

        prime, casts = _weight_cast_tasks(
            [(win_hbm, win_ref, None), (wout_hbm, wout_ref, fold_pool),
             (wg_hbm, wg_ref, None), (wu_hbm, wu_ref, None), (wd_hbm, wd_ref, None)],
            stage_ref, cast_sems)
        front_casts = (win_hbm.shape[0] + wout_hbm.shape[0]) // CAST_CHUNK_ROWS
        prime()
        run(casts[:front_casts])
        run(_interleave(front, casts[front_casts:]) + late_out_proj)

    @pl.when(s > 0)
    def _():
        order = [task for task in _interleave(front, back) if task not in back[-2:]]
        run(order + [back[-2], back[-1], late_out_proj[0]])


def _layer(x, mod, positions, sinks, norm1, norm2, norm_f, w_in, w_pool, pool_scale, w_out,
           w_gate, w_up, w_down):
    b, s, d = x.shape
    t = TOKEN_TILE
    d_ff = w_gate.shape[1]
    assert d == D_MODEL and s % t == 0 and t % WINDOW == 0 and d_ff % FF_CHUNK == 0
    assert d_ff % CAST_CHUNK_ROWS == 0 and d % CAST_CHUNK_ROWS == 0
    assert CAST_CHUNK_ROWS == POOL_GROUP_WIDTH
    tiles_per_seq = s // t
    n_tiles = b * tiles_per_seq

    def front_tile(step):
        return jnp.minimum(step, n_tiles - 1)

    def back_tile(step):
        return jnp.maximum(step - 1, 0)

    def tok_map(tile_of):
        return lambda step, *_: (tile_of(step) // tiles_per_seq, tile_of(step) % tiles_per_seq, 0)

    per_row = LANES // HALF
    inv_freq = ROPE_THETA ** (-jnp.arange(HALF, dtype=F32) * (2.0 / HEAD_DIM))
    freq = jnp.tile(inv_freq, per_row).reshape(1, LANES)
    pos_rows = positions.reshape(b * s // per_row, per_row)
    grid_spec = pltpu.PrefetchScalarGridSpec(
        num_scalar_prefetch=1,
        grid=(n_tiles + 1,),
        in_specs=[
            pl.BlockSpec((1, t, d), tok_map(front_tile)),
            pl.BlockSpec((t // per_row, per_row), lambda step, *_: (front_tile(step), 0)),
            _const_spec((1, LANES)),
            _const_spec((N_MOD, b, d)),
            _const_spec((1, d)), _const_spec((1, d)), _const_spec((1, d)),
            _const_spec((1, POOL_WIDTH)),
            _const_spec(w_pool.shape),
        ] + [pl.BlockSpec(memory_space=pl.ANY)] * 5,
        out_specs=pl.BlockSpec((1, t, d), tok_map(back_tile)),
        scratch_shapes=[
            pltpu.VMEM((d, IN_PROJ_WIDTH), BF16),
            pltpu.VMEM((ATTN_WIDTH + POOL_WIDTH, d), BF16),
            pltpu.VMEM((d, d_ff), BF16),
            pltpu.VMEM((d, d_ff), BF16),
            pltpu.VMEM((d_ff, d), BF16),
            pltpu.VMEM((CAST_SLOTS, CAST_CHUNK_ROWS, max(d_ff, IN_PROJ_WIDTH)), F32),
            pltpu.SemaphoreType.DMA((CAST_SLOTS,)),
            pltpu.VMEM((t, LANES), F32),
            pltpu.VMEM((t, LANES), F32),
            pltpu.VMEM((t, d), BF16),
            pltpu.VMEM((t, ATTN_WIDTH), BF16),
            pltpu.VMEM((KV_WIDTH, t + WINDOW), BF16),
            pltpu.VMEM((t + WINDOW, 2 * KV_WIDTH), BF16),
            pltpu.VMEM((t + POOL_HALO, POOL_WIDTH), F32),
            pltpu.VMEM((t, ATTN_WIDTH + POOL_WIDTH), BF16),
            pltpu.VMEM((2, t, d), F32),
            pltpu.VMEM((t, d), BF16),
            pltpu.VMEM((t, d_ff), BF16),
        ],
    )
    return pl.pallas_call(
        functools.partial(_layer_kernel, tiles_per_seq=tiles_per_seq),
        grid_spec=grid_spec,
        out_shape=jax.ShapeDtypeStruct((b, s, d), F32),
        compiler_params=pltpu.CompilerParams(
            dimension_semantics=("arbitrary",), vmem_limit_bytes=VMEM_LIMIT_BYTES),
        name="layer",
    )(sinks, x, pos_rows, freq, mod, norm1.reshape(1, d), norm2.reshape(1, d), norm_f.reshape(1, d),
      pool_scale.reshape(1, -1), w_pool, w_in, w_out,
      w_gate, w_up, w_down)


def kernel(x, c, positions, w_ada, b_ada, norm1, w_in, sinks, w_pool, pool_scale,
           w_out, norm2, w_gate, w_up, w_down, norm_f):
    mod = _adaln_mod(c, w_ada, b_ada)
    return _layer(x, mod, positions, sinks, norm1, norm2, norm_f, w_in, w_pool, pool_scale,
                  w_out, w_gate, w_up, w_down)
```

```python
import functools
import math

import jax
import jax.numpy as jnp
from jax import lax
from jax.experimental import pallas as pl
from jax.experimental.pallas import tpu as pltpu

F32 = jnp.float32
BF16 = jnp.bfloat16

D_MODEL = 1024
HEAD_DIM = 64
N_HEADS = 8
N_KV_HEADS = 2
GROUP = N_HEADS // N_KV_HEADS
ATTN_WIDTH = N_HEADS * HEAD_DIM
KV_WIDTH = N_KV_HEADS * HEAD_DIM
POOL_WINDOWS = (2, 4, 8, 16)
POOL_GROUP_WIDTH = 128
POOL_WIDTH = POOL_GROUP_WIDTH * len(POOL_WINDOWS)
IN_PROJ_WIDTH = ATTN_WIDTH + 2 * KV_WIDTH + POOL_WIDTH
WINDOW = 128
ROPE_THETA = 10000.0
N_MOD = 6
RMS_EPS = 1e-6
LOG2_E = math.log2(math.e)
HALF = HEAD_DIM // 2

LANES = 128
POOL_HALO = 16
VMEM_LIMIT_BYTES = 56 * 1024 * 1024

TOKEN_TILE = 512
FF_CHUNK = 256
DOWN_CHUNK = 256
CAST_CHUNK_ROWS = 128
CAST_SLOTS = 6
CAST_ROWS_PER_ITER = 16
ATTN_SKEW = 3
OUT_PROJ_LAG = 2
OUT_PROJ_BLOCKS = 2
FIRST_CHUNK_ROW_BLOCKS = 2
BACK_SPAN = 0.9


def _const_spec(shape):
    zeros = (0,) * len(shape)
    return pl.BlockSpec(shape, lambda *_: zeros, pipeline_mode=pl.Buffered(1))


def _rms_scale(x):
    return lax.rsqrt(jnp.mean(x * x, axis=-1, keepdims=True) + RMS_EPS)


def _interleave(front, back):
    keyed = [((k + 0.5) / len(front), 0, k, f) for k, f in enumerate(front)]
    keyed += [(BACK_SPAN * (k + 0.5) / len(back), 1, k, f) for k, f in enumerate(back)]
    return [f for *_, f in sorted(keyed, key=lambda e: e[:3])]


def _mod_kernel(c_ref, w_ref, b_ref, o_ref):
    c = c_ref[...]
    n_seq = c.shape[0]
    sc = c * jax.nn.sigmoid(c)
    sc = jnp.concatenate([sc, jnp.zeros((8 - n_seq, c.shape[1]), F32)], axis=0).T
    w = w_ref[...]
    for r in range(n_seq):
        o_ref[0, r:r + 1, :] = jnp.sum(w * sc[:, r:r + 1], axis=0, keepdims=True) + b_ref[...]


def _adaln_mod(c, w_ada, b_ada):
    b, d = c.shape
    return pl.pallas_call(
        _mod_kernel,
        grid=(N_MOD,),
        in_specs=[
            pl.BlockSpec((b, d), lambda j: (0, 0)),
            pl.BlockSpec((d, d), lambda j: (0, j)),
            pl.BlockSpec((1, d), lambda j: (0, j)),
        ],
        out_specs=pl.BlockSpec((1, b, d), lambda j: (j, 0, 0)),
        out_shape=jax.ShapeDtypeStruct((N_MOD, b, d), F32),
        compiler_params=pltpu.CompilerParams(dimension_semantics=("arbitrary",)),
        name="adaln_mod",
    )(c, w_ada, b_ada.reshape(1, -1))


def _dot_f32(a, b):
    ah = a.astype(BF16)
    al = (a - ah.astype(F32)).astype(BF16)
    bh = b.astype(BF16)
    bl = (b - bh.astype(F32)).astype(BF16)
    dot = functools.partial(jnp.dot, preferred_element_type=F32)
    return dot(ah, bh) + dot(ah, bl) + dot(al, bh)


def _weight_cast_tasks(triples, stage_ref, sems):
    chunks = [(src, dst, c, fold) for src, dst, fold in triples
              for c in range(src.shape[0] // CAST_CHUNK_ROWS)]
    slots = stage_ref.shape[0]

    def copy(k):
        src, _, c, _ = chunks[k]
        return pltpu.make_async_copy(
            src.at[pl.ds(c * CAST_CHUNK_ROWS, CAST_CHUNK_ROWS), :],
            stage_ref.at[k % slots, :, pl.ds(0, src.shape[1])],
            sems.at[k % slots])

    def prime():
        for k in range(min(slots - 1, len(chunks))):
            copy(k).start()

    def cast(k):
        def task():
            src, dst, c, fold = chunks[k]
            if k + slots - 1 < len(chunks):
                copy(k + slots - 1).start()
            copy(k).wait()
            folded = fold(c, stage_ref.at[k % slots, :, pl.ds(0, src.shape[1])]) if fold else None
            if folded is not None:
                dst[c * CAST_CHUNK_ROWS:(c + 1) * CAST_CHUNK_ROWS, :] = folded.astype(BF16)
                return

            def cast_rows(r, carry):
                off = pl.multiple_of(r * CAST_ROWS_PER_ITER, CAST_ROWS_PER_ITER)
                dst[pl.ds(c * CAST_CHUNK_ROWS + off, CAST_ROWS_PER_ITER), :] = (
                    stage_ref[k % slots, pl.ds(off, CAST_ROWS_PER_ITER), 0:src.shape[1]]
                    .astype(BF16))
                return carry

            lax.fori_loop(0, CAST_CHUNK_ROWS // CAST_ROWS_PER_ITER, cast_rows, 0)
        return task

    return prime, [cast(k) for k in range(len(chunks))]


def _layer_kernel(sinks_ref, x_ref, pos_ref, freq_ref, mod_ref, n1_ref, n2_ref, nf_ref,
                  pscale_ref, wpool_ref, win_hbm, wout_hbm, wg_hbm, wu_hbm, wd_hbm,
                  o_ref,
                  win_ref, wout_ref, wg_ref, wu_ref, wd_ref, stage_ref, cast_sems,
                  cos_ref, sin_ref, h_ref, q_ref, kt_ref, v_ref, ext_ref, mix_ref, x1_ref, h2_ref,
                  act_ref, *, tiles_per_seq):
    t = x_ref.shape[1]
    s = pl.program_id(0)
    n_tiles = pl.num_programs(0) - 1
    tile_f = jnp.minimum(s, n_tiles - 1)
    first = (tile_f % tiles_per_seq) == 0
    slot_f = s % 2
    slot_b = 1 - slot_f
    seq_f = tile_f // tiles_per_seq
    seq_b = jnp.maximum(s - 1, 0) // tiles_per_seq

    def mod_row(k, seq):
        return mod_ref[k, pl.ds(seq, 1), :]

    lane = lax.broadcasted_iota(jnp.int32, (1, LANES), 1)
    low_lanes = lane < HEAD_DIM
    first_half = (lane % HEAD_DIM) < HALF

    def f_norm():
        x = x_ref[0]
        scale = n1_ref[...] * (1.0 + mod_row(1, seq_f))
        h_ref[...] = (x * _rms_scale(x) * scale + mod_row(0, seq_f)).astype(BF16)

    def in_proj(cols):
        return jnp.dot(h_ref[...], win_ref[:, cols], preferred_element_type=F32)

    def f_trig():
        per_row = LANES // HALF
        pos = pos_ref[...].astype(F32)
        pos = jnp.concatenate([jnp.broadcast_to(pos[:, m:m + 1], (t // per_row, HALF))
                               for m in range(per_row)], axis=-1)
        ang = pos * freq_ref[...]
        cos, sin = jnp.cos(ang), jnp.sin(ang)
        for m in range(per_row):
            c32 = cos[:, m * HALF:(m + 1) * HALF]
            s32 = sin[:, m * HALF:(m + 1) * HALF]
            rows = pl.ds(m, t // per_row, stride=per_row)
            cos_ref[rows, :] = jnp.concatenate([c32, c32, c32, c32], axis=-1)
            sin_ref[rows, :] = jnp.concatenate([-s32, s32, -s32, s32], axis=-1)

    def rope(tile):
        partner = jnp.where(first_half,
                            pltpu.roll(tile, LANES - HALF, 1),
                            pltpu.roll(tile, HALF, 1))
        return tile * cos_ref[...] + partner * sin_ref[...]

    def f_q():
        u = in_proj(slice(0, ATTN_WIDTH))
        q_scale = LOG2_E / math.sqrt(HEAD_DIM)
        for j in range(ATTN_WIDTH // LANES):
            cols = slice(j * LANES, (j + 1) * LANES)
            q_ref[:, cols] = (rope(u[:, cols]) * q_scale).astype(BF16)

    def f_kv():
        kt_ref[:, 0:WINDOW] = kt_ref[:, t:]
        v_ref[0:WINDOW, :] = v_ref[t:, :]
        u = in_proj(slice(ATTN_WIDTH, ATTN_WIDTH + 2 * KV_WIDTH))
        kt_ref[:, WINDOW:] = rope(u[:, :KV_WIDTH]).T.astype(BF16)
        vv = u[:, KV_WIDTH:]
        for g in range(N_KV_HEADS):
            vg = vv[:, g * HEAD_DIM:(g + 1) * HEAD_DIM]
            cols = slice(g * LANES, (g + 1) * LANES)
            v_ref[WINDOW:, cols] = jnp.concatenate([vg, vg], axis=-1).astype(BF16)

    def f_up():
        halo = ext_ref[t:, :]
        ext_ref[0:POOL_HALO, :] = jnp.where(first, jnp.zeros_like(halo), halo)
        ext_ref[POOL_HALO:, :] = in_proj(slice(ATTN_WIDTH + 2 * KV_WIDTH, IN_PROJ_WIDTH))

    def f_pool(gi, w):
        def task():
            cols = slice(gi * POOL_GROUP_WIDTH, (gi + 1) * POOL_GROUP_WIDTH)
            pos_in_seq = ((tile_f % tiles_per_seq) * t
                          + lax.broadcasted_iota(jnp.int32, (t, 1), 0))
            acc = ext_ref[:, cols]
            span = 1
            while span < w:
                acc = acc + pltpu.roll(acc, span, 0)
                span *= 2
            tok = ext_ref[POOL_HALO:, cols]
            total = acc[POOL_HALO:]
            count = jnp.minimum(pos_in_seq + 1, w).astype(F32)
            mix_ref[:, ATTN_WIDTH + gi * POOL_GROUP_WIDTH:
                    ATTN_WIDTH + (gi + 1) * POOL_GROUP_WIDTH] = (total / count - tok).astype(BF16)
        return task

    n_blocks = t // WINDOW
    chunks = N_HEADS // 2
    bodies = [(j, chunk) for j in range(n_blocks) for chunk in range(chunks)]
    kv_cache, logits, probs = {}, {}, {}

    def band_mask(j):
        qi = lax.broadcasted_iota(jnp.int32, (WINDOW, 2 * WINDOW), 0)
        kj = lax.broadcasted_iota(jnp.int32, (WINDOW, 2 * WINDOW), 1)
        rel = kj - WINDOW - qi
        band = (rel <= 0) & (rel > -WINDOW)
        if j == 0:
            band = band & (kj >= jnp.where(first, WINDOW, 0))
        return band

    def block_diag_kv(j, g):
        if (j, g) not in kv_cache:
            keys = slice(j * WINDOW, (j + 2) * WINDOW)
            ktg = kt_ref[g * HEAD_DIM:(g + 1) * HEAD_DIM, keys]
            zk = jnp.zeros_like(ktg)
            k_bd = jnp.concatenate([jnp.concatenate([ktg, zk], axis=1),
                                    jnp.concatenate([zk, ktg], axis=1)], axis=0)
            vd = v_ref[keys, g * LANES:(g + 1) * LANES]
            zv = jnp.zeros_like(vd)
            v_bd = jnp.concatenate([jnp.where(low_lanes, vd, zv),
                                    jnp.where(low_lanes, zv, vd)], axis=0)
            kv_cache[(j, g)] = (k_bd, v_bd)
        return kv_cache[(j, g)]

    def scores(j, chunk):
        k_bd, _ = block_diag_kv(j, chunk // (GROUP // 2))
        qc = q_ref[j * WINDOW:(j + 1) * WINDOW, chunk * LANES:(chunk + 1) * LANES]
        return jnp.dot(qc, k_bd, preferred_element_type=F32)

    def softmax(j, chunk, lg):
        mask = band_mask(j)
        ps, inv_den = [], []
        for hh in range(2):
            sink = sinks_ref[2 * chunk + hh] * LOG2_E
            l = jnp.where(mask, lg[:, hh * 2 * WINDOW:(hh + 1) * 2 * WINDOW], -jnp.inf)
            m = jnp.maximum(jnp.max(l, axis=-1, keepdims=True), sink)
            p = jnp.exp2(l - m)
            den = jnp.sum(p, axis=-1, keepdims=True) + jnp.exp2(sink - m)
            ps.append(p.astype(BF16))
            inv_den.append(1.0 / den)
        return jnp.concatenate(ps, axis=-1), jnp.where(low_lanes, inv_den[0], inv_den[1])

    def values(j, chunk, p, inv_den):
        _, v_bd = block_diag_kv(j, chunk // (GROUP // 2))
        pv = jnp.dot(p, v_bd, preferred_element_type=F32)
        mix_ref[j * WINDOW:(j + 1) * WINDOW, chunk * LANES:(chunk + 1) * LANES] = (
            (pv * inv_den).astype(BF16))

    def out_proj(group):
        rows = slice(group * OUT_PROJ_BLOCKS * WINDOW, (group + 1) * OUT_PROJ_BLOCKS * WINDOW)
        mixed = jnp.dot(mix_ref[rows, :], wout_ref[...], preferred_element_type=F32)
        x1_ref[slot_f, rows, :] = x_ref[0, rows, :] + mod_row(2, seq_f) * mixed

    def f_attn(step):
        def task():
            n = len(bodies)
            if step < n:
                logits[step] = scores(*bodies[step])
            if 0 <= step - 1 < n:
                probs[step - 1] = softmax(*bodies[step - 1], logits.pop(step - 1))
            done = step - ATTN_SKEW
            if 0 <= done < n:
                values(*bodies[done], *probs.pop(done))
            ready = step - ATTN_SKEW - OUT_PROJ_LAG
            if (0 <= ready < n and bodies[ready][1] == chunks - 1
                    and (bodies[ready][0] + 1) % OUT_PROJ_BLOCKS == 0
                    and bodies[ready][0] < n_blocks - OUT_PROJ_BLOCKS):
                out_proj(bodies[ready][0] // OUT_PROJ_BLOCKS)
        return task

    late_out_proj = [functools.partial(out_proj, n_blocks // OUT_PROJ_BLOCKS - 1)]

    front = [f_norm, f_trig, f_q, f_kv, f_up]
    front += [f_pool(gi, w) for gi, w in enumerate(POOL_WINDOWS)]
    front += [f_attn(step) for step in range(len(bodies) + ATTN_SKEW + OUT_PROJ_LAG)]

    def b_norm():
        x1 = x1_ref[slot_b]
        scale = n2_ref[...] * (1.0 + mod_row(4, seq_b))
        h2_ref[...] = (x1 * _rms_scale(x1) * scale + mod_row(3, seq_b)).astype(BF16)

    def b_gate_up(n):
        def task():
            cols = slice(n * FF_CHUNK, (n + 1) * FF_CHUNK)
            row_blocks = FIRST_CHUNK_ROW_BLOCKS if n == 0 else 1
            for rb in range(row_blocks):
                rows = slice(rb * t // row_blocks, (rb + 1) * t // row_blocks)
                g = jnp.dot(h2_ref[rows, :], wg_ref[:, cols], preferred_element_type=F32)
                u = jnp.dot(h2_ref[rows, :], wu_ref[:, cols], preferred_element_type=F32)
                act_ref[rows, cols] = (g * jax.nn.sigmoid(g) * u).astype(BF16)
        return task

    def b_down(n):
        def task():
            cols = slice(n * DOWN_CHUNK, (n + 1) * DOWN_CHUNK)
            ff = jnp.dot(act_ref[...], wd_ref[:, cols], preferred_element_type=F32)
            o_ref[0, :, cols] = x1_ref[slot_b, :, cols] + mod_row(5, seq_b)[:, cols] * ff
        return task

    def b_final():
        x2 = o_ref[0]
        o_ref[0] = x2 * _rms_scale(x2) * nf_ref[...]

    d_ff = wg_ref.shape[1]
    back = [b_norm]
    back += [b_gate_up(n) for n in range(d_ff // FF_CHUNK)]
    back += [b_down(n) for n in range(D_MODEL // DOWN_CHUNK)]
    back += [b_final]

    def run(tasks):
        kv_cache.clear(), logits.clear(), probs.clear()
        for task in tasks:
            task()

    @pl.when(s == 0)
    def _():
        kt_ref[:, t:] = jnp.zeros((KV_WIDTH, WINDOW), BF16)
        v_ref[t:, :] = jnp.zeros((WINDOW, 2 * KV_WIDTH), BF16)
        ext_ref[t:, :] = jnp.zeros((POOL_HALO, POOL_WIDTH), F32)

        def fold_pool(c, rows_ref):
            g = c - ATTN_WIDTH // CAST_CHUNK_ROWS
            if g < 0:
                return None
            cols = slice(g * POOL_GROUP_WIDTH, (g + 1) * POOL_GROUP_WIDTH)
            return _dot_f32(wpool_ref[g] * pscale_ref[:, cols], rows_ref[...])

        prime, casts = _weight_cast_tasks(
            [(win_hbm, win_ref, None), (wout_hbm, wout_ref, fold_pool),
             (wg_hbm, wg_ref, None), (wu_hbm, wu_ref, None), (wd_hbm, wd_ref, None)],
            stage_ref, cast_sems)
        win_casts = win_hbm.shape[0] // CAST_CHUNK_ROWS
        wout_casts = wout_hbm.shape[0] // CAST_CHUNK_ROWS
        order = _interleave(front, casts[win_casts:]) + late_out_proj
        first_out_proj = front[len(front) - len(bodies) + OUT_PROJ_BLOCKS * chunks - 1]
        assert order.index(casts[win_casts + wout_casts - 1]) < order.index(first_out_proj)
        prime()
        run(casts[:win_casts])
        run(order)

    @pl.when(s > 0)
    def _():
        order = [task for task in _interleave(front, back) if task not in back[-2:]]
        run(order + [back[-2], back[-1], late_out_proj[0]])


def _layer(x, mod, positions, sinks, norm1, norm2, norm_f, w_in, w_pool, pool_scale, w_out,
           w_gate, w_up, w_down):
    b, s, d = x.shape
    t = TOKEN_TILE
    d_ff = w_gate.shape[1]
    assert d == D_MODEL and s % t == 0 and t % WINDOW == 0 and d_ff % FF_CHUNK == 0
    assert d_ff % CAST_CHUNK_ROWS == 0 and d % CAST_CHUNK_ROWS == 0
    assert CAST_CHUNK_ROWS == POOL_GROUP_WIDTH
    tiles_per_seq = s // t
    n_tiles = b * tiles_per_seq

    def front_tile(step):
        return jnp.minimum(step, n_tiles - 1)

    def back_tile(step):
        return jnp.maximum(step - 1, 0)

    def tok_map(tile_of):
        return lambda step, *_: (tile_of(step) // tiles_per_seq, tile_of(step) % tiles_per_seq, 0)

    per_row = LANES // HALF
    inv_freq = ROPE_THETA ** (-jnp.arange(HALF, dtype=F32) * (2.0 / HEAD_DIM))
    freq = jnp.tile(inv_freq, per_row).reshape(1, LANES)
    pos_rows = positions.reshape(b * s // per_row, per_row)
    grid_spec = pltpu.PrefetchScalarGridSpec(
        num_scalar_prefetch=1,
        grid=(n_tiles + 1,),
        in_specs=[
            pl.BlockSpec((1, t, d), tok_map(front_tile)),
            pl.BlockSpec((t // per_row, per_row), lambda step, *_: (front_tile(step), 0)),
            _const_spec((1, LANES)),
            _const_spec((N_MOD, b, d)),
            _const_spec((1, d)), _const_spec((1, d)), _const_spec((1, d)),
            _const_spec((1, POOL_WIDTH)),
            _const_spec(w_pool.shape),
        ] + [pl.BlockSpec(memory_space=pl.ANY)] * 5,
        out_specs=pl.BlockSpec((1, t, d), tok_map(back_tile)),
        scratch_shapes=[
            pltpu.VMEM((d, IN_PROJ_WIDTH), BF16),
            pltpu.VMEM((ATTN_WIDTH + POOL_WIDTH, d), BF16),
            pltpu.VMEM((d, d_ff), BF16),
            pltpu.VMEM((d, d_ff), BF16),
            pltpu.VMEM((d_ff, d), BF16),
            pltpu.VMEM((CAST_SLOTS, CAST_CHUNK_ROWS, max(d_ff, IN_PROJ_WIDTH)), F32),
            pltpu.SemaphoreType.DMA((CAST_SLOTS,)),
            pltpu.VMEM((t, LANES), F32),
            pltpu.VMEM((t, LANES), F32),
            pltpu.VMEM((t, d), BF16),
            pltpu.VMEM((t, ATTN_WIDTH), BF16),
            pltpu.VMEM((KV_WIDTH, t + WINDOW), BF16),
            pltpu.VMEM((t + WINDOW, 2 * KV_WIDTH), BF16),
            pltpu.VMEM((t + POOL_HALO, POOL_WIDTH), F32),
            pltpu.VMEM((t, ATTN_WIDTH + POOL_WIDTH), BF16),
            pltpu.VMEM((2, t, d), F32),
            pltpu.VMEM((t, d), BF16),
            pltpu.VMEM((t, d_ff), BF16),
        ],
    )
    return pl.pallas_call(
        functools.partial(_layer_kernel, tiles_per_seq=tiles_per_seq),
        grid_spec=grid_spec,
        out_shape=jax.ShapeDtypeStruct((b, s, d), F32),
        compiler_params=pltpu.CompilerParams(
            dimension_semantics=("arbitrary",), vmem_limit_bytes=VMEM_LIMIT_BYTES),
        name="layer",
    )(sinks, x, pos_rows, freq, mod, norm1.reshape(1, d), norm2.reshape(1, d), norm_f.reshape(1, d),
      pool_scale.reshape(1, -1), w_pool, w_in, w_out,
      w_gate, w_up, w_down)


def kernel(x, c, positions, w_ada, b_ada, norm1, w_in, sinks, w_pool, pool_scale,
           w_out, norm2, w_gate, w_up, w_down, norm_f):
    mod = _adaln_mod(c, w_ada, b_ada)
    return _layer(x, mod, positions, sinks, norm1, norm2, norm_f, w_in, w_pool, pool_scale,
                  w_out, w_gate, w_up, w_down)
```

```python
import functools
import math

import jax
import jax.numpy as jnp
from jax import lax
from jax.experimental import pallas as pl
from jax.experimental.pallas import tpu as pltpu

F32 = jnp.float32
BF16 = jnp.bfloat16

D_MODEL = 1024
HEAD_DIM = 64
N_HEADS = 8
N_KV_HEADS = 2
GROUP = N_HEADS // N_KV_HEADS
ATTN_WIDTH = N_HEADS * HEAD_DIM
KV_WIDTH = N_KV_HEADS * HEAD_DIM
POOL_WINDOWS = (2, 4, 8, 16)
POOL_GROUP_WIDTH = 128
POOL_WIDTH = POOL_GROUP_WIDTH * len(POOL_WINDOWS)
IN_PROJ_WIDTH = ATTN_WIDTH + 2 * KV_WIDTH + POOL_WIDTH
WINDOW = 128
ROPE_THETA = 10000.0
N_MOD = 6
RMS_EPS = 1e-6
LOG2_E = math.log2(math.e)
HALF = HEAD_DIM // 2

LANES = 128
POOL_HALO = 16
VMEM_LIMIT_BYTES = 60 * 1024 * 1024

TOKEN_TILE = 512
FF_CHUNK = 256
DOWN_CHUNK = 256
CAST_CHUNK_ROWS = 128
CAST_SLOTS = 8
CAST_ROWS_PER_ITER = 16
ATTN_SKEW = 3
OUT_PROJ_LAG = 2
OUT_PROJ_BLOCKS = 2
FIRST_CHUNK_ROW_BLOCKS = 2
BACK_SPAN = 0.9


def _const_spec(shape):
    zeros = (0,) * len(shape)
    return pl.BlockSpec(shape, lambda *_: zeros, pipeline_mode=pl.Buffered(1))


def _rms_scale(x):
    return lax.rsqrt(jnp.mean(x * x, axis=-1, keepdims=True) + RMS_EPS)


def _interleave(front, back):
    keyed = [((k + 0.5) / len(front), 0, k, f) for k, f in enumerate(front)]
    keyed += [(BACK_SPAN * (k + 0.5) / len(back), 1, k, f) for k, f in enumerate(back)]
    return [f for *_, f in sorted(keyed, key=lambda e: e[:3])]


def _mod_kernel(c_ref, w_ref, b_ref, o_ref):
    c = c_ref[...]
    n_seq = c.shape[0]
    sc = c * jax.nn.sigmoid(c)
    sc = jnp.concatenate([sc, jnp.zeros((8 - n_seq, c.shape[1]), F32)], axis=0).T
    w = w_ref[...]
    for r in range(n_seq):
        o_ref[0, r:r + 1, :] = jnp.sum(w * sc[:, r:r + 1], axis=0, keepdims=True) + b_ref[...]


def _adaln_mod(c, w_ada, b_ada):
    b, d = c.shape
    return pl.pallas_call(
        _mod_kernel,
        grid=(N_MOD,),
        in_specs=[
            pl.BlockSpec((b, d), lambda j: (0, 0)),
            pl.BlockSpec((d, d), lambda j: (0, j)),
            pl.BlockSpec((1, d), lambda j: (0, j)),
        ],
        out_specs=pl.BlockSpec((1, b, d), lambda j: (j, 0, 0)),
        out_shape=jax.ShapeDtypeStruct((N_MOD, b, d), F32),
        compiler_params=pltpu.CompilerParams(dimension_semantics=("arbitrary",)),
        name="adaln_mod",
    )(c, w_ada, b_ada.reshape(1, -1))


def _dot_f32(a, b):
    ah = a.astype(BF16)
    al = (a - ah.astype(F32)).astype(BF16)
    bh = b.astype(BF16)
    bl = (b - bh.astype(F32)).astype(BF16)
    dot = functools.partial(jnp.dot, preferred_element_type=F32)
    return dot(ah, bh) + dot(ah, bl) + dot(al, bh)


def _weight_cast_tasks(triples, stage_ref, sems):
    chunks = [(src, dst, c, fold) for src, dst, fold in triples
              for c in range(src.shape[0] // CAST_CHUNK_ROWS)]
    slots = stage_ref.shape[0]

    def copy(k):
        src, _, c, _ = chunks[k]
        return pltpu.make_async_copy(
            src.at[pl.ds(c * CAST_CHUNK_ROWS, CAST_CHUNK_ROWS), :],
            stage_ref.at[k % slots, :, pl.ds(0, src.shape[1])],
            sems.at[k % slots])

    def prime():
        for k in range(min(slots - 1, len(chunks))):
            copy(k).start()

    def cast(k):
        def task():
            src, dst, c, fold = chunks[k]
            if k + slots - 1 < len(chunks):
                copy(k + slots - 1).start()
            copy(k).wait()
            folded = fold(c, stage_ref.at[k % slots, :, pl.ds(0, src.shape[1])]) if fold else None
            if folded is not None:
                dst[c * CAST_CHUNK_ROWS:(c + 1) * CAST_CHUNK_ROWS, :] = folded.astype(BF16)
                return

            def cast_rows(r, carry):
                off = pl.multiple_of(r * CAST_ROWS_PER_ITER, CAST_ROWS_PER_ITER)
                dst[pl.ds(c * CAST_CHUNK_ROWS + off, CAST_ROWS_PER_ITER), :] = (
                    stage_ref[k % slots, pl.ds(off, CAST_ROWS_PER_ITER), 0:src.shape[1]]
                    .astype(BF16))
                return carry

            lax.fori_loop(0, CAST_CHUNK_ROWS // CAST_ROWS_PER_ITER, cast_rows, 0)
        return task

    return prime, [cast(k) for k in range(len(chunks))]


def _layer_kernel(sinks_ref, x_ref, pos_ref, freq_ref, mod_ref, n1_ref, n2_ref, nf_ref,
                  pscale_ref, wpool_ref, win_hbm, wout_hbm, wg_hbm, wu_hbm, wd_hbm,
                  o_ref,
                  win_ref, wout_ref, wg_ref, wu_ref, wd_ref, stage_ref, cast_sems,
                  cos_ref, sin_ref, h_ref, q_ref, kt_ref, v_ref, ext_ref, mix_ref, x1_ref, h2_ref,
                  act_ref, *, tiles_per_seq):
    t = x_ref.shape[1]
    s = pl.program_id(0)
    n_tiles = pl.num_programs(0) - 1
    tile_f = jnp.minimum(s, n_tiles - 1)
    first = (tile_f % tiles_per_seq) == 0
    slot_f = s % 2
    slot_b = 1 - slot_f
    seq_f = tile_f // tiles_per_seq
    seq_b = jnp.maximum(s - 1, 0) // tiles_per_seq

    def mod_row(k, seq):
        return mod_ref[k, pl.ds(seq, 1), :]

    lane = lax.broadcasted_iota(jnp.int32, (1, LANES), 1)
    low_lanes = lane < HEAD_DIM
    first_half = (lane % HEAD_DIM) < HALF

    def f_norm():
        x = x_ref[0]
        scale = n1_ref[...] * (1.0 + mod_row(1, seq_f))
        h_ref[...] = (x * _rms_scale(x) * scale + mod_row(0, seq_f)).astype(BF16)

    def in_proj(cols):
        return jnp.dot(h_ref[...], win_ref[:, cols], preferred_element_type=F32)

    def f_trig():
        per_row = LANES // HALF
        pos = pos_ref[...].astype(F32)
        pos = jnp.concatenate([jnp.broadcast_to(pos[:, m:m + 1], (t // per_row, HALF))
                               for m in range(per_row)], axis=-1)
        ang = pos * freq_ref[...]
        cos, sin = jnp.cos(ang), jnp.sin(ang)
        for m in range(per_row):
            c32 = cos[:, m * HALF:(m + 1) * HALF]
            s32 = sin[:, m * HALF:(m + 1) * HALF]
            rows = pl.ds(m, t // per_row, stride=per_row)
            cos_ref[rows, :] = jnp.concatenate([c32, c32, c32, c32], axis=-1)
            sin_ref[rows, :] = jnp.concatenate([-s32, s32, -s32, s32], axis=-1)

    def rope(tile):
        partner = jnp.where(first_half,
                            pltpu.roll(tile, LANES - HALF, 1),
                            pltpu.roll(tile, HALF, 1))
        return tile * cos_ref[...] + partner * sin_ref[...]

    def f_q():
        u = in_proj(slice(0, ATTN_WIDTH))
        q_scale = LOG2_E / math.sqrt(HEAD_DIM)
        for j in range(ATTN_WIDTH // LANES):
            cols = slice(j * LANES, (j + 1) * LANES)
            q_ref[:, cols] = (rope(u[:, cols]) * q_scale).astype(BF16)

    def f_kv():
        kt_ref[:, 0:WINDOW] = kt_ref[:, t:]
        v_ref[0:WINDOW, :] = v_ref[t:, :]
        u = in_proj(slice(ATTN_WIDTH, ATTN_WIDTH + 2 * KV_WIDTH))
        kt_ref[:, WINDOW:] = rope(u[:, :KV_WIDTH]).T.astype(BF16)
        vv = u[:, KV_WIDTH:]
        for g in range(N_KV_HEADS):
            vg = vv[:, g * HEAD_DIM:(g + 1) * HEAD_DIM]
            cols = slice(g * LANES, (g + 1) * LANES)
            v_ref[WINDOW:, cols] = jnp.concatenate([vg, vg], axis=-1).astype(BF16)

    def f_up():
        halo = ext_ref[t:, :]
        ext_ref[0:POOL_HALO, :] = jnp.where(first, jnp.zeros_like(halo), halo)
        ext_ref[POOL_HALO:, :] = in_proj(slice(ATTN_WIDTH + 2 * KV_WIDTH, IN_PROJ_WIDTH))

    def f_pool(gi, w):
        def task():
            cols = slice(gi * POOL_GROUP_WIDTH, (gi + 1) * POOL_GROUP_WIDTH)
            pos_in_seq = ((tile_f % tiles_per_seq) * t
                          + lax.broadcasted_iota(jnp.int32, (t, 1), 0))
            acc = ext_ref[:, cols]
            span = 1
            while span < w:
                acc = acc + pltpu.roll(acc, span, 0)
                span *= 2
            tok = ext_ref[POOL_HALO:, cols]
            total = acc[POOL_HALO:]
            count = jnp.minimum(pos_in_seq + 1, w).astype(F32)
            mix_ref[:, ATTN_WIDTH + gi * POOL_GROUP_WIDTH:
                    ATTN_WIDTH + (gi + 1) * POOL_GROUP_WIDTH] = (total / count - tok).astype(BF16)
        return task

    n_blocks = t // WINDOW
    chunks = N_HEADS // 2
    bodies = [(j, chunk) for j in range(n_blocks) for chunk in range(chunks)]
    kv_cache, logits, probs = {}, {}, {}

    def band_mask(j):
        qi = lax.broadcasted_iota(jnp.int32, (WINDOW, 2 * WINDOW), 0)
        kj = lax.broadcasted_iota(jnp.int32, (WINDOW, 2 * WINDOW), 1)
        rel = kj - WINDOW - qi
        band = (rel <= 0) & (rel > -WINDOW)
        if j == 0:
            band = band & (kj >= jnp.where(first, WINDOW, 0))
        return band

    def block_diag_kv(j, g):
        if (j, g) not in kv_cache:
            keys = slice(j * WINDOW, (j + 2) * WINDOW)
            ktg = kt_ref[g * HEAD_DIM:(g + 1) * HEAD_DIM, keys]
            zk = jnp.zeros_like(ktg)
            k_bd = jnp.concatenate([jnp.concatenate([ktg, zk], axis=1),
                                    jnp.concatenate([zk, ktg], axis=1)], axis=0)
            vd = v_ref[keys, g * LANES:(g + 1) * LANES]
            zv = jnp.zeros_like(vd)
            v_bd = jnp.concatenate([jnp.where(low_lanes, vd, zv),
                                    jnp.where(low_lanes, zv, vd)], axis=0)
            kv_cache[(j, g)] = (k_bd, v_bd)
        return kv_cache[(j, g)]

    def scores(j, chunk):
        k_bd, _ = block_diag_kv(j, chunk // (GROUP // 2))
        qc = q_ref[j * WINDOW:(j + 1) * WINDOW, chunk * LANES:(chunk + 1) * LANES]
        return jnp.dot(qc, k_bd, preferred_element_type=F32)

    def softmax(j, chunk, lg):
        mask = band_mask(j)
        ps, inv_den = [], []
        for hh in range(2):
            sink = sinks_ref[2 * chunk + hh] * LOG2_E
            l = jnp.where(mask, lg[:, hh * 2 * WINDOW:(hh + 1) * 2 * WINDOW], -jnp.inf)
            m = jnp.maximum(jnp.max(l, axis=-1, keepdims=True), sink)
            p = jnp.exp2(l - m)
            den = jnp.sum(p, axis=-1, keepdims=True) + jnp.exp2(sink - m)
            ps.append(p.astype(BF16))
            inv_den.append(1.0 / den)
        return jnp.concatenate(ps, axis=-1), jnp.where(low_lanes, inv_den[0], inv_den[1])

    def values(j, chunk, p, inv_den):
        _, v_bd = block_diag_kv(j, chunk // (GROUP // 2))
        pv = jnp.dot(p, v_bd, preferred_element_type=F32)
        mix_ref[j * WINDOW:(j + 1) * WINDOW, chunk * LANES:(chunk + 1) * LANES] = (
            (pv * inv_den).astype(BF16))

    def out_proj(group):
        rows = slice(group * OUT_PROJ_BLOCKS * WINDOW, (group + 1) * OUT_PROJ_BLOCKS * WINDOW)
        mixed = jnp.dot(mix_ref[rows, :], wout_ref[...], preferred_element_type=F32)
        x1_ref[slot_f, rows, :] = x_ref[0, rows, :] + mod_row(2, seq_f) * mixed

    def f_attn(step):
        def task():
            n = len(bodies)
            if step < n:
                logits[step] = scores(*bodies[step])
            if 0 <= step - 1 < n:
                probs[step - 1] = softmax(*bodies[step - 1], logits.pop(step - 1))
            done = step - ATTN_SKEW
            if 0 <= done < n:
                values(*bodies[done], *probs.pop(done))
            ready = step - ATTN_SKEW - OUT_PROJ_LAG
            if (0 <= ready < n and bodies[ready][1] == chunks - 1
                    and (bodies[ready][0] + 1) % OUT_PROJ_BLOCKS == 0
                    and bodies[ready][0] < n_blocks - OUT_PROJ_BLOCKS):
                out_proj(bodies[ready][0] // OUT_PROJ_BLOCKS)
        return task

    late_out_proj = [functools.partial(out_proj, n_blocks // OUT_PROJ_BLOCKS - 1)]

    front = [f_norm, f_trig, f_q, f_kv, f_up]
    front += [f_pool(gi, w) for gi, w in enumerate(POOL_WINDOWS)]
    front += [f_attn(step) for step in range(len(bodies) + ATTN_SKEW + OUT_PROJ_LAG)]

    def b_norm():
        x1 = x1_ref[slot_b]
        scale = n2_ref[...] * (1.0 + mod_row(4, seq_b))
        h2_ref[...] = (x1 * _rms_scale(x1) * scale + mod_row(3, seq_b)).astype(BF16)

    def b_gate_up(n):
        def task():
            cols = slice(n * FF_CHUNK, (n + 1) * FF_CHUNK)
            row_blocks = FIRST_CHUNK_ROW_BLOCKS if n == 0 else 1
            for rb in range(row_blocks):
                rows = slice(rb * t // row_blocks, (rb + 1) * t // row_blocks)
                g = jnp.dot(h2_ref[rows, :], wg_ref[:, cols], preferred_element_type=F32)
                u = jnp.dot(h2_ref[rows, :], wu_ref[:, cols], preferred_element_type=F32)
                act_ref[rows, cols] = (g * jax.nn.sigmoid(g) * u).astype(BF16)
        return task

    def b_down(n):
        def task():
            cols = slice(n * DOWN_CHUNK, (n + 1) * DOWN_CHUNK)
            ff = jnp.dot(act_ref[...], wd_ref[:, cols], preferred_element_type=F32)
            o_ref[0, :, cols] = x1_ref[slot_b, :, cols] + mod_row(5, seq_b)[:, cols] * ff
        return task

    def b_final():
        x2 = o_ref[0]
        o_ref[0] = x2 * _rms_scale(x2) * nf_ref[...]

    d_ff = wg_ref.shape[1]
    back = [b_norm]
    back += [b_gate_up(n) for n in range(d_ff // FF_CHUNK)]
    back += [b_down(n) for n in range(D_MODEL // DOWN_CHUNK)]
    back += [b_final]

    def run(tasks):
        kv_cache.clear(), logits.clear(), probs.clear()
        for task in tasks:
            task()

    @pl.when(s == 0)
    def _():
        kt_ref[:, t:] = jnp.zeros((KV_WIDTH, WINDOW), BF16)
        v_ref[t:, :] = jnp.zeros((WINDOW, 2 * KV_WIDTH), BF16)
        ext_ref[t:, :] = jnp.zeros((POOL_HALO, POOL_WIDTH), F32)

        def fold_pool(c, rows_ref):
            g = c - ATTN_WIDTH // CAST_CHUNK_ROWS
            if g < 0:
                return None
            cols = slice(g * POOL_GROUP_WIDTH, (g + 1) * POOL_GROUP_WIDTH)
            return _dot_f32(wpool_ref[g] * pscale_ref[:, cols], rows_ref[...])

        prime, casts = _weight_cast_tasks(
            [(win_hbm, win_ref, None), (wout_hbm, wout_ref, fold_pool),
             (wg_hbm, wg_ref, None), (wu_hbm, wu_ref, None), (wd_hbm, wd_ref, None)],
            stage_ref, cast_sems)
        front_casts = (win_hbm.shape[0] + wout_hbm.shape[0]) // CAST_CHUNK_ROWS
        prime()
        run(casts[:front_casts])
        run(_interleave(front, casts[front_casts:]) + late_out_proj)

    @pl.when(s > 0)
    def _():
        order = [task for task in _interleave(front, back) if task not in back[-2:]]
        run(order + [back[-2], back[-1], late_out_proj[0]])


def _layer(x, mod, positions, sinks, norm1, norm2, norm_f, w_in, w_pool, pool_scale, w_out,
           w_gate, w_up, w_down):
    b, s, d = x.shape
    t = TOKEN_TILE
    d_ff = w_gate.shape[1]
    assert d == D_MODEL and s % t == 0 and t % WINDOW == 0 and d_ff % FF_CHUNK == 0
    assert d_ff % CAST_CHUNK_ROWS == 0 and d % CAST_CHUNK_ROWS == 0
    assert CAST_CHUNK_ROWS == POOL_GROUP_WIDTH
    tiles_per_seq = s // t
    n_tiles = b * tiles_per_seq

    def front_tile(step):
        return jnp.minimum(step, n_tiles - 1)

    def back_tile(step):
        return jnp.maximum(step - 1, 0)

    def tok_map(tile_of):
        return lambda step, *_: (tile_of(step) // tiles_per_seq, tile_of(step) % tiles_per_seq, 0)

    per_row = LANES // HALF
    inv_freq = ROPE_THETA ** (-jnp.arange(HALF, dtype=F32) * (2.0 / HEAD_DIM))
    freq = jnp.tile(inv_freq, per_row).reshape(1, LANES)
    pos_rows = positions.reshape(b * s // per_row, per_row)
    grid_spec = pltpu.PrefetchScalarGridSpec(
        num_scalar_prefetch=1,
        grid=(n_tiles + 1,),
        in_specs=[
            pl.BlockSpec((1, t, d), tok_map(front_tile)),
            pl.BlockSpec((t // per_row, per_row), lambda step, *_: (front_tile(step), 0)),
            _const_spec((1, LANES)),
            _const_spec((N_MOD, b, d)),
            _const_spec((1, d)), _const_spec((1, d)), _const_spec((1, d)),
            _const_spec((1, POOL_WIDTH)),
            _const_spec(w_pool.shape),
        ] + [pl.BlockSpec(memory_space=pl.ANY)] * 5,
        out_specs=pl.BlockSpec((1, t, d), tok_map(back_tile)),
        scratch_shapes=[
            pltpu.VMEM((d, IN_PROJ_WIDTH), BF16),
            pltpu.VMEM((ATTN_WIDTH + POOL_WIDTH, d), BF16),
            pltpu.VMEM((d, d_ff), BF16),
            pltpu.VMEM((d, d_ff), BF16),
            pltpu.VMEM((d_ff, d), BF16),
            pltpu.VMEM((CAST_SLOTS, CAST_CHUNK_ROWS, max(d_ff, IN_PROJ_WIDTH)), F32),
            pltpu.SemaphoreType.DMA((CAST_SLOTS,)),
            pltpu.VMEM((t, LANES), F32),
            pltpu.VMEM((t, LANES), F32),
            pltpu.VMEM((t, d), BF16),
            pltpu.VMEM((t, ATTN_WIDTH), BF16),
            pltpu.VMEM((KV_WIDTH, t + WINDOW), BF16),
            pltpu.VMEM((t + WINDOW, 2 * KV_WIDTH), BF16),
            pltpu.VMEM((t + POOL_HALO, POOL_WIDTH), F32),
            pltpu.VMEM((t, ATTN_WIDTH + POOL_WIDTH), BF16),
            pltpu.VMEM((2, t, d), F32),
            pltpu.VMEM((t, d), BF16),
            pltpu.VMEM((t, d_ff), BF16),
        ],
    )
    return pl.pallas_call(
        functools.partial(_layer_kernel, tiles_per_seq=tiles_per_seq),
        grid_spec=grid_spec,
        out_shape=jax.ShapeDtypeStruct((b, s, d), F32),
        compiler_params=pltpu.CompilerParams(
            dimension_semantics=("arbitrary",), vmem_limit_bytes=VMEM_LIMIT_BYTES),
        name="layer",
    )(sinks, x, pos_rows, freq, mod, norm1.reshape(1, d), norm2.reshape(1, d), norm_f.reshape(1, d),
      pool_scale.reshape(1, -1), w_pool, w_in, w_out,
      w_gate, w_up, w_down)


def kernel(x, c, positions, w_ada, b_ada, norm1, w_in, sinks, w_pool, pool_scale,
           w_out, norm2, w_gate, w_up, w_down, norm_f):
    mod = _adaln_mod(c, w_ada, b_ada)
    return _layer(x, mod, positions, sinks, norm1, norm2, norm_f, w_in, w_pool, pool_scale,
                  w_out, w_gate, w_up, w_down)
```
